```python
import jax, jax.numpy as jnp
from jax import lax
import numpy as np

D_MODEL = 1024
BATCH = 8
SEQ = 4096
DEPTH = 4

N_EVEN = (DEPTH + 1) // 2
N_ODD = DEPTH // 2
D_PLE = 256
D_FF = 4 * D_MODEL
DN_ALPHA = (2.0 * DEPTH) ** 0.25
DN_BETA = (8.0 * DEPTH) ** -0.25
LN_EPS = 1e-5
NEG = -1e30

A_HEADS = 4
A_DH = D_MODEL // 8
A_W = A_HEADS * A_DH
A_CONV = 4
A_CHUNK = 64

B_HEADS = 8
B_DH = 64
B_W = B_HEADS * B_DH
B_KV = 2
B_HPG = B_HEADS // B_KV
B_KVW = B_KV * B_DH
B_CMP_LEN = 32
B_CMP_STRIDE = 16
B_CMP_HID = 128
B_SEL_BLK = 64
B_SEL_N = 16
B_WIN = 512
B_QBLK = 64

C_HEADS = 8
C_DN = 128
C_DR = 64
C_DV = 128
C_QL = 512
C_KVL = 256
C_IDX_HEADS = 8
C_IDX_DH = 64
C_IDX_DR = 32
C_TOPK = 256
C_QBLK = 128
ROPE_BASE = 10000.0

EVEN_SIZES = (A_W, A_W, A_W, A_W, A_HEADS, A_HEADS, B_W, B_KVW, B_KVW, B_KVW, B_KVW, B_KVW, B_KVW, 3 * B_HEADS)
ODD_SIZES = (C_QL, C_KVL, C_DR, C_IDX_DH, C_IDX_HEADS)

kernel_name = 'hybrid_mlstm_nsa_dsa_deepnorm'


def _split(z, sizes):
    out, o = [], 0
    for s in sizes:
        out.append(z[..., o:o + s])
        o += s
    return out


def _layer_norm(x, g, b):
    xf = x.astype(jnp.float32)
    mu = xf.mean(-1, keepdims=True)
    var = jnp.square(xf - mu).mean(-1, keepdims=True)
    return ((xf - mu) * lax.rsqrt(var + LN_EPS) * g + b).astype(x.dtype)


def _rms_norm(x, g):
    xf = x.astype(jnp.float32)
    return (xf * lax.rsqrt(jnp.square(xf).mean(-1, keepdims=True) + LN_EPS) * g).astype(x.dtype)


def _masked_softmax(s, mask):
    s = jnp.where(mask, s.astype(jnp.float32), NEG)
    return jnp.where(mask, jax.nn.softmax(s, axis=-1), 0.0)


def _causal_dwconv(x, w):
    k = w.shape[0]
    return lax.conv_general_dilated(x, w[:, None, :].astype(x.dtype), window_strides=(1,), padding=[(k - 1, 0)],
                                    dimension_numbers=('NWC', 'WIO', 'NWC'), feature_group_count=x.shape[-1])


def _rope_tables(pos, d):
    inv = ROPE_BASE ** (-jnp.arange(0, d, 2, dtype=jnp.float32) / d)
    ang = pos.astype(jnp.float32)[..., None] * inv
    return jnp.cos(ang), jnp.sin(ang)


def _rope(x, cos, sin):
    h = x.shape[-1] // 2
    c, s = cos.astype(x.dtype), sin.astype(x.dtype)
    x1, x2 = x[..., :h], x[..., h:]
    return jnp.concatenate([x1 * c - x2 * s, x1 * s + x2 * c], -1)


def _mlstm(q, k, v, o, ig, fg, conv_w, norm_g):
    B, T, _ = q.shape
    dt = q.dtype
    qk = jax.nn.silu(_causal_dwconv(jnp.concatenate([q, k], -1), conv_w))
    q, k = qk[..., :A_W], qk[..., A_W:]
    N = T // A_CHUNK

    def heads(u):
        return u.astype(jnp.float32).reshape(B, N, A_CHUNK, A_HEADS, A_DH).transpose(1, 0, 3, 2, 4)

    def gate(u):
        return u.astype(jnp.float32).reshape(B, N, A_CHUNK, A_HEADS).transpose(1, 0, 3, 2)

    xs = (heads(q) * A_DH ** -0.5, heads(k), heads(v), gate(ig), jax.nn.log_sigmoid(gate(fg)))
    causal = jnp.tril(jnp.ones((A_CHUNK, A_CHUNK), bool))

    def step(carry, inp):
        C, n, m = carry
        qc, kc, vc, ic, lf = inp
        b = jnp.cumsum(lf, axis=-1)
        D = jnp.where(causal, b[..., :, None] - b[..., None, :] + ic[..., None, :], NEG)
        m_inter = b + m[..., None]
        m_t = jnp.maximum(m_inter, D.max(-1))
        e_inter = jnp.exp(m_inter - m_t)
        S = jnp.einsum('bhld,bhsd->bhls', qc, kc) * jnp.exp(D - m_t[..., None])
        num = e_inter[..., None] * jnp.einsum('bhld,bhvd->bhlv', qc, C) + jnp.einsum('bhls,bhsv->bhlv', S, vc)
        den = e_inter * jnp.einsum('bhld,bhd->bhl', qc, n) + S.sum(-1)
        hc = num / jnp.maximum(jnp.abs(den), jnp.exp(-m_t))[..., None]
        bL = b[..., -1]
        dec = bL[..., None] - b + ic
        m_new = jnp.maximum(bL + m, dec.max(-1))
        w = jnp.exp(dec - m_new[..., None])
        e_st = jnp.exp(bL + m - m_new)
        C_new = e_st[..., None, None] * C + jnp.einsum('bhs,bhsv,bhsd->bhvd', w, vc, kc)
        n_new = e_st[..., None] * n + jnp.einsum('bhs,bhsd->bhd', w, kc)
        return (C_new, n_new, m_new), hc

    init = (jnp.zeros((B, A_HEADS, A_DH, A_DH), jnp.float32), jnp.zeros((B, A_HEADS, A_DH), jnp.float32),
            jnp.zeros((B, A_HEADS), jnp.float32))
    _, hs = lax.scan(step, init, xs)
    hs = hs.transpose(1, 0, 3, 2, 4).reshape(B, T, A_HEADS, A_DH)
    mu = hs.mean(-1, keepdims=True)
    var = jnp.square(hs - mu).mean(-1, keepdims=True)
    hn = ((hs - mu) * lax.rsqrt(var + LN_EPS)).reshape(B, T, A_W) * norm_g
    return (hn * jax.nn.sigmoid(o.astype(jnp.float32))).astype(dt)


def _nsa(q, kc_raw, vc_raw, ks, vs, kw, vw, g_pre, cmp_pos, cmp_w1, cmp_w2):
    B, T, _ = q.shape
    dt = q.dtype
    q = q.reshape(B, T, B_KV, B_HPG, B_DH) * B_DH ** -0.5
    r = lambda u: u.reshape(B, T, B_KV, B_DH)
    M = (T - B_CMP_LEN) // B_CMP_STRIDE + 1
    NSB = T // B_SEL_BLK
    n_sel = min(B_SEL_N, NSB)
    win = jnp.arange(M)[:, None] * B_CMP_STRIDE + jnp.arange(B_CMP_LEN)[None, :]

    def compress(u, j):
        blk = u[:, win] + cmp_pos[j][None, None, :, None, :].astype(dt)
        blk = blk.transpose(0, 1, 3, 2, 4).reshape(B, M, B_KV, B_CMP_LEN * B_DH)
        return jax.nn.gelu(blk @ cmp_w1[j]) @ cmp_w2[j]

    k_cmp = compress(r(kc_raw), 0)
    v_cmp = compress(r(vc_raw), 1)
    cmp_end = jnp.arange(M) * B_CMP_STRIDE + B_CMP_LEN - 1
    mi, jb = jnp.arange(M), jnp.arange(NSB)
    overlap = ((mi[:, None] * B_CMP_STRIDE < (jb[None, :] + 1) * B_SEL_BLK) &
               (mi[:, None] * B_CMP_STRIDE + B_CMP_LEN > jb[None, :] * B_SEL_BLK)).astype(jnp.float32)
    ks_blk = r(ks).reshape(B, NSB, B_SEL_BLK, B_KV, B_DH).transpose(0, 3, 1, 2, 4)
    vs_blk = r(vs).reshape(B, NSB, B_SEL_BLK, B_KV, B_DH).transpose(0, 3, 1, 2, 4)
    kw_pad = jnp.pad(r(kw), ((0, 0), (B_WIN, 0), (0, 0), (0, 0)))
    vw_pad = jnp.pad(r(vw), ((0, 0), (B_WIN, 0), (0, 0), (0, 0)))
    gates = jax.nn.sigmoid(g_pre.astype(jnp.float32)).reshape(B, T, B_KV, B_HPG, 3)
    bi = jnp.arange(B)[:, None, None, None]
    gi = jnp.arange(B_KV)[None, :, None, None]
    wlen = B_WIN + B_QBLK

    def block(c):
        s0 = c * B_QBLK
        tq = s0 + jnp.arange(B_QBLK)
        qb = lax.dynamic_slice_in_dim(q, s0, B_QBLK, 1)
        s = jnp.einsum('bqghd,bmgd->bqghm', qb, k_cmp)
        p_cmp = _masked_softmax(s, (cmp_end[None, :] <= tq[:, None])[None, :, None, None, :])
        o_cmp = jnp.einsum('bqghm,bmgd->bqghd', p_cmp.astype(dt), v_cmp)
        imp = p_cmp.sum(3) @ overlap
        cur = tq[:, None] // B_SEL_BLK
        forced = (jb[None, :] == 0) | (jb[None, :] == cur) | (jb[None, :] == cur - 1)
        valid = jb[None, :] * B_SEL_BLK <= tq[:, None]
        score = jnp.where(forced[None, :, None, :], 1e6, imp)
        score = jnp.where(valid[None, :, None, :], score, NEG)
        _, idx = lax.top_k(score, n_sel)
        idx = idx.transpose(0, 2, 1, 3)
        k_sel = ks_blk[bi, gi, idx].reshape(B, B_KV, B_QBLK, n_sel * B_SEL_BLK, B_DH)
        v_sel = vs_blk[bi, gi, idx].reshape(B, B_KV, B_QBLK, n_sel * B_SEL_BLK, B_DH)
        kpos = (idx[..., None] * B_SEL_BLK + jnp.arange(B_SEL_BLK)).reshape(B, B_KV, B_QBLK, n_sel * B_SEL_BLK)
        s = jnp.einsum('bgqhd,bgqkd->bgqhk', qb.transpose(0, 2, 1, 3, 4), k_sel)
        p = _masked_softmax(s, (kpos <= tq[None, None, :, None])[:, :, :, None, :])
        o_sel = jnp.einsum('bgqhk,bgqkd->bqghd', p.astype(dt), v_sel)
        kwb = lax.dynamic_slice_in_dim(kw_pad, s0, wlen, 1)
        vwb = lax.dynamic_slice_in_dim(vw_pad, s0, wlen, 1)
        wpos = s0 - B_WIN + jnp.arange(wlen)
        wmask = (wpos[None, :] <= tq[:, None]) & (wpos[None, :] > tq[:, None] - B_WIN) & (wpos[None, :] >= 0)
        s = jnp.einsum('bqghd,bkgd->bqghk', qb, kwb)
        p = _masked_softmax(s, wmask[None, :, None, None, :])
        o_win = jnp.einsum('bqghk,bkgd->bqghd', p.astype(dt), vwb)
        gb = lax.dynamic_slice_in_dim(gates, s0, B_QBLK, 1).astype(dt)
        o = gb[..., 0:1] * o_cmp + gb[..., 1:2] * o_sel + gb[..., 2:3] * o_win
        return o.reshape(B, B_QBLK, B_W)

    out = lax.map(block, jnp.arange(T // B_QBLK))
    return out.transpose(1, 0, 2, 3).reshape(B, T, B_W)


def _even_mixer(h, w_in, a_conv, a_i_b, a_f_b, a_norm, b_cmp_pos, b_cmp_w1, b_cmp_w2, b_g_b, w_out):
    aq, ak, av, ao, ai, af, bq, bkc, bvc, bks, bvs, bkw, bvw, bg = _split(h @ w_in, EVEN_SIZES)
    ya = _mlstm(aq, ak, av, ao, ai + a_i_b, af + a_f_b, a_conv, a_norm)
    yb = _nsa(bq, bkc, bvc, bks, bvs, bkw, bvw, bg + b_g_b, b_cmp_pos, b_cmp_w1, b_cmp_w2)
    return jnp.concatenate([ya, yb], -1) @ w_out


def _odd_mixer(h, pos, w_in, q_norm, kv_norm, w_qb, w_uk, w_uv, w_iq, ik_g, ik_b, w_out):
    B, T, _ = h.shape
    dt = h.dtype
    cq, ckv, kr, ik, iw = _split(h @ w_in, ODD_SIZES)
    cq = _rms_norm(cq, q_norm)
    ckv = _rms_norm(ckv, kv_norm)
    cos, sin = _rope_tables(pos, C_DR)
    qf = (cq @ w_qb).reshape(B, T, C_HEADS, C_DN + C_DR)
    q_rope = _rope(qf[..., C_DN:], cos[:, :, None], sin[:, :, None])
    k_rope = _rope(kr, cos, sin)
    q_abs = jnp.einsum('bthd,chd->bthc', qf[..., :C_DN], w_uk)
    qa = jnp.concatenate([q_abs, q_rope], -1) * (C_DN + C_DR) ** -0.5
    kv_cat = jnp.concatenate([ckv, k_rope], -1)
    icos, isin = _rope_tables(pos, C_IDX_DR)
    qi = (cq @ w_iq).reshape(B, T, C_IDX_HEADS, C_IDX_DH)
    qi = jnp.concatenate([_rope(qi[..., :C_IDX_DR], icos[:, :, None], isin[:, :, None]), qi[..., C_IDX_DR:]], -1)
    ki = _layer_norm(ik, ik_g, ik_b)
    ki = jnp.concatenate([_rope(ki[..., :C_IDX_DR], icos, isin), ki[..., C_IDX_DR:]], -1)
    wi = iw * (C_IDX_HEADS ** -0.5 * C_IDX_DH ** -0.5)
    k_sel = min(C_TOPK, T // 4)
    kpos = jnp.arange(T)
    bidx = jnp.arange(B)[:, None, None]

    def block(c):
        s0 = c * C_QBLK
        tq = s0 + jnp.arange(C_QBLK)
        qib = lax.dynamic_slice_in_dim(qi, s0, C_QBLK, 1)
        wib = lax.dynamic_slice_in_dim(wi, s0, C_QBLK, 1)
        isc = jnp.einsum('bqh,bqhs->bqs', wib, jax.nn.relu(jnp.einsum('bqhd,bsd->bqhs', qib, ki))).astype(jnp.float32)
        isc = jnp.where(kpos[None, None, :] <= tq[None, :, None], isc, NEG)
        _, idx = lax.top_k(isc, k_sel)
        g = kv_cat[bidx, idx]
        qab = lax.dynamic_slice_in_dim(qa, s0, C_QBLK, 1)
        s = jnp.einsum('bqhc,bqkc->bqhk', qab, g)
        pr = _masked_softmax(s, (idx <= tq[None, :, None])[:, :, None, :])
        return jnp.einsum('bqhk,bqkc->bqhc', pr.astype(dt), g[..., :C_KVL])

    o_lat = lax.map(block, jnp.arange(T // C_QBLK))
    o_lat = o_lat.transpose(1, 0, 2, 3, 4).reshape(B, T, C_HEADS, C_KVL)
    o = jnp.einsum('bthc,chv->bthv', o_lat, w_uv).reshape(B, T, C_HEADS * C_DV)
    return o @ w_out


def setup_inputs(seed: int = 0) -> dict:
    key = jax.random.key(seed)
    keys = iter(jax.random.split(key, 64))

    def nrm(shape, scale):
        return jax.random.normal(next(keys), shape, jnp.float32) * scale

    E, O, L, D = N_EVEN, N_ODD, DEPTH, D_MODEL
    sD = D ** -0.5
    x = nrm((BATCH, SEQ, D), 1.0)
    p = nrm((DEPTH, BATCH, SEQ, D_PLE), 1.0)
    positions = jnp.broadcast_to(jnp.arange(SEQ, dtype=jnp.int32), (BATCH, SEQ))
    seg = lambda n, sc=1.0: nrm((E, D, n), sD * sc)
    e_w_in = jnp.concatenate([seg(A_W), seg(A_W), seg(A_W, DN_BETA), seg(A_W), seg(A_HEADS), seg(A_HEADS),
                              seg(B_W), seg(B_KVW), seg(B_KVW, DN_BETA), seg(B_KVW), seg(B_KVW, DN_BETA),
                              seg(B_KVW), seg(B_KVW, DN_BETA), seg(3 * B_HEADS)], axis=-1)
    e_a_conv = nrm((E, A_CONV, 2 * A_W), A_CONV ** -0.5)
    e_a_i_b = nrm((E, A_HEADS), 0.1)
    e_a_f_b = jnp.linspace(3.0, 6.0, A_HEADS, dtype=jnp.float32)[None] + nrm((E, A_HEADS), 0.1)
    e_a_norm = 1.0 + nrm((E, A_W), 0.02)
    e_b_cmp_pos = nrm((E, 2, B_CMP_LEN, B_DH), 0.1)
    e_b_cmp_w1 = nrm((E, 2, B_CMP_LEN * B_DH, B_CMP_HID), (B_CMP_LEN * B_DH) ** -0.5)
    e_b_cmp_w2 = nrm((E, 2, B_CMP_HID, B_DH), B_CMP_HID ** -0.5) * jnp.array([1.0, DN_BETA], jnp.float32)[None, :, None, None]
    e_b_g_b = nrm((E, 3 * B_HEADS), 0.1)
    e_w_out = nrm((E, A_W + B_W, D), (A_W + B_W) ** -0.5 * DN_BETA)
    o_w_in = jnp.concatenate([nrm((O, D, n), sD) for n in ODD_SIZES], axis=-1)
    o_q_norm = 1.0 + nrm((O, C_QL), 0.02)
    o_kv_norm = 1.0 + nrm((O, C_KVL), 0.02)
    o_w_qb = nrm((O, C_QL, C_HEADS * (C_DN + C_DR)), C_QL ** -0.5)
    o_w_uk = nrm((O, C_KVL, C_HEADS, C_DN), C_KVL ** -0.5)
    o_w_uv = nrm((O, C_KVL, C_HEADS, C_DV), C_KVL ** -0.5 * DN_BETA)
    o_w_iq = nrm((O, C_QL, C_IDX_HEADS * C_IDX_DH), C_QL ** -0.5)
    o_ik_g = 1.0 + nrm((O, C_IDX_DH), 0.02)
    o_ik_b = nrm((O, C_IDX_DH), 0.02)
    o_w_out = nrm((O, C_HEADS * C_DV, D), (C_HEADS * C_DV) ** -0.5 * DN_BETA)
    ln1_g = 1.0 + nrm((L, D), 0.02)
    ln1_b = nrm((L, D), 0.02)
    ln2_g = 1.0 + nrm((L, D), 0.02)
    ln2_b = nrm((L, D), 0.02)
    mlp_w1 = nrm((L, D, D_FF), sD * DN_BETA)
    mlp_w2 = nrm((L, D_FF, D), D_FF ** -0.5 * DN_BETA)
    ple_gate_w = nrm((L, D, D), sD)
    ple_w = nrm((L, D_PLE, D), D_PLE ** -0.5)
    return {'x': x, 'p': p, 'positions': positions,
            'e_w_in': e_w_in, 'e_a_conv': e_a_conv, 'e_a_i_b': e_a_i_b, 'e_a_f_b': e_a_f_b, 'e_a_norm': e_a_norm,
            'e_b_cmp_pos': e_b_cmp_pos, 'e_b_cmp_w1': e_b_cmp_w1, 'e_b_cmp_w2': e_b_cmp_w2, 'e_b_g_b': e_b_g_b,
            'e_w_out': e_w_out,
            'o_w_in': o_w_in, 'o_q_norm': o_q_norm, 'o_kv_norm': o_kv_norm, 'o_w_qb': o_w_qb, 'o_w_uk': o_w_uk,
            'o_w_uv': o_w_uv, 'o_w_iq': o_w_iq, 'o_ik_g': o_ik_g, 'o_ik_b': o_ik_b, 'o_w_out': o_w_out,
            'ln1_g': ln1_g, 'ln1_b': ln1_b, 'ln2_g': ln2_g, 'ln2_b': ln2_b, 'mlp_w1': mlp_w1, 'mlp_w2': mlp_w2,
            'ple_gate_w': ple_gate_w, 'ple_w': ple_w}


def reference(x, p, positions,
              e_w_in, e_a_conv, e_a_i_b, e_a_f_b, e_a_norm, e_b_cmp_pos, e_b_cmp_w1, e_b_cmp_w2, e_b_g_b, e_w_out,
              o_w_in, o_q_norm, o_kv_norm, o_w_qb, o_w_uk, o_w_uv, o_w_iq, o_ik_g, o_ik_b, o_w_out,
              ln1_g, ln1_b, ln2_g, ln2_b, mlp_w1, mlp_w2, ple_gate_w, ple_w):
    h = x
    for i in range(DEPTH):
        j = i // 2
        if i % 2 == 0:
            y = _even_mixer(h, e_w_in[j], e_a_conv[j], e_a_i_b[j], e_a_f_b[j], e_a_norm[j], e_b_cmp_pos[j],
                            e_b_cmp_w1[j], e_b_cmp_w2[j], e_b_g_b[j], e_w_out[j])
        else:
            y = _odd_mixer(h, positions, o_w_in[j], o_q_norm[j], o_kv_norm[j], o_w_qb[j], o_w_uk[j], o_w_uv[j],
                           o_w_iq[j], o_ik_g[j], o_ik_b[j], o_w_out[j])
        h = _layer_norm(DN_ALPHA * h + y, ln1_g[i], ln1_b[i])
        u = jnp.square(jax.nn.relu(h @ mlp_w1[i])) @ mlp_w2[i]
        h = _layer_norm(DN_ALPHA * h + u, ln2_g[i], ln2_b[i])
        h = h + jax.nn.sigmoid(h @ ple_gate_w[i]) * (p[i] @ ple_w[i])
    return h
```

```python
import functools

import numpy as np
import jax
import jax.numpy as jnp
from jax import lax
from jax.experimental import pallas as pl
from jax.experimental.pallas import tpu as pltpu

D_MODEL = 1024
DEPTH = 4
D_PLE = 256
D_FF = 4 * D_MODEL
DN_ALPHA = (2.0 * DEPTH) ** 0.25
LN_EPS = 1e-5
NEG = -1e30

A_HEADS = 4
A_DH = D_MODEL // 8
A_W = A_HEADS * A_DH
A_CONV = 4

B_HEADS = 8
B_DH = 64
B_KV = 2
B_HPG = B_HEADS // B_KV
B_CMP_LEN = 32
B_CMP_STRIDE = 16
B_CMP_HID = 128
B_SEL_BLK = 64
B_SEL_N = 16
B_WIN = 512

C_HEADS = 8
C_DN = 128
C_DR = 64
C_DV = 128
C_QL = 512
C_KVL = 256
C_IDX_HEADS = 8
C_IDX_DH = 64
C_IDX_DR = 32
C_TOPK = 256
ROPE_BASE = 10000.0

LANES = 128
SUBLANES = 8
VMEM_LIMIT_BYTES = 56 * 2**20
MXU_DT = jnp.bfloat16
INT_MIN = -2**31

MLSTM_CHUNK = 64
ROW_TILE = 256
Q_BLOCK = 128
KEY_CHUNK = 512
C_KPAD = 384


def _cparams(*sem):
    return pltpu.CompilerParams(dimension_semantics=sem, vmem_limit_bytes=VMEM_LIMIT_BYTES)


def _const_spec(shape):
    nd = len(shape)
    return pl.BlockSpec(shape, lambda *_: (0,) * nd, pipeline_mode=pl.Buffered(1))


def _dot(a, b):
    return jnp.dot(a, b, preferred_element_type=jnp.float32)


def _dot_nt(a, b):
    return lax.dot_general(a, b, (((1,), (1,)), ((), ())), preferred_element_type=jnp.float32)


def _dot_tn(a, b):
    return lax.dot_general(a, b, (((0,), (0,)), ((), ())), preferred_element_type=jnp.float32)


def _dot_f32(a, b):
    return jnp.dot(a, b, preferred_element_type=jnp.float32, precision=lax.Precision.HIGHEST)


def _layer_norm(x, g, b):
    mu = jnp.mean(x, axis=-1, keepdims=True)
    var = jnp.mean(jnp.square(x - mu), axis=-1, keepdims=True)
    return (x - mu) * lax.rsqrt(var + LN_EPS) * g + b


def _masked_softmax(s, mask):
    s = jnp.where(mask, s, NEG)
    m = jnp.max(s, axis=-1, keepdims=True)
    e = jnp.exp(s - m)
    p = e / jnp.sum(e, axis=-1, keepdims=True)
    return jnp.where(mask, p, 0.0)


def _iota(shape, dim):
    return lax.broadcasted_iota(jnp.int32, shape, dim)


def _pad_heads(w, n_heads, dh, axis=-1):
    axis = axis % w.ndim
    shp = w.shape[:axis] + (n_heads, dh) + w.shape[axis + 1:]
    w = w.reshape(shp)
    pad = [(0, 0)] * w.ndim
    pad[axis + 1] = (0, LANES - dh)
    w = jnp.pad(w, pad)
    return w.reshape(shp[:axis] + (n_heads * LANES,) + shp[axis + 2:])


def _pad_cols(w, width):
    return jnp.pad(w, [(0, 0)] * (w.ndim - 1) + [(0, width - w.shape[-1])])


def _even_proj_kernel(h_ref, w_ref, qk_ref, av_ref, ao_ref, sm_ref, bq_ref, bc_ref, bs_ref):
    z = _dot(h_ref[0].astype(MXU_DT), w_ref[...])
    o = 0
    qk_ref[0] = z[:, o:o + 2 * A_W]; o += 2 * A_W
    av_ref[0] = z[:, o:o + A_W].astype(av_ref.dtype); o += A_W
    ao_ref[0] = z[:, o:o + A_W]; o += A_W
    sm_ref[0] = z[:, o:o + LANES]; o += LANES
    for hd in range(B_HEADS):
        bq_ref[0, hd] = z[:, o:o + LANES].astype(bq_ref.dtype); o += LANES
    for j in range(2 * B_KV):
        bc_ref[0, j] = z[:, o:o + LANES]; o += LANES
    for j in range(4 * B_KV):
        bs_ref[0, j] = z[:, o:o + LANES].astype(bs_ref.dtype); o += LANES


def _even_proj(h3, w_aug, tm):
    B, T, D = h3.shape
    n = w_aug.shape[1]
    f32 = jnp.float32
    out_shape = (
        jax.ShapeDtypeStruct((B, T, 2 * A_W), f32),
        jax.ShapeDtypeStruct((B, T, A_W), MXU_DT),
        jax.ShapeDtypeStruct((B, T, A_W), f32),
        jax.ShapeDtypeStruct((B, T, LANES), f32),
        jax.ShapeDtypeStruct((B, B_HEADS, T, LANES), MXU_DT),
        jax.ShapeDtypeStruct((B, 2 * B_KV, T, LANES), f32),
        jax.ShapeDtypeStruct((B, 4 * B_KV, T, LANES), MXU_DT),
    )
    row = lambda w: pl.BlockSpec((1, tm, w), lambda b, i: (b, i, 0))
    hm = lambda nh: pl.BlockSpec((1, nh, tm, LANES), lambda b, i: (b, 0, i, 0))
    return pl.pallas_call(
        _even_proj_kernel,
        grid=(B, T // tm),
        in_specs=[row(D), _const_spec((D, n))],
        out_specs=(row(2 * A_W), row(A_W), row(A_W), row(LANES), hm(B_HEADS), hm(2 * B_KV), hm(4 * B_KV)),
        out_shape=out_shape,
        compiler_params=_cparams("parallel", "parallel"),
    )(h3, w_aug)


def _even_w_in_aug(w_in):
    sizes = (A_W, A_W, A_W, A_W, A_HEADS, A_HEADS, B_HEADS * B_DH) + (B_KV * B_DH,) * 6 + (3 * B_HEADS,)
    parts, o = [], 0
    for s in sizes:
        parts.append(w_in[:, o:o + s]); o += s
    aq, ak, av, ao, ai, af, bq, bkc, bvc, bks, bvs, bkw, bvw, bg = parts
    small = _pad_cols(jnp.concatenate([ai, af, bg], -1), LANES)
    ph = lambda w: _pad_heads(w, B_KV, B_DH)
    cols = [aq, ak, av, ao, small, _pad_heads(bq, B_HEADS, B_DH),
            ph(bkc), ph(bvc), ph(bks), ph(bvs), ph(bkw), ph(bvw)]
    return jnp.concatenate(cols, -1).astype(MXU_DT)


def _mlstm_kernel(qk_ref, v_ref, o_ref, gc_ref, gr_ref, cw_ref, bc_ref, br_ref, ng_ref, tri_ref, out_ref,
                  xs_scr, ct_scr, n_scr, m_scr, *, L):
    c = pl.program_id(1)
    f32 = jnp.float32
    W2 = 2 * A_W

    @pl.when(c == 0)
    def _():
        xs_scr[0:SUBLANES, :] = jnp.zeros((SUBLANES, W2), f32)
        ct_scr[...] = jnp.zeros(ct_scr.shape, f32)
        n_scr[...] = jnp.zeros(n_scr.shape, f32)
        m_scr[...] = jnp.zeros(m_scr.shape, f32)

    xs_scr[SUBLANES:SUBLANES + L, :] = qk_ref[0]
    cw = cw_ref[...]
    base = SUBLANES - (A_CONV - 1)
    acc = xs_scr[base:base + L, :] * cw[0:1, :]
    for j in range(1, A_CONV):
        acc = acc + xs_scr[base + j:base + j + L, :] * cw[j:j + 1, :]
    qk = acc * jax.nn.sigmoid(acc)
    xs_scr[0:SUBLANES, :] = xs_scr[L:L + SUBLANES, :]

    gc = gc_ref[0] + bc_ref[...]
    gr = gr_ref[0, 0] + br_ref[...]
    tri = tri_ref[...]
    ig_c = gc[:, 0:A_HEADS]
    b_c = _dot_f32(tri, jax.nn.log_sigmoid(gc[:, A_HEADS:2 * A_HEADS]))
    ig_r = gr[0:A_HEADS, :]
    b_r = _dot_f32(jax.nn.log_sigmoid(gr[A_HEADS:2 * A_HEADS, :]), tri.T)
    causal = _iota((L, L), 1) <= _iota((L, L), 0)

    for hd in range(A_HEADS):
        sl = slice(hd * A_DH, (hd + 1) * A_DH)
        q_h = (qk[:, sl] * A_DH ** -0.5).astype(MXU_DT)
        k_f = qk[:, A_W + hd * A_DH:A_W + (hd + 1) * A_DH]
        k_h = k_f.astype(MXU_DT)
        v_h = v_ref[0][:, sl]
        bi = b_c[:, hd:hd + 1]
        ic = ig_c[:, hd:hd + 1]
        dmat = jnp.where(causal, bi - b_r[hd:hd + 1, :] + ig_r[hd:hd + 1, :], NEG)
        m_prev = m_scr[hd][:, 0:1]
        m_inter = bi + m_prev
        m_t = jnp.maximum(m_inter, jnp.max(dmat, axis=1, keepdims=True))
        e_inter = jnp.exp(m_inter - m_t)
        s = _dot_nt(q_h, k_h) * jnp.exp(dmat - m_t)
        ct = ct_scr[hd]
        nrow = n_scr[hd]
        num = e_inter * _dot(q_h, ct.astype(MXU_DT)) + _dot(s.astype(MXU_DT), v_h)
        den = e_inter * jnp.sum(q_h.astype(f32) * nrow, axis=1, keepdims=True) + jnp.sum(s, axis=1, keepdims=True)
        hc = num / jnp.maximum(jnp.abs(den), jnp.exp(-m_t))
        b_last = bi[L - 1:L, :]
        dec = b_last - bi + ic
        m_new = jnp.maximum(b_last + m_prev, jnp.max(dec, axis=0, keepdims=True))
        wgt = jnp.exp(dec - m_new)
        e_st = jnp.exp(b_last + m_prev - m_new)
        kw = k_f * wgt
        ct_scr[hd] = e_st * ct + _dot_tn(kw.astype(MXU_DT), v_h)
        n_scr[hd] = e_st * nrow + jnp.sum(kw, axis=0, keepdims=True)
        m_scr[hd] = jnp.broadcast_to(m_new, (1, LANES))
        mu = jnp.mean(hc, axis=1, keepdims=True)
        var = jnp.mean(jnp.square(hc - mu), axis=1, keepdims=True)
        hn = (hc - mu) * lax.rsqrt(var + LN_EPS) * ng_ref[:, sl]
        out_ref[0, :, sl] = (hn * jax.nn.sigmoid(o_ref[0][:, sl])).astype(out_ref.dtype)


def _mlstm(qk, av, ao, small, conv_w, i_b, f_b, norm_g):
    B, T, _ = qk.shape
    L = min(MLSTM_CHUNK, T)
    N = T // L
    f32 = jnp.float32
    gates = small[..., 0:2 * A_HEADS]
    gates_r = gates.reshape(B, N, L, 2 * A_HEADS).transpose(0, 1, 3, 2)
    bias = jnp.concatenate([i_b, f_b]).astype(f32)
    tri = jnp.tril(jnp.ones((L, L), f32))
    row = lambda w: pl.BlockSpec((1, L, w), lambda b, c: (b, c, 0))
    kern = functools.partial(_mlstm_kernel, L=L)
    return pl.pallas_call(
        kern,
        grid=(B, N),
        in_specs=[row(2 * A_W), row(A_W), row(A_W), row(2 * A_HEADS),
                  pl.BlockSpec((1, 1, 2 * A_HEADS, L), lambda b, c: (b, c, 0, 0)),
                  _const_spec((A_CONV, 2 * A_W)), _const_spec((1, 2 * A_HEADS)), _const_spec((2 * A_HEADS, 1)),
                  _const_spec((1, A_W)), _const_spec((L, L))],
        out_specs=row(A_W),
        out_shape=jax.ShapeDtypeStruct((B, T, A_W), MXU_DT),
        scratch_shapes=[pltpu.VMEM((L + 2 * SUBLANES, 2 * A_W), f32),
                        pltpu.VMEM((A_HEADS, A_DH, A_DH), f32),
                        pltpu.VMEM((A_HEADS, 1, A_DH), f32),
                        pltpu.VMEM((A_HEADS, 1, LANES), f32)],
        compiler_params=_cparams("parallel", "arbitrary"),
    )(qk, av, ao, gates, gates_r, conv_w, bias[None, :], bias[:, None], norm_g[None, :], tri)


def _nsa_cmp_kernel(x_ref, w1a_ref, w1b_ref, w2_ref, pos_ref, out_ref):
    n = x_ref.shape[2]
    half = x_ref.shape[3]
    for j in range(2):
        bias = (_dot(pos_ref[j, :, 0:half].astype(MXU_DT), w1a_ref[j])
                + _dot(pos_ref[j, :, half:2 * half].astype(MXU_DT), w1b_ref[j]))[0:1, :]
        for g in range(B_KV):
            u = x_ref[0, j * B_KV + g].astype(MXU_DT)
            a = _dot(u, w1a_ref[j])
            bm = _dot(u, w1b_ref[j])
            pre = a + pltpu.roll(bm, n - 1, 0) + bias
            hid = jax.nn.gelu(pre)
            out_ref[0, j * B_KV + g] = _dot(hid.astype(MXU_DT), w2_ref[j])


def _nsa_compress(bc, cmp_pos, cmp_w1, cmp_w2):
    B, _, T, _ = bc.shape
    nblk = T // B_CMP_STRIDE
    half = B_CMP_STRIDE * LANES
    x = bc.reshape(B, 2 * B_KV, nblk, half)
    w1 = jnp.pad(cmp_w1.reshape(2, B_CMP_LEN, B_DH, B_CMP_HID), ((0, 0), (0, 0), (0, LANES - B_DH), (0, 0)))
    w1 = w1.reshape(2, B_CMP_LEN * LANES, B_CMP_HID).astype(MXU_DT)
    w1a, w1b = w1[:, :half], w1[:, half:]
    w2 = _pad_cols(cmp_w2, LANES).astype(MXU_DT)
    pos = jnp.pad(cmp_pos, ((0, 0), (0, 0), (0, LANES - B_DH))).reshape(2, 1, B_CMP_LEN * LANES)
    pos = jnp.broadcast_to(pos, (2, SUBLANES, B_CMP_LEN * LANES))
    return pl.pallas_call(
        _nsa_cmp_kernel,
        grid=(B,),
        in_specs=[pl.BlockSpec((1, 2 * B_KV, nblk, half), lambda b: (b, 0, 0, 0)),
                  _const_spec(w1a.shape), _const_spec(w1b.shape), _const_spec(w2.shape), _const_spec(pos.shape)],
        out_specs=pl.BlockSpec((1, 2 * B_KV, nblk, LANES), lambda b: (b, 0, 0, 0)),
        out_shape=jax.ShapeDtypeStruct((B, 2 * B_KV, nblk, LANES), jnp.float32),
        compiler_params=_cparams("parallel"),
    )(x, w1a, w1b, w2, pos)


def _nsa_kernel(q_ref, kv_ref, cmp_ref, sm_ref, gb_ref, ovt_ref, exp_ref, out_ref, acc_scr,
                *, QB, KC, WL, NSB, NSEL):
    f32 = jnp.float32
    i = pl.program_id(1)
    s0 = i * QB
    R = B_HPG * QB
    tq_col = s0 + _iota((QB, 1), 0)
    tq_row = s0 + _iota((1, QB), 1)
    tq4 = jnp.concatenate([tq_col] * B_HPG, axis=0)
    nch = (s0 + QB + KC - 1) // KC
    gates = jax.nn.sigmoid(sm_ref[0] + gb_ref[...])
    ncmp = cmp_ref.shape[2]
    cmp_end = _iota((1, ncmp), 1) * B_CMP_STRIDE + (B_CMP_LEN - 1)
    jb = _iota((LANES, 1), 0)
    cur = lax.shift_right_logical(tq_row, int(np.log2(B_SEL_BLK)))
    forced = (jb == 0) | (jb == cur) | (jb == cur - 1)
    valid = jb * B_SEL_BLK <= tq_row
    wstart = pl.multiple_of(jnp.maximum(s0 + QB - WL, 0), QB)
    wpos = wstart + _iota((1, WL), 1)
    wmask = (wpos <= tq4) & (wpos > tq4 - B_WIN)

    for g in range(B_KV):
        qs = (q_ref[0, g * B_HPG:(g + 1) * B_HPG].reshape(R, LANES) * (B_DH ** -0.5)).astype(MXU_DT)
        kcm = cmp_ref[0, g].astype(MXU_DT)
        vcm = cmp_ref[0, B_KV + g].astype(MXU_DT)
        p_cmp = _masked_softmax(_dot_nt(qs, kcm), cmp_end <= tq4)
        o_cmp = _dot(p_cmp.astype(MXU_DT), vcm)
        psum = jnp.sum(p_cmp.reshape(B_HPG, QB, ncmp), axis=0)
        imp_t = lax.dot_general(ovt_ref[...], psum, (((1,), (1,)), ((), ())),
                                preferred_element_type=f32, precision=lax.Precision.HIGHEST)
        sc = jnp.where(forced, 1e6, imp_t)
        sc = jnp.where(valid, sc, NEG)
        sc = jnp.where(jb < NSB, sc, -jnp.inf)
        rank = jnp.zeros((LANES, QB), f32)
        for j in range(NSB):
            cj = jnp.broadcast_to(sc[j:j + 1, :], (LANES, QB))
            ahead = (cj > sc) | ((cj == sc) & (jb > j))
            rank = rank + jnp.where(ahead, 1.0, 0.0)
        sel_b = jnp.where(rank < NSEL, 1.0, 0.0).T.astype(MXU_DT)
        acc_scr[...] = jnp.zeros(acc_scr.shape, f32)

        def sel_body(c, carry):
            m_prev, l_prev = carry
            ks = pl.multiple_of(c * KC, KC)
            kc_ = kv_ref[0, 0 * B_KV + g, pl.ds(ks, KC), :]
            vc_ = kv_ref[0, 1 * B_KV + g, pl.ds(ks, KC), :]
            kpos = ks + _iota((1, KC), 1)
            mk = (_dot(sel_b, exp_ref[c]) > 0.5) & (kpos <= tq_col)
            mk3 = mk[None]
            s3 = jnp.where(mk3, _dot_nt(qs, kc_).reshape(B_HPG, QB, KC), NEG)
            m_new = jnp.maximum(m_prev, jnp.max(s3, axis=-1, keepdims=True))
            p = jnp.where(mk3, jnp.exp(s3 - m_new), 0.0)
            alpha = jnp.exp(m_prev - m_new)
            l_new = alpha * l_prev + jnp.sum(p, axis=-1, keepdims=True)
            pv = _dot(p.reshape(R, KC).astype(MXU_DT), vc_)
            acc_scr[...] = alpha.reshape(R, 1) * acc_scr[...] + pv
            return m_new, l_new

        m0 = jnp.full((B_HPG, QB, 1), NEG, f32)
        l0 = jnp.zeros((B_HPG, QB, 1), f32)
        _, l_fin = lax.fori_loop(0, nch, sel_body, (m0, l0))
        o_sel = acc_scr[...] / l_fin.reshape(R, 1)
        kw_ = kv_ref[0, 2 * B_KV + g, pl.ds(wstart, WL), :]
        vw_ = kv_ref[0, 3 * B_KV + g, pl.ds(wstart, WL), :]
        p_win = _masked_softmax(_dot_nt(qs, kw_), wmask)
        o_win = _dot(p_win.astype(MXU_DT), vw_)
        for hd in range(B_HPG):
            c0 = 2 * A_HEADS + (g * B_HPG + hd) * 3
            rs = slice(hd * QB, (hd + 1) * QB)
            o = (gates[:, c0:c0 + 1] * o_cmp[rs] + gates[:, c0 + 1:c0 + 2] * o_sel[rs]
                 + gates[:, c0 + 2:c0 + 3] * o_win[rs])
            col = (g * B_HPG + hd) * LANES
            out_ref[0, :, col:col + LANES] = o.astype(out_ref.dtype)


def _nsa(bq, bs, cmp, small, g_b):
    B, _, T, _ = bq.shape
    QB = min(Q_BLOCK, T)
    KC = min(KEY_CHUNK, T)
    WL = min(B_WIN + QB, T)
    NSB = T // B_SEL_BLK
    NSEL = min(B_SEL_N, NSB)
    ncmp = cmp.shape[2]
    f32 = jnp.float32
    M = (T - B_CMP_LEN) // B_CMP_STRIDE + 1
    mi, jj = np.arange(ncmp)[None, :], np.arange(LANES)[:, None]
    ovt = ((mi * B_CMP_STRIDE < (jj + 1) * B_SEL_BLK) & (mi * B_CMP_STRIDE + B_CMP_LEN > jj * B_SEL_BLK)
           & (mi < M) & (jj < NSB)).astype(np.float32)
    kk = np.arange(T)
    expand = (kk[None, :] // B_SEL_BLK == np.arange(LANES)[:, None]).astype(np.float32)
    expand = jnp.asarray(expand.reshape(LANES, T // KC, KC).transpose(1, 0, 2), MXU_DT)
    gb = jnp.zeros((1, LANES), f32).at[0, 2 * A_HEADS:2 * A_HEADS + 3 * B_HEADS].set(g_b)
    kern = functools.partial(_nsa_kernel, QB=QB, KC=KC, WL=WL, NSB=NSB, NSEL=NSEL)
    return pl.pallas_call(
        kern,
        grid=(B, T // QB),
        in_specs=[pl.BlockSpec((1, B_HEADS, QB, LANES), lambda b, i: (b, 0, i, 0)),
                  pl.BlockSpec((1, 4 * B_KV, T, LANES), lambda b, i: (b, 0, 0, 0)),
                  pl.BlockSpec((1, 2 * B_KV, ncmp, LANES), lambda b, i: (b, 0, 0, 0)),
                  pl.BlockSpec((1, QB, LANES), lambda b, i: (b, i, 0)),
                  _const_spec((1, LANES)), _const_spec(ovt.shape), _const_spec(expand.shape)],
        out_specs=pl.BlockSpec((1, QB, B_HEADS * LANES), lambda b, i: (b, i, 0)),
        out_shape=jax.ShapeDtypeStruct((B, T, B_HEADS * LANES), MXU_DT),
        scratch_shapes=[pltpu.VMEM((B_HPG * QB, LANES), f32)],
        compiler_params=_cparams("parallel", "arbitrary"),
    )(bq, bs, cmp, small, gb, jnp.asarray(ovt), expand)


def _rope_rows(d):
    inv = ROPE_BASE ** (-jnp.arange(0, d, 2, dtype=jnp.float32) / d)
    z = jnp.zeros((C_DR - d,), jnp.float32)
    inv64 = jnp.concatenate([inv, inv, z])
    sgn64 = jnp.concatenate([-jnp.ones(d // 2), jnp.ones(d // 2), z]).astype(jnp.float32)
    return inv64, sgn64


def _odd_prep_kernel(h_ref, pos_ref, win_ref, wqb_ref, wiq_ref, wuk_ref, qn_ref, kvn_ref, ikg_ref, ikb_ref,
                     rope_ref, perm_ref, qa_ref, kv_ref, qi_ref, ki_ref, wi_ref):
    f32 = jnp.float32
    z = _dot(h_ref[0].astype(MXU_DT), win_ref[...])
    pos = pos_ref[0].astype(f32)
    rr = rope_ref[...]
    ang_q = pos * rr[0:1, :]
    cos_q, sin_q = jnp.cos(ang_q), jnp.sin(ang_q) * rr[1:2, :]
    ang_i = pos * rr[2:3, :]
    cos_i, sin_i = jnp.cos(ang_i), jnp.sin(ang_i) * rr[3:4, :]

    def rms(x, g):
        return x * lax.rsqrt(jnp.mean(jnp.square(x), axis=-1, keepdims=True) + LN_EPS) * g

    cq = rms(z[:, 0:C_QL], qn_ref[...])
    ckv = rms(z[:, C_QL:C_QL + C_KVL], kvn_ref[...])
    o = C_QL + C_KVL
    k_rope = z[:, o:o + LANES] * cos_q + z[:, o + LANES:o + 2 * LANES] * sin_q
    kv_ref[0, :, 0:C_KVL] = ckv.astype(kv_ref.dtype)
    kv_ref[0, :, C_KVL:C_KPAD] = k_rope.astype(kv_ref.dtype)
    ik = z[:, o + 2 * LANES:o + 3 * LANES]
    real = _iota((1, LANES), 1) < C_IDX_DH
    mu = jnp.sum(ik, axis=-1, keepdims=True) / C_IDX_DH
    dlt = jnp.where(real, ik - mu, 0.0)
    var = jnp.sum(jnp.square(dlt), axis=-1, keepdims=True) / C_IDX_DH
    ki = dlt * lax.rsqrt(var + LN_EPS) * ikg_ref[...] + ikb_ref[...]
    ki = ki * cos_i + _dot_f32(ki, perm_ref[...]) * sin_i
    ki_ref[0] = ki.astype(ki_ref.dtype)
    wi_ref[0] = z[:, o + 3 * LANES:o + 4 * LANES] * (C_IDX_HEADS ** -0.5 * C_IDX_DH ** -0.5)
    cqb = cq.astype(MXU_DT)
    qf = _dot(cqb, wqb_ref[...])
    qi = _dot(cqb, wiq_ref[...])
    scale = (C_DN + C_DR) ** -0.5
    nh = C_HEADS * LANES
    for hd in range(C_HEADS):
        q_abs = _dot(qf[:, hd * C_DN:(hd + 1) * C_DN].astype(MXU_DT), wuk_ref[hd])
        cs = slice(hd * LANES, (hd + 1) * LANES)
        q_rope = qf[:, nh:2 * nh][:, cs] * cos_q + qf[:, 2 * nh:3 * nh][:, cs] * sin_q
        qa_ref[0, hd, :, 0:C_KVL] = (q_abs * scale).astype(qa_ref.dtype)
        qa_ref[0, hd, :, C_KVL:C_KPAD] = (q_rope * scale).astype(qa_ref.dtype)
        qi_h = qi[:, 0:nh][:, cs] * cos_i + qi[:, nh:2 * nh][:, cs] * sin_i
        qi_ref[0, hd] = qi_h.astype(qi_ref.dtype)


def _rot_cols(w, n_heads, dh, d):
    w = w.reshape(w.shape[0], n_heads, dh)
    h = d // 2
    return jnp.concatenate([w[..., h:d], w[..., 0:h], jnp.zeros_like(w[..., d:])], -1).reshape(w.shape[0], n_heads * dh)


def _odd_prep(h3, pos3, w_in, q_norm, kv_norm, w_qb, w_uk, w_iq, ik_g, ik_b, tm):
    B, T, D = h3.shape
    f32 = jnp.float32
    o = 0
    parts = []
    for s in (C_QL, C_KVL, C_DR, C_IDX_DH, C_IDX_HEADS):
        parts.append(w_in[:, o:o + s]); o += s
    w_cq, w_ckv, w_kr, w_ik, w_iw = parts
    pc = lambda w: _pad_cols(w, LANES)
    win = jnp.concatenate([w_cq, w_ckv, pc(w_kr), pc(_rot_cols(w_kr, 1, C_DR, C_DR)), pc(w_ik), pc(w_iw)],
                          -1).astype(MXU_DT)
    wq = w_qb.reshape(C_QL, C_HEADS, C_DN + C_DR)
    w_nope = wq[..., :C_DN].reshape(C_QL, C_HEADS * C_DN)
    w_rope = wq[..., C_DN:].reshape(C_QL, C_HEADS * C_DR)
    wqb = jnp.concatenate([w_nope, _pad_heads(w_rope, C_HEADS, C_DR),
                           _pad_heads(_rot_cols(w_rope, C_HEADS, C_DR, C_DR), C_HEADS, C_DR)], -1).astype(MXU_DT)
    wiq = jnp.concatenate([_pad_heads(w_iq, C_IDX_HEADS, C_IDX_DH),
                           _pad_heads(_rot_cols(w_iq, C_IDX_HEADS, C_IDX_DH, C_IDX_DR), C_IDX_HEADS, C_IDX_DH)],
                          -1).astype(MXU_DT)
    wuk = w_uk.transpose(1, 2, 0).astype(MXU_DT)
    inv_q, sgn_q = _rope_rows(C_DR)
    inv_i, sgn_i = _rope_rows(C_IDX_DR)
    rope = jnp.stack([jnp.tile(v, LANES // C_DR) for v in (inv_q, sgn_q, inv_i, sgn_i)])
    rope = jnp.concatenate([rope, jnp.zeros((SUBLANES - 4, LANES), f32)])
    hh = C_IDX_DR // 2
    src = np.arange(LANES)
    src[:hh] += hh
    src[hh:C_IDX_DR] -= hh
    perm = np.zeros((LANES, LANES), np.float32)
    perm[src, np.arange(LANES)] = 1.0
    ikg = _pad_cols(ik_g[None, :], LANES)
    ikb = _pad_cols(ik_b[None, :], LANES)
    out_shape = (
        jax.ShapeDtypeStruct((B, C_HEADS, T, C_KPAD), MXU_DT),
        jax.ShapeDtypeStruct((B, T, C_KPAD), MXU_DT),
        jax.ShapeDtypeStruct((B, C_IDX_HEADS, T, LANES), MXU_DT),
        jax.ShapeDtypeStruct((B, T, LANES), MXU_DT),
        jax.ShapeDtypeStruct((B, T, LANES), f32),
    )
    row = lambda w: pl.BlockSpec((1, tm, w), lambda b, i: (b, i, 0))
    hm = lambda w: pl.BlockSpec((1, C_HEADS, tm, w), lambda b, i: (b, 0, i, 0))
    return pl.pallas_call(
        _odd_prep_kernel,
        grid=(B, T // tm),
        in_specs=[row(D), row(1), _const_spec(win.shape), _const_spec(wqb.shape), _const_spec(wiq.shape),
                  _const_spec(wuk.shape), _const_spec((1, C_QL)), _const_spec((1, C_KVL)),
                  _const_spec((1, LANES)), _const_spec((1, LANES)), _const_spec(rope.shape),
                  _const_spec(perm.shape)],
        out_specs=(hm(C_KPAD), row(C_KPAD), hm(LANES), row(LANES), row(LANES)),
        out_shape=out_shape,
        compiler_params=_cparams("parallel", "parallel"),
    )(h3, pos3, win, wqb, wiq, wuk, q_norm[None, :], kv_norm[None, :], ikg, ikb, rope, jnp.asarray(perm))


def _dsa_kernel(qa_ref, qi_ref, wi_ref, kv_ref, ki_ref, wuv_ref, out_ref, key_scr, tie_scr, acc_scr,
                *, QB, KC, TOPK):
    f32, i32 = jnp.float32, jnp.int32
    i = pl.program_id(1)
    s0 = i * QB
    H = C_HEADS
    R = H * QB
    nch = (s0 + QB + KC - 1) // KC
    tq = s0 + _iota((QB, 1), 0)
    wv = wi_ref[0]
    wcol = jnp.concatenate([wv[:, hd:hd + 1] for hd in range(C_IDX_HEADS)], axis=0)
    qi = qi_ref[0].reshape(R, LANES)

    def idx_body(c, _):
        ks = pl.multiple_of(c * KC, KC)
        s = jnp.maximum(_dot_nt(qi, ki_ref[0, pl.ds(ks, KC), :]), 0.0) * wcol
        isc = jnp.sum(s.reshape(H, QB, KC), axis=0)
        isc = jnp.where(isc == 0.0, 0.0, isc)
        kpos = ks + _iota((1, KC), 1)
        isc = jnp.where(kpos <= tq, isc, NEG)
        bits = lax.bitcast_convert_type(isc, i32)
        key_scr[c] = jnp.where(bits < 0, bits ^ jnp.int32(0x7FFFFFFF), bits)
        return 0

    lax.fori_loop(0, nch, idx_body, 0)

    def count(pred):
        def body(c, acc):
            x = jnp.where(pred(key_scr[c], c * KC + _iota((1, KC), 1)), 1.0, 0.0)
            for j in range(KC // LANES):
                acc = acc + x[:, j * LANES:(j + 1) * LANES]
            return acc
        acc = lax.fori_loop(0, nch, body, jnp.zeros((QB, LANES), f32))
        return jnp.sum(acc, axis=1, keepdims=True)

    def bit_body(b, t):
        cand = t + lax.shift_left(jnp.int32(1), 31 - b)
        return jnp.where(count(lambda k, _: k >= cand) >= TOPK, cand, t)

    thr = lax.fori_loop(0, 32, bit_body, jnp.full((QB, 1), INT_MIN, i32))
    room = TOPK - count(lambda k, _: k > thr)
    n_eq = count(lambda k, _: k == thr)
    tie_scr[...] = jnp.full(tie_scr.shape, 2**30, i32)

    @pl.when(jnp.max(jnp.where(n_eq > room, 1, 0)) > 0)
    def _():
        def pos_body(b, lo):
            cand = lo + lax.shift_left(jnp.int32(1), 15 - b)
            return jnp.where(count(lambda k, p: (k == thr) & (p < cand)) < room, cand, lo)
        lo = lax.fori_loop(0, 16, pos_body, jnp.zeros((QB, 1), i32))
        tie_scr[...] = jnp.broadcast_to(lo, tie_scr.shape)

    last = tie_scr[:, 0:1]

    qa = qa_ref[0].reshape(R, C_KPAD)
    acc_scr[...] = jnp.zeros(acc_scr.shape, f32)

    def att_body(c, carry):
        m_prev, l_prev = carry
        ks = pl.multiple_of(c * KC, KC)
        kvc = kv_ref[0, pl.ds(ks, KC), :]
        key = key_scr[c]
        kpos = ks + _iota((1, KC), 1)
        sel = ((key > thr) | ((key == thr) & (kpos <= last))) & (kpos <= tq)
        sel3 = sel[None]
        s3 = jnp.where(sel3, _dot_nt(qa, kvc).reshape(H, QB, KC), NEG)
        m_new = jnp.maximum(m_prev, jnp.max(s3, axis=-1, keepdims=True))
        p = jnp.where(sel3, jnp.exp(s3 - m_new), 0.0)
        alpha = jnp.exp(m_prev - m_new)
        l_new = alpha * l_prev + jnp.sum(p, axis=-1, keepdims=True)
        pv = _dot(p.reshape(R, KC).astype(MXU_DT), kvc[:, 0:C_KVL])
        acc_scr[...] = alpha.reshape(R, 1) * acc_scr[...] + pv
        return m_new, l_new

    m0 = jnp.full((H, QB, 1), NEG, f32)
    l0 = jnp.zeros((H, QB, 1), f32)
    _, l_fin = lax.fori_loop(0, nch, att_body, (m0, l0))
    o_lat = (acc_scr[...] / l_fin.reshape(R, 1)).astype(MXU_DT)
    for hd in range(H):
        out_ref[0, :, hd * C_DV:(hd + 1) * C_DV] = _dot(o_lat[hd * QB:(hd + 1) * QB], wuv_ref[hd]).astype(out_ref.dtype)


def _dsa(qa, kv, qi, ki, wi, w_uv):
    B, H, T, _ = qa.shape
    QB = min(Q_BLOCK, T)
    KC = min(KEY_CHUNK, T)
    topk = min(C_TOPK, T // 4)
    wuv = w_uv.transpose(1, 0, 2).astype(MXU_DT)
    kern = functools.partial(_dsa_kernel, QB=QB, KC=KC, TOPK=topk)
    return pl.pallas_call(
        kern,
        grid=(B, T // QB),
        in_specs=[pl.BlockSpec((1, H, QB, C_KPAD), lambda b, i: (b, 0, i, 0)),
                  pl.BlockSpec((1, H, QB, LANES), lambda b, i: (b, 0, i, 0)),
                  pl.BlockSpec((1, QB, LANES), lambda b, i: (b, i, 0)),
                  pl.BlockSpec((1, T, C_KPAD), lambda b, i: (b, 0, 0)),
                  pl.BlockSpec((1, T, LANES), lambda b, i: (b, 0, 0)),
                  _const_spec(wuv.shape)],
        out_specs=pl.BlockSpec((1, QB, H * C_DV), lambda b, i: (b, i, 0)),
        out_shape=jax.ShapeDtypeStruct((B, T, H * C_DV), MXU_DT),
        scratch_shapes=[pltpu.VMEM((T // KC, QB, KC), jnp.int32),
                        pltpu.VMEM((QB, LANES), jnp.int32),
                        pltpu.VMEM((H * QB, C_KVL), jnp.float32)],
        compiler_params=_cparams("parallel", "arbitrary"),
    )(qa, qi, wi, kv, ki, wuv)


def _post_kernel(*refs, n_mix, n_ff):
    h_ref = refs[0]
    mix = refs[1:1 + 2 * n_mix]
    g1, b1, w1_ref, w2_ref, g2, b2, wg_ref, p_ref, wp_ref, out_ref = refs[1 + 2 * n_mix:]
    h = h_ref[...]
    y = _dot(mix[0][...].astype(MXU_DT), mix[1][...])
    for k in range(1, n_mix):
        y = y + _dot(mix[2 * k][...].astype(MXU_DT), mix[2 * k + 1][...])
    h1 = _layer_norm(DN_ALPHA * h + y, g1[...], b1[...])
    h1b = h1.astype(MXU_DT)
    ff = D_FF // n_ff
    u = None
    for k in range(n_ff):
        a = jnp.square(jnp.maximum(_dot(h1b, w1_ref[:, k * ff:(k + 1) * ff]), 0.0))
        t = _dot(a.astype(MXU_DT), w2_ref[k * ff:(k + 1) * ff, :])
        u = t if u is None else u + t
    h2 = _layer_norm(DN_ALPHA * h1 + u, g2[...], b2[...])
    gate = jax.nn.sigmoid(_dot(h2.astype(MXU_DT), wg_ref[...]))
    out_ref[...] = h2 + gate * _dot(p_ref[...].astype(MXU_DT), wp_ref[...])


def _post(h2d, mixes, ln1_g, ln1_b, w1, w2, ln2_g, ln2_b, wg, p2d, wp, tm):
    M, D = h2d.shape
    row = lambda w: pl.BlockSpec((tm, w), lambda i: (i, 0))
    vec = lambda v: v[None, :]
    in_specs = [row(D)]
    args = [h2d]
    for x, w in mixes:
        in_specs += [row(x.shape[1]), _const_spec(w.shape)]
        args += [x, w]
    in_specs += [_const_spec((1, D)), _const_spec((1, D)), _const_spec(w1.shape), _const_spec(w2.shape),
                 _const_spec((1, D)), _const_spec((1, D)), _const_spec(wg.shape), row(D_PLE), _const_spec(wp.shape)]
    args += [vec(ln1_g), vec(ln1_b), w1, w2, vec(ln2_g), vec(ln2_b), wg, p2d, wp]
    kern = functools.partial(_post_kernel, n_mix=len(mixes), n_ff=4)
    return pl.pallas_call(
        kern,
        grid=(M // tm,),
        in_specs=in_specs,
        out_specs=row(D),
        out_shape=jax.ShapeDtypeStruct((M, D), jnp.float32),
        compiler_params=_cparams("parallel"),
    )(*args)


def kernel(x, p, positions, e_w_in, e_a_conv, e_a_i_b, e_a_f_b, e_a_norm, e_b_cmp_pos, e_b_cmp_w1, e_b_cmp_w2, e_b_g_b, e_w_out, o_w_in, o_q_norm, o_kv_norm, o_w_qb, o_w_uk, o_w_uv, o_w_iq, o_ik_g, o_ik_b, o_w_out, ln1_g, ln1_b, ln2_g, ln2_b, mlp_w1, mlp_w2, ple_gate_w, ple_w):
    B, T, D = x.shape
    M = B * T
    tm = min(ROW_TILE, T)
    h = x
    pos3 = positions[..., None]
    bf = lambda w: w.astype(MXU_DT)
    for i in range(DEPTH):
        j = i // 2
        if i % 2 == 0:
            qk, av, ao, small, bq, bc, bs = _even_proj(h, _even_w_in_aug(e_w_in[j]), tm)
            ya = _mlstm(qk, av, ao, small, e_a_conv[j], e_a_i_b[j], e_a_f_b[j], e_a_norm[j])
            cmp = _nsa_compress(bc, e_b_cmp_pos[j], e_b_cmp_w1[j], e_b_cmp_w2[j])
            yb = _nsa(bq, bs, cmp, small, e_b_g_b[j])
            w_out = e_w_out[j]
            mixes = [(ya.reshape(M, A_W), bf(w_out[:A_W])),
                     (yb.reshape(M, B_HEADS * LANES), bf(_pad_heads(w_out[A_W:], B_HEADS, B_DH, axis=0)))]
        else:
            qa, kv, qi, ki, wi = _odd_prep(h, pos3, o_w_in[j], o_q_norm[j], o_kv_norm[j], o_w_qb[j], o_w_uk[j],
                                           o_w_iq[j], o_ik_g[j], o_ik_b[j], tm)
            o = _dsa(qa, kv, qi, ki, wi, o_w_uv[j])
            mixes = [(o.reshape(M, C_HEADS * C_DV), bf(o_w_out[j]))]
        h = _post(h.reshape(M, D), mixes, ln1_g[i], ln1_b[i], bf(mlp_w1[i]), bf(mlp_w2[i]), ln2_g[i], ln2_b[i],
                  bf(ple_gate_w[i]), p[i].reshape(M, D_PLE), bf(ple_w[i]), tm).reshape(B, T, D)
    return h
```

```python
import functools

import numpy as np
import jax
import jax.numpy as jnp
from jax import lax
from jax.experimental import pallas as pl
from jax.experimental.pallas import tpu as pltpu

D_MODEL = 1024
DEPTH = 4
D_PLE = 256
D_FF = 4 * D_MODEL
DN_ALPHA = (2.0 * DEPTH) ** 0.25
LN_EPS = 1e-5
NEG = -1e30

A_HEADS = 4
A_DH = D_MODEL // 8
A_W = A_HEADS * A_DH
A_CONV = 4

B_HEADS = 8
B_DH = 64
B_KV = 2
B_HPG = B_HEADS // B_KV
B_CMP_LEN = 32
B_CMP_STRIDE = 16
B_CMP_HID = 128
B_SEL_BLK = 64
B_SEL_N = 16
B_WIN = 512

C_HEADS = 8
C_DN = 128
C_DR = 64
C_DV = 128
C_QL = 512
C_KVL = 256
C_IDX_HEADS = 8
C_IDX_DH = 64
C_IDX_DR = 32
C_TOPK = 256
ROPE_BASE = 10000.0

LANES = 128
SUBLANES = 8
VMEM_LIMIT_BYTES = 56 * 2**20
MXU_DT = jnp.bfloat16
INT_MIN = -2**31

MLSTM_CHUNK = 64
ROW_TILE = 256
Q_BLOCK = 128
DSA_Q_BLOCK = 256
DSA_HEAD_GROUP = 4
KEY_CHUNK = 512
KEY_SUB = 512
C_KPAD = 384

LOG2E = 1.4426950408889634
SOFTMAX_M0 = 0.5 * NEG
SOFTMAX_TINY = 1e-30


def _cparams(*sem):
    return pltpu.CompilerParams(dimension_semantics=sem, vmem_limit_bytes=VMEM_LIMIT_BYTES)


def _const_spec(shape):
    nd = len(shape)
    return pl.BlockSpec(shape, lambda *_: (0,) * nd, pipeline_mode=pl.Buffered(1))


def _dot(a, b):
    return jnp.dot(a, b, preferred_element_type=jnp.float32)


def _dot_nt(a, b):
    return lax.dot_general(a, b, (((1,), (1,)), ((), ())), preferred_element_type=jnp.float32)


def _dot_tn(a, b):
    return lax.dot_general(a, b, (((0,), (0,)), ((), ())), preferred_element_type=jnp.float32)


def _dot_f32(a, b):
    return jnp.dot(a, b, preferred_element_type=jnp.float32, precision=lax.Precision.HIGHEST)


def _layer_norm(x, g, b):
    mu = jnp.mean(x, axis=-1, keepdims=True)
    var = jnp.mean(jnp.square(x - mu), axis=-1, keepdims=True)
    return (x - mu) * lax.rsqrt(var + LN_EPS) * g + b


def _masked_softmax2(s, bias):
    s = s + bias
    m = jnp.maximum(jnp.max(s, axis=-1, keepdims=True), SOFTMAX_M0)
    e = jnp.exp2(s - m)
    return e / jnp.maximum(jnp.sum(e, axis=-1, keepdims=True), SOFTMAX_TINY)


def _iota(shape, dim):
    return lax.broadcasted_iota(jnp.int32, shape, dim)


def _pad_heads(w, n_heads, dh, axis=-1):
    axis = axis % w.ndim
    shp = w.shape[:axis] + (n_heads, dh) + w.shape[axis + 1:]
    w = w.reshape(shp)
    pad = [(0, 0)] * w.ndim
    pad[axis + 1] = (0, LANES - dh)
    w = jnp.pad(w, pad)
    return w.reshape(shp[:axis] + (n_heads * LANES,) + shp[axis + 2:])


def _pad_cols(w, width):
    return jnp.pad(w, [(0, 0)] * (w.ndim - 1) + [(0, width - w.shape[-1])])


def _even_proj_kernel(h_ref, w_ref, qk_ref, av_ref, ao_ref, sm_ref, bq_ref, bc_ref, bs_ref):
    z = _dot(h_ref[0].astype(MXU_DT), w_ref[...])
    o = 0
    qk_ref[0] = z[:, o:o + 2 * A_W]; o += 2 * A_W
    av_ref[0] = z[:, o:o + A_W].astype(av_ref.dtype); o += A_W
    ao_ref[0] = z[:, o:o + A_W]; o += A_W
    sm_ref[0] = z[:, o:o + LANES]; o += LANES
    for hd in range(B_HEADS):
        bq_ref[0, hd] = z[:, o:o + LANES].astype(bq_ref.dtype); o += LANES
    for j in range(2 * B_KV):
        bc_ref[0, j] = z[:, o:o + LANES]; o += LANES
    for j in range(4 * B_KV):
        bs_ref[0, j] = z[:, o:o + LANES].astype(bs_ref.dtype); o += LANES


def _even_proj(h3, w_aug, tm):
    B, T, D = h3.shape
    n = w_aug.shape[1]
    f32 = jnp.float32
    out_shape = (
        jax.ShapeDtypeStruct((B, T, 2 * A_W), f32),
        jax.ShapeDtypeStruct((B, T, A_W), MXU_DT),
        jax.ShapeDtypeStruct((B, T, A_W), f32),
        jax.ShapeDtypeStruct((B, T, LANES), f32),
        jax.ShapeDtypeStruct((B, B_HEADS, T, LANES), MXU_DT),
        jax.ShapeDtypeStruct((B, 2 * B_KV, T, LANES), f32),
        jax.ShapeDtypeStruct((B, 4 * B_KV, T, LANES), MXU_DT),
    )
    row = lambda w: pl.BlockSpec((1, tm, w), lambda b, i: (b, i, 0))
    hm = lambda nh: pl.BlockSpec((1, nh, tm, LANES), lambda b, i: (b, 0, i, 0))
    return pl.pallas_call(
        _even_proj_kernel,
        grid=(B, T // tm),
        in_specs=[row(D), _const_spec((D, n))],
        out_specs=(row(2 * A_W), row(A_W), row(A_W), row(LANES), hm(B_HEADS), hm(2 * B_KV), hm(4 * B_KV)),
        out_shape=out_shape,
        compiler_params=_cparams("parallel", "parallel"),
    )(h3, w_aug)


def _even_w_in_aug(w_in):
    sizes = (A_W, A_W, A_W, A_W, A_HEADS, A_HEADS, B_HEADS * B_DH) + (B_KV * B_DH,) * 6 + (3 * B_HEADS,)
    parts, o = [], 0
    for s in sizes:
        parts.append(w_in[:, o:o + s]); o += s
    aq, ak, av, ao, ai, af, bq, bkc, bvc, bks, bvs, bkw, bvw, bg = parts
    small = _pad_cols(jnp.concatenate([ai, af, bg], -1), LANES)
    ph = lambda w: _pad_heads(w, B_KV, B_DH)
    bq = bq * (B_DH ** -0.5 * LOG2E)
    cols = [aq, ak, av, ao, small, _pad_heads(bq, B_HEADS, B_DH),
            ph(bkc), ph(bvc), ph(bks), ph(bvs), ph(bkw), ph(bvw)]
    return jnp.concatenate(cols, -1).astype(MXU_DT)


def _mlstm_kernel(qk_ref, v_ref, o_ref, gc_ref, gr_ref, cw_ref, bc_ref, br_ref, ng_ref, tri_ref, out_ref,
                  xs_scr, ct_scr, n_scr, m_scr, *, L):
    c = pl.program_id(1)
    f32 = jnp.float32
    W2 = 2 * A_W

    @pl.when(c == 0)
    def _():
        xs_scr[0:SUBLANES, :] = jnp.zeros((SUBLANES, W2), f32)
        ct_scr[...] = jnp.zeros(ct_scr.shape, f32)
        n_scr[...] = jnp.zeros(n_scr.shape, f32)
        m_scr[...] = jnp.zeros(m_scr.shape, f32)

    xs_scr[SUBLANES:SUBLANES + L, :] = qk_ref[0]
    cw = cw_ref[...]
    base = SUBLANES - (A_CONV - 1)
    acc = xs_scr[base:base + L, :] * cw[0:1, :]
    for j in range(1, A_CONV):
        acc = acc + xs_scr[base + j:base + j + L, :] * cw[j:j + 1, :]
    qk = acc * jax.nn.sigmoid(acc)
    xs_scr[0:SUBLANES, :] = xs_scr[L:L + SUBLANES, :]

    gc = gc_ref[0] + bc_ref[...]
    gr = gr_ref[0, 0] + br_ref[...]
    tri = tri_ref[...]
    ig_c = gc[:, 0:A_HEADS]
    b_c = _dot_f32(tri, jax.nn.log_sigmoid(gc[:, A_HEADS:2 * A_HEADS]))
    ig_r = gr[0:A_HEADS, :]
    b_r = _dot_f32(jax.nn.log_sigmoid(gr[A_HEADS:2 * A_HEADS, :]), tri.T)
    causal = _iota((L, L), 1) <= _iota((L, L), 0)

    for hd in range(A_HEADS):
        sl = slice(hd * A_DH, (hd + 1) * A_DH)
        q_h = (qk[:, sl] * A_DH ** -0.5).astype(MXU_DT)
        k_f = qk[:, A_W + hd * A_DH:A_W + (hd + 1) * A_DH]
        k_h = k_f.astype(MXU_DT)
        v_h = v_ref[0][:, sl]
        bi = b_c[:, hd:hd + 1]
        ic = ig_c[:, hd:hd + 1]
        dmat = jnp.where(causal, bi - b_r[hd:hd + 1, :] + ig_r[hd:hd + 1, :], NEG)
        m_prev = m_scr[hd][:, 0:1]
        m_inter = bi + m_prev
        m_t = jnp.maximum(m_inter, jnp.max(dmat, axis=1, keepdims=True))
        e_inter = jnp.exp(m_inter - m_t)
        s = _dot_nt(q_h, k_h) * jnp.exp(dmat - m_t)
        ct = ct_scr[hd]
        nrow = n_scr[hd]
        num = e_inter * _dot(q_h, ct.astype(MXU_DT)) + _dot(s.astype(MXU_DT), v_h)
        den = e_inter * jnp.sum(q_h.astype(f32) * nrow, axis=1, keepdims=True) + jnp.sum(s, axis=1, keepdims=True)
        hc = num / jnp.maximum(jnp.abs(den), jnp.exp(-m_t))
        b_last = bi[L - 1:L, :]
        dec = b_last - bi + ic
        m_new = jnp.maximum(b_last + m_prev, jnp.max(dec, axis=0, keepdims=True))
        wgt = jnp.exp(dec - m_new)
        e_st = jnp.exp(b_last + m_prev - m_new)
        kw = k_f * wgt
        ct_scr[hd] = e_st * ct + _dot_tn(kw.astype(MXU_DT), v_h)
        n_scr[hd] = e_st * nrow + jnp.sum(kw, axis=0, keepdims=True)
        m_scr[hd] = jnp.broadcast_to(m_new, (1, LANES))
        mu = jnp.mean(hc, axis=1, keepdims=True)
        var = jnp.mean(jnp.square(hc - mu), axis=1, keepdims=True)
        hn = (hc - mu) * lax.rsqrt(var + LN_EPS) * ng_ref[:, sl]
        out_ref[0, :, sl] = (hn * jax.nn.sigmoid(o_ref[0][:, sl])).astype(out_ref.dtype)


def _mlstm(qk, av, ao, small, conv_w, i_b, f_b, norm_g):
    B, T, _ = qk.shape
    L = min(MLSTM_CHUNK, T)
    N = T // L
    f32 = jnp.float32
    gates = small[..., 0:2 * A_HEADS]
    gates_r = gates.reshape(B, N, L, 2 * A_HEADS).transpose(0, 1, 3, 2)
    bias = jnp.concatenate([i_b, f_b]).astype(f32)
    tri = jnp.tril(jnp.ones((L, L), f32))
    row = lambda w: pl.BlockSpec((1, L, w), lambda b, c: (b, c, 0))
    kern = functools.partial(_mlstm_kernel, L=L)
    return pl.pallas_call(
        kern,
        grid=(B, N),
        in_specs=[row(2 * A_W), row(A_W), row(A_W), row(2 * A_HEADS),
                  pl.BlockSpec((1, 1, 2 * A_HEADS, L), lambda b, c: (b, c, 0, 0)),
                  _const_spec((A_CONV, 2 * A_W)), _const_spec((1, 2 * A_HEADS)), _const_spec((2 * A_HEADS, 1)),
                  _const_spec((1, A_W)), _const_spec((L, L))],
        out_specs=row(A_W),
        out_shape=jax.ShapeDtypeStruct((B, T, A_W), MXU_DT),
        scratch_shapes=[pltpu.VMEM((L + 2 * SUBLANES, 2 * A_W), f32),
                        pltpu.VMEM((A_HEADS, A_DH, A_DH), f32),
                        pltpu.VMEM((A_HEADS, 1, A_DH), f32),
                        pltpu.VMEM((A_HEADS, 1, LANES), f32)],
        compiler_params=_cparams("parallel", "arbitrary"),
    )(qk, av, ao, gates, gates_r, conv_w, bias[None, :], bias[:, None], norm_g[None, :], tri)


def _nsa_cmp_kernel(x_ref, w1a_ref, w1b_ref, w2_ref, pos_ref, out_ref):
    n = x_ref.shape[2]
    half = x_ref.shape[3]
    for j in range(2):
        bias = (_dot(pos_ref[j, :, 0:half].astype(MXU_DT), w1a_ref[j])
                + _dot(pos_ref[j, :, half:2 * half].astype(MXU_DT), w1b_ref[j]))[0:1, :]
        for g in range(B_KV):
            u = x_ref[0, j * B_KV + g].astype(MXU_DT)
            a = _dot(u, w1a_ref[j])
            bm = _dot(u, w1b_ref[j])
            pre = a + pltpu.roll(bm, n - 1, 0) + bias
            hid = jax.nn.gelu(pre)
            out_ref[0, j * B_KV + g] = _dot(hid.astype(MXU_DT), w2_ref[j])


def _nsa_compress(bc, cmp_pos, cmp_w1, cmp_w2):
    B, _, T, _ = bc.shape
    nblk = T // B_CMP_STRIDE
    half = B_CMP_STRIDE * LANES
    x = bc.reshape(B, 2 * B_KV, nblk, half)
    w1 = jnp.pad(cmp_w1.reshape(2, B_CMP_LEN, B_DH, B_CMP_HID), ((0, 0), (0, 0), (0, LANES - B_DH), (0, 0)))
    w1 = w1.reshape(2, B_CMP_LEN * LANES, B_CMP_HID).astype(MXU_DT)
    w1a, w1b = w1[:, :half], w1[:, half:]
    w2 = _pad_cols(cmp_w2, LANES).astype(MXU_DT)
    pos = jnp.pad(cmp_pos, ((0, 0), (0, 0), (0, LANES - B_DH))).reshape(2, 1, B_CMP_LEN * LANES)
    pos = jnp.broadcast_to(pos, (2, SUBLANES, B_CMP_LEN * LANES))
    return pl.pallas_call(
        _nsa_cmp_kernel,
        grid=(B,),
        in_specs=[pl.BlockSpec((1, 2 * B_KV, nblk, half), lambda b: (b, 0, 0, 0)),
                  _const_spec(w1a.shape), _const_spec(w1b.shape), _const_spec(w2.shape), _const_spec(pos.shape)],
        out_specs=pl.BlockSpec((1, 2 * B_KV, nblk, LANES), lambda b: (b, 0, 0, 0)),
        out_shape=jax.ShapeDtypeStruct((B, 2 * B_KV, nblk, LANES), jnp.float32),
        compiler_params=_cparams("parallel"),
    )(x, w1a, w1b, w2, pos)


def _nsa_kernel(q_ref, kv_ref, cmp_ref, sm_ref, gb_ref, ovt_ref, exp_ref, out_ref, acc_scr,
                *, QB, KC, WL, NSB, NSEL):
    f32 = jnp.float32
    i = pl.program_id(1)
    s0 = i * QB
    R = B_HPG * QB
    tq_col = s0 + _iota((QB, 1), 0)
    tq_row = s0 + _iota((1, QB), 1)
    nch = (s0 + QB + KC - 1) // KC
    gates = jax.nn.sigmoid(sm_ref[0] + gb_ref[...])
    ncmp = cmp_ref.shape[2]
    cmp_end = _iota((1, ncmp), 1) * B_CMP_STRIDE + (B_CMP_LEN - 1)
    cmp_bias = jnp.where(cmp_end <= tq_col, 0.0, NEG)[None]
    jb = _iota((LANES, 1), 0)
    cur = lax.shift_right_logical(tq_row, int(np.log2(B_SEL_BLK)))
    forced = (jb == 0) | (jb == cur) | (jb == cur - 1)
    valid = jb * B_SEL_BLK <= tq_row
    wstart = pl.multiple_of(jnp.maximum(s0 + QB - WL, 0), QB)
    wpos = wstart + _iota((1, WL), 1)
    win_bias = jnp.where((wpos <= tq_col) & (wpos > tq_col - B_WIN), 0.0, NEG)[None]

    for g in range(B_KV):
        qs = q_ref[0, g * B_HPG:(g + 1) * B_HPG].reshape(R, LANES)
        kcm = cmp_ref[0, g].astype(MXU_DT)
        vcm = cmp_ref[0, B_KV + g].astype(MXU_DT)
        p_cmp = _masked_softmax2(_dot_nt(qs, kcm).reshape(B_HPG, QB, ncmp), cmp_bias)
        o_cmp = _dot(p_cmp.reshape(R, ncmp).astype(MXU_DT), vcm)
        psum = jnp.sum(p_cmp, axis=0)
        imp_t = lax.dot_general(ovt_ref[...], psum, (((1,), (1,)), ((), ())),
                                preferred_element_type=f32, precision=lax.Precision.HIGHEST)
        sc = jnp.where(forced, 1e6, imp_t)
        sc = jnp.where(valid, sc, NEG)
        sc = jnp.where(jb < NSB, sc, -jnp.inf)
        rank = jnp.zeros((LANES, QB), f32)
        for j in range(NSB):
            cj = jnp.broadcast_to(sc[j:j + 1, :], (LANES, QB))
            ahead = (cj > sc) | ((cj == sc) & (jb > j))
            rank = rank + jnp.where(ahead, 1.0, 0.0)
        sel_b = jnp.where(rank < NSEL, 1.0, 0.0).T.astype(MXU_DT)
        acc_scr[...] = jnp.zeros(acc_scr.shape, f32)

        def sel_body(c, carry):
            m_prev, l_prev = carry
            ks = pl.multiple_of(c * KC, KC)
            kc_ = kv_ref[0, 0 * B_KV + g, pl.ds(ks, KC), :]
            vc_ = kv_ref[0, 1 * B_KV + g, pl.ds(ks, KC), :]
            kpos = ks + _iota((1, KC), 1)
            mk = (_dot(sel_b, exp_ref[c]) > 0.5) & (kpos <= tq_col)
            s3 = _dot_nt(qs, kc_).reshape(B_HPG, QB, KC) + jnp.where(mk, 0.0, NEG)[None]
            m_new = jnp.maximum(m_prev, jnp.max(s3, axis=-1, keepdims=True))
            p = jnp.exp2(s3 - m_new)
            alpha = jnp.exp2(m_prev - m_new)
            l_new = alpha * l_prev + jnp.sum(p, axis=-1, keepdims=True)
            pv = _dot(p.reshape(R, KC).astype(MXU_DT), vc_)
            acc_scr[...] = alpha.reshape(R, 1) * acc_scr[...] + pv
            return m_new, l_new

        m0 = jnp.full((B_HPG, QB, 1), SOFTMAX_M0, f32)
        l0 = jnp.zeros((B_HPG, QB, 1), f32)
        _, l_fin = lax.fori_loop(0, nch, sel_body, (m0, l0))
        o_sel = acc_scr[...] / l_fin.reshape(R, 1)
        kw_ = kv_ref[0, 2 * B_KV + g, pl.ds(wstart, WL), :]
        vw_ = kv_ref[0, 3 * B_KV + g, pl.ds(wstart, WL), :]
        p_win = _masked_softmax2(_dot_nt(qs, kw_).reshape(B_HPG, QB, WL), win_bias)
        o_win = _dot(p_win.reshape(R, WL).astype(MXU_DT), vw_)
        for hd in range(B_HPG):
            c0 = 2 * A_HEADS + (g * B_HPG + hd) * 3
            rs = slice(hd * QB, (hd + 1) * QB)
            o = (gates[:, c0:c0 + 1] * o_cmp[rs] + gates[:, c0 + 1:c0 + 2] * o_sel[rs]
                 + gates[:, c0 + 2:c0 + 3] * o_win[rs])
            col = (g * B_HPG + hd) * LANES
            out_ref[0, :, col:col + LANES] = o.astype(out_ref.dtype)


def _nsa(bq, bs, cmp, small, g_b):
    B, _, T, _ = bq.shape
    QB = min(Q_BLOCK, T)
    KC = min(KEY_CHUNK, T)
    WL = min(B_WIN + QB, T)
    NSB = T // B_SEL_BLK
    NSEL = min(B_SEL_N, NSB)
    ncmp = cmp.shape[2]
    f32 = jnp.float32
    M = (T - B_CMP_LEN) // B_CMP_STRIDE + 1
    mi, jj = np.arange(ncmp)[None, :], np.arange(LANES)[:, None]
    ovt = ((mi * B_CMP_STRIDE < (jj + 1) * B_SEL_BLK) & (mi * B_CMP_STRIDE + B_CMP_LEN > jj * B_SEL_BLK)
           & (mi < M) & (jj < NSB)).astype(np.float32)
    kk = np.arange(T)
    expand = (kk[None, :] // B_SEL_BLK == np.arange(LANES)[:, None]).astype(np.float32)
    expand = jnp.asarray(expand.reshape(LANES, T // KC, KC).transpose(1, 0, 2), MXU_DT)
    gb = jnp.zeros((1, LANES), f32).at[0, 2 * A_HEADS:2 * A_HEADS + 3 * B_HEADS].set(g_b)
    kern = functools.partial(_nsa_kernel, QB=QB, KC=KC, WL=WL, NSB=NSB, NSEL=NSEL)
    return pl.pallas_call(
        kern,
        grid=(B, T // QB),
        in_specs=[pl.BlockSpec((1, B_HEADS, QB, LANES), lambda b, i: (b, 0, i, 0)),
                  pl.BlockSpec((1, 4 * B_KV, T, LANES), lambda b, i: (b, 0, 0, 0)),
                  pl.BlockSpec((1, 2 * B_KV, ncmp, LANES), lambda b, i: (b, 0, 0, 0)),
                  pl.BlockSpec((1, QB, LANES), lambda b, i: (b, i, 0)),
                  _const_spec((1, LANES)), _const_spec(ovt.shape), _const_spec(expand.shape)],
        out_specs=pl.BlockSpec((1, QB, B_HEADS * LANES), lambda b, i: (b, i, 0)),
        out_shape=jax.ShapeDtypeStruct((B, T, B_HEADS * LANES), MXU_DT),
        scratch_shapes=[pltpu.VMEM((B_HPG * QB, LANES), f32)],
        compiler_params=_cparams("parallel", "arbitrary"),
    )(bq, bs, cmp, small, gb, jnp.asarray(ovt), expand)


def _rope_rows(d):
    inv = ROPE_BASE ** (-jnp.arange(0, d, 2, dtype=jnp.float32) / d)
    z = jnp.zeros((C_DR - d,), jnp.float32)
    inv64 = jnp.concatenate([inv, inv, z])
    sgn64 = jnp.concatenate([-jnp.ones(d // 2), jnp.ones(d // 2), z]).astype(jnp.float32)
    return inv64, sgn64


def _odd_prep_kernel(h_ref, pos_ref, win_ref, wqb_ref, wiq_ref, wuk_ref, qn_ref, kvn_ref, ikg_ref, ikb_ref,
                     rope_ref, perm_ref, qa_ref, kv_ref, qi_ref, ki_ref, wi_ref):
    f32 = jnp.float32
    z = _dot(h_ref[0].astype(MXU_DT), win_ref[...])
    pos = pos_ref[0].astype(f32)
    rr = rope_ref[...]
    ang_q = pos * rr[0:1, :]
    cos_q, sin_q = jnp.cos(ang_q), jnp.sin(ang_q) * rr[1:2, :]
    ang_i = pos * rr[2:3, :]
    cos_i, sin_i = jnp.cos(ang_i), jnp.sin(ang_i) * rr[3:4, :]

    def rms(x, g):
        return x * lax.rsqrt(jnp.mean(jnp.square(x), axis=-1, keepdims=True) + LN_EPS) * g

    cq = rms(z[:, 0:C_QL], qn_ref[...])
    ckv = rms(z[:, C_QL:C_QL + C_KVL], kvn_ref[...])
    o = C_QL + C_KVL
    k_rope = z[:, o:o + LANES] * cos_q + z[:, o + LANES:o + 2 * LANES] * sin_q
    kv_ref[0, :, 0:C_KVL] = ckv.astype(kv_ref.dtype)
    kv_ref[0, :, C_KVL:C_KPAD] = k_rope.astype(kv_ref.dtype)
    ik = z[:, o + 2 * LANES:o + 3 * LANES]
    real = _iota((1, LANES), 1) < C_IDX_DH
    mu = jnp.sum(ik, axis=-1, keepdims=True) / C_IDX_DH
    dlt = jnp.where(real, ik - mu, 0.0)
    var = jnp.sum(jnp.square(dlt), axis=-1, keepdims=True) / C_IDX_DH
    ki = dlt * lax.rsqrt(var + LN_EPS) * ikg_ref[...] + ikb_ref[...]
    ki = ki * cos_i + _dot_f32(ki, perm_ref[...]) * sin_i
    ki_ref[0] = ki.astype(ki_ref.dtype)
    wi_ref[0] = z[:, o + 3 * LANES:o + 4 * LANES] * (C_IDX_HEADS ** -0.5 * C_IDX_DH ** -0.5)
    cqb = cq.astype(MXU_DT)
    qf = _dot(cqb, wqb_ref[...])
    qi = _dot(cqb, wiq_ref[...])
    scale = (C_DN + C_DR) ** -0.5 * LOG2E
    nh = C_HEADS * LANES
    for hd in range(C_HEADS):
        q_abs = _dot(qf[:, hd * C_DN:(hd + 1) * C_DN].astype(MXU_DT), wuk_ref[hd])
        cs = slice(hd * LANES, (hd + 1) * LANES)
        q_rope = qf[:, nh:2 * nh][:, cs] * cos_q + qf[:, 2 * nh:3 * nh][:, cs] * sin_q
        qa_ref[0, hd, :, 0:C_KVL] = (q_abs * scale).astype(qa_ref.dtype)
        qa_ref[0, hd, :, C_KVL:C_KPAD] = (q_rope * scale).astype(qa_ref.dtype)
        qi_h = qi[:, 0:nh][:, cs] * cos_i + qi[:, nh:2 * nh][:, cs] * sin_i
        qi_ref[0, hd] = qi_h.astype(qi_ref.dtype)


def _rot_cols(w, n_heads, dh, d):
    w = w.reshape(w.shape[0], n_heads, dh)
    h = d // 2
    return jnp.concatenate([w[..., h:d], w[..., 0:h], jnp.zeros_like(w[..., d:])], -1).reshape(w.shape[0], n_heads * dh)


def _odd_prep(h3, pos3, w_in, q_norm, kv_norm, w_qb, w_uk, w_iq, ik_g, ik_b, tm):
    B, T, D = h3.shape
    f32 = jnp.float32
    o = 0
    parts = []
    for s in (C_QL, C_KVL, C_DR, C_IDX_DH, C_IDX_HEADS):
        parts.append(w_in[:, o:o + s]); o += s
    w_cq, w_ckv, w_kr, w_ik, w_iw = parts
    pc = lambda w: _pad_cols(w, LANES)
    win = jnp.concatenate([w_cq, w_ckv, pc(w_kr), pc(_rot_cols(w_kr, 1, C_DR, C_DR)), pc(w_ik), pc(w_iw)],
                          -1).astype(MXU_DT)
    wq = w_qb.reshape(C_QL, C_HEADS, C_DN + C_DR)
    w_nope = wq[..., :C_DN].reshape(C_QL, C_HEADS * C_DN)
    w_rope = wq[..., C_DN:].reshape(C_QL, C_HEADS * C_DR)
    wqb = jnp.concatenate([w_nope, _pad_heads(w_rope, C_HEADS, C_DR),
                           _pad_heads(_rot_cols(w_rope, C_HEADS, C_DR, C_DR), C_HEADS, C_DR)], -1).astype(MXU_DT)
    wiq = jnp.concatenate([_pad_heads(w_iq, C_IDX_HEADS, C_IDX_DH),
                           _pad_heads(_rot_cols(w_iq, C_IDX_HEADS, C_IDX_DH, C_IDX_DR), C_IDX_HEADS, C_IDX_DH)],
                          -1).astype(MXU_DT)
    wuk = w_uk.transpose(1, 2, 0).astype(MXU_DT)
    inv_q, sgn_q = _rope_rows(C_DR)
    inv_i, sgn_i = _rope_rows(C_IDX_DR)
    rope = jnp.stack([jnp.tile(v, LANES // C_DR) for v in (inv_q, sgn_q, inv_i, sgn_i)])
    rope = jnp.concatenate([rope, jnp.zeros((SUBLANES - 4, LANES), f32)])
    hh = C_IDX_DR // 2
    src = np.arange(LANES)
    src[:hh] += hh
    src[hh:C_IDX_DR] -= hh
    perm = np.zeros((LANES, LANES), np.float32)
    perm[src, np.arange(LANES)] = 1.0
    ikg = _pad_cols(ik_g[None, :], LANES)
    ikb = _pad_cols(ik_b[None, :], LANES)
    out_shape = (
        jax.ShapeDtypeStruct((B, C_HEADS, T, C_KPAD), MXU_DT),
        jax.ShapeDtypeStruct((B, T, C_KPAD), MXU_DT),
        jax.ShapeDtypeStruct((B, C_IDX_HEADS, T, LANES), MXU_DT),
        jax.ShapeDtypeStruct((B, T, LANES), MXU_DT),
        jax.ShapeDtypeStruct((B, T, LANES), f32),
    )
    row = lambda w: pl.BlockSpec((1, tm, w), lambda b, i: (b, i, 0))
    hm = lambda w: pl.BlockSpec((1, C_HEADS, tm, w), lambda b, i: (b, 0, i, 0))
    return pl.pallas_call(
        _odd_prep_kernel,
        grid=(B, T // tm),
        in_specs=[row(D), row(1), _const_spec(win.shape), _const_spec(wqb.shape), _const_spec(wiq.shape),
                  _const_spec(wuk.shape), _const_spec((1, C_QL)), _const_spec((1, C_KVL)),
                  _const_spec((1, LANES)), _const_spec((1, LANES)), _const_spec(rope.shape),
                  _const_spec(perm.shape)],
        out_specs=(hm(C_KPAD), row(C_KPAD), hm(LANES), row(LANES), row(LANES)),
        out_shape=out_shape,
        compiler_params=_cparams("parallel", "parallel"),
    )(h3, pos3, win, wqb, wiq, wuk, q_norm[None, :], kv_norm[None, :], ikg, ikb, rope, jnp.asarray(perm))


def _dsa_kernel(qa_ref, qi_ref, wi_ref, kv_ref, ki_ref, wuv_ref, out_ref, key_scr, hi_scr, lo_scr, tie_scr, acc_scr,
                *, QB, KC, SUB, TOPK, HG):
    f32, i32, i16 = jnp.float32, jnp.int32, jnp.int16
    i = pl.program_id(1)
    s0 = i * QB
    H = C_HEADS
    NG = H // HG
    RG = HG * QB
    T = kv_ref.shape[1]
    nch = (s0 + QB + KC - 1) // KC
    nrep = KC // LANES
    tq = s0 + _iota((QB, 1), 0)
    wv = wi_ref[0]
    ones = jnp.ones((LANES, LANES), MXU_DT)
    one, zero = jnp.ones((), MXU_DT), jnp.zeros((), MXU_DT)

    def idx_body(c, _):
        ks = pl.multiple_of(c * KC, KC)
        kic = ki_ref[0, pl.ds(ks, KC), :]
        isc = None
        for g in range(NG):
            qg = qi_ref[0, g * HG:(g + 1) * HG].reshape(RG, LANES)
            wcol = jnp.concatenate([wv[:, hd:hd + 1] for hd in range(g * HG, (g + 1) * HG)], axis=0)
            s = jnp.maximum(_dot_nt(qg, kic), 0.0) * wcol
            part = jnp.sum(s.reshape(HG, QB, KC), axis=0)
            isc = part if isc is None else isc + part
        isc = jnp.where(isc == 0.0, 0.0, isc)
        kpos = ks + _iota((1, KC), 1)
        isc = jnp.where(kpos <= tq, isc, NEG)
        bits = lax.bitcast_convert_type(isc, i32)
        key = jnp.where(bits < 0, bits ^ jnp.int32(0x7FFFFFFF), bits)
        key_scr[c] = key
        hi_scr[c] = lax.shift_right_arithmetic(key, 16).astype(i16)
        lo_scr[c] = ((key & 0xFFFF) - 32768).astype(i16)
        return 0

    lax.fori_loop(0, nch, idx_body, 0)

    def rep16(v):
        return jnp.tile(v.astype(i16), (1, nrep))

    def select_threshold(nk):
        halves = [slice(k * (QB // 2), (k + 1) * (QB // 2)) for k in range(2)]

        def count16(pred, rows):
            acc = jnp.zeros((QB // 2, LANES), MXU_DT)
            for c in range(nk):
                x = jnp.where(pred(hi_scr[c, rows, :], lo_scr[c, rows, :]), one, zero)
                for j in range(nrep):
                    acc = acc + x[:, j * LANES:(j + 1) * LANES]
            return _dot(acc, ones)

        def bisect(pick, base):
            def body(b, ts):
                out = []
                for rows, t, n0 in zip(halves, ts, base):
                    cand = t + lax.shift_left(jnp.int32(1), 15 - b)
                    c16 = rep16(cand)
                    cnt = n0 + count16(lambda h, l: pick(h, l) >= c16, rows)
                    out.append(jnp.where(cnt >= TOPK, cand, t))
                return tuple(out)
            return lax.fori_loop(0, 16, body, tuple(jnp.full((QB // 2, LANES), -32768, i32) for _ in halves))

        thi = bisect(lambda h, l: h, (0.0, 0.0))
        thi16 = [rep16(t) for t in thi]
        for c in range(nk):
            for rows, t16 in zip(halves, thi16):
                lo_scr[c, rows, :] = jnp.where(hi_scr[c, rows, :] == t16, lo_scr[c, rows, :],
                                               jnp.full((), -32768, i16))
        n_hi = [count16(lambda h, l: h > t16, rows) for rows, t16 in zip(halves, thi16)]
        tlo = bisect(lambda h, l: l, n_hi)
        tlo16 = [rep16(t) for t in tlo]
        n_gt = [n0 + count16(lambda h, l: l > t16, rows) for rows, t16, n0 in zip(halves, tlo16, n_hi)]
        n_eq = [count16(lambda h, l: (h == a16) & (l == b16), rows) for rows, a16, b16 in zip(halves, thi16, tlo16)]
        cat = lambda xs: jnp.concatenate(xs, axis=0)
        return cat(thi), cat(tlo), cat(n_gt), cat(n_eq)

    thi, tlo, n_gt, n_eq = lax.switch(nch - 1, [functools.partial(select_threshold, k + 1) for k in range(T // KC)])
    thr = (thi * 65536 + (tlo + 32768))[:, 0:1]
    room = TOPK - n_gt
    tie_scr[...] = jnp.full(tie_scr.shape, 2**30, i32)

    @pl.when(jnp.max(jnp.where(n_eq > room, 1, 0)) > 0)
    def _():
        room1 = room[:, 0:1]

        def count_before(cand):
            def body(c, acc):
                x = jnp.where((key_scr[c] == thr) & (c * KC + _iota((1, KC), 1) < cand), 1.0, 0.0)
                for j in range(nrep):
                    acc = acc + x[:, j * LANES:(j + 1) * LANES]
                return acc
            acc = lax.fori_loop(0, nch, body, jnp.zeros((QB, LANES), f32))
            return jnp.sum(acc, axis=1, keepdims=True)

        def pos_body(b, lo):
            cand = lo + lax.shift_left(jnp.int32(1), 15 - b)
            return jnp.where(count_before(cand) < room1, cand, lo)

        lo = lax.fori_loop(0, 16, pos_body, jnp.zeros((QB, 1), i32))
        tie_scr[...] = jnp.broadcast_to(lo, tie_scr.shape)

    last = tie_scr[:, 0:1]

    acc_scr[...] = jnp.zeros(acc_scr.shape, f32)

    def att_body(c, carry):
        ms, ls = list(carry[0]), list(carry[1])
        ks = pl.multiple_of(c * KC, KC)
        key = key_scr[c]
        kpos = ks + _iota((1, KC), 1)
        sel = ((key > thr) | ((key == thr) & (kpos <= last))) & (kpos <= tq)
        bias = jnp.where(sel, 0.0, NEG)
        for u in range(KC // SUB):
            kvc = kv_ref[0, pl.ds(pl.multiple_of(ks + u * SUB, SUB), SUB), :]
            b_u = bias[:, u * SUB:(u + 1) * SUB][None]
            for g in range(NG):
                qg = qa_ref[0, g * HG:(g + 1) * HG].reshape(RG, C_KPAD)
                s = _dot_nt(qg, kvc).reshape(HG, QB, SUB) + b_u
                m_new = jnp.maximum(ms[g], jnp.max(s, axis=-1, keepdims=True))
                p = jnp.exp2(s - m_new)
                alpha = jnp.exp2(ms[g] - m_new)
                ls[g] = alpha * ls[g] + jnp.sum(p, axis=-1, keepdims=True)
                ms[g] = m_new
                rows = slice(g * RG, (g + 1) * RG)
                pv = _dot(p.reshape(RG, SUB).astype(MXU_DT), kvc[:, 0:C_KVL])
                acc_scr[rows] = alpha.reshape(RG, 1) * acc_scr[rows] + pv
        return tuple(ms), tuple(ls)

    m0 = tuple(jnp.full((HG, QB, 1), SOFTMAX_M0, f32) for _ in range(NG))
    l0 = tuple(jnp.zeros((HG, QB, 1), f32) for _ in range(NG))
    _, l_fin = lax.fori_loop(0, nch, att_body, (m0, l0))
    for g in range(NG):
        o_lat = (acc_scr[g * RG:(g + 1) * RG] / l_fin[g].reshape(RG, 1)).astype(MXU_DT)
        for k in range(HG):
            hd = g * HG + k
            out_ref[0, :, hd * C_DV:(hd + 1) * C_DV] = _dot(o_lat[k * QB:(k + 1) * QB], wuv_ref[hd]).astype(out_ref.dtype)


def _dsa(qa, kv, qi, ki, wi, w_uv):
    B, H, T, _ = qa.shape
    QB = min(DSA_Q_BLOCK, T)
    KC = min(KEY_CHUNK, T)
    topk = min(C_TOPK, T // 4)
    wuv = w_uv.transpose(1, 0, 2).astype(MXU_DT)
    assert T // LANES <= 256
    kern = functools.partial(_dsa_kernel, QB=QB, KC=KC, SUB=min(KEY_SUB, KC), TOPK=topk, HG=DSA_HEAD_GROUP)
    return pl.pallas_call(
        kern,
        grid=(B, T // QB),
        in_specs=[pl.BlockSpec((1, H, QB, C_KPAD), lambda b, i: (b, 0, i, 0)),
                  pl.BlockSpec((1, H, QB, LANES), lambda b, i: (b, 0, i, 0)),
                  pl.BlockSpec((1, QB, LANES), lambda b, i: (b, i, 0)),
                  pl.BlockSpec((1, T, C_KPAD), lambda b, i: (b, 0, 0)),
                  pl.BlockSpec((1, T, LANES), lambda b, i: (b, 0, 0)),
                  _const_spec(wuv.shape)],
        out_specs=pl.BlockSpec((1, QB, H * C_DV), lambda b, i: (b, i, 0)),
        out_shape=jax.ShapeDtypeStruct((B, T, H * C_DV), MXU_DT),
        scratch_shapes=[pltpu.VMEM((T // KC, QB, KC), jnp.int32),
                        pltpu.VMEM((T // KC, QB, KC), jnp.int16),
                        pltpu.VMEM((T // KC, QB, KC), jnp.int16),
                        pltpu.VMEM((QB, LANES), jnp.int32),
                        pltpu.VMEM((H * QB, C_KVL), jnp.float32)],
        compiler_params=_cparams("parallel", "arbitrary"),
    )(qa, qi, wi, kv, ki, wuv)


def _post_kernel(*refs, n_mix, n_ff):
    h_ref = refs[0]
    mix = refs[1:1 + 2 * n_mix]
    g1, b1, w1_ref, w2_ref, g2, b2, wg_ref, p_ref, wp_ref, out_ref = refs[1 + 2 * n_mix:]
    h = h_ref[...]
    y = _dot(mix[0][...].astype(MXU_DT), mix[1][...])
    for k in range(1, n_mix):
        y = y + _dot(mix[2 * k][...].astype(MXU_DT), mix[2 * k + 1][...])
    h1 = _layer_norm(DN_ALPHA * h + y, g1[...], b1[...])
    h1b = h1.astype(MXU_DT)
    ff = D_FF // n_ff
    u = None
    for k in range(n_ff):
        a = jnp.square(jnp.maximum(_dot(h1b, w1_ref[:, k * ff:(k + 1) * ff]), 0.0))
        t = _dot(a.astype(MXU_DT), w2_ref[k * ff:(k + 1) * ff, :])
        u = t if u is None else u + t
    h2 = _layer_norm(DN_ALPHA * h1 + u, g2[...], b2[...])
    gate = jax.nn.sigmoid(_dot(h2.astype(MXU_DT), wg_ref[...]))
    out_ref[...] = h2 + gate * _dot(p_ref[...].astype(MXU_DT), wp_ref[...])


def _post(h2d, mixes, ln1_g, ln1_b, w1, w2, ln2_g, ln2_b, wg, p2d, wp, tm):
    M, D = h2d.shape
    row = lambda w: pl.BlockSpec((tm, w), lambda i: (i, 0))
    vec = lambda v: v[None, :]
    in_specs = [row(D)]
    args = [h2d]
    for x, w in mixes:
        in_specs += [row(x.shape[1]), _const_spec(w.shape)]
        args += [x, w]
    in_specs += [_const_spec((1, D)), _const_spec((1, D)), _const_spec(w1.shape), _const_spec(w2.shape),
                 _const_spec((1, D)), _const_spec((1, D)), _const_spec(wg.shape), row(D_PLE), _const_spec(wp.shape)]
    args += [vec(ln1_g), vec(ln1_b), w1, w2, vec(ln2_g), vec(ln2_b), wg, p2d, wp]
    kern = functools.partial(_post_kernel, n_mix=len(mixes), n_ff=4)
    return pl.pallas_call(
        kern,
        grid=(M // tm,),
        in_specs=in_specs,
        out_specs=row(D),
        out_shape=jax.ShapeDtypeStruct((M, D), jnp.float32),
        compiler_params=_cparams("parallel"),
    )(*args)


def kernel(x, p, positions, e_w_in, e_a_conv, e_a_i_b, e_a_f_b, e_a_norm, e_b_cmp_pos, e_b_cmp_w1, e_b_cmp_w2, e_b_g_b, e_w_out, o_w_in, o_q_norm, o_kv_norm, o_w_qb, o_w_uk, o_w_uv, o_w_iq, o_ik_g, o_ik_b, o_w_out, ln1_g, ln1_b, ln2_g, ln2_b, mlp_w1, mlp_w2, ple_gate_w, ple_w):
    B, T, D = x.shape
    M = B * T
    tm = min(ROW_TILE, T)
    h = x
    pos3 = positions[..., None]
    bf = lambda w: w.astype(MXU_DT)
    for i in range(DEPTH):
        j = i // 2
        if i % 2 == 0:
            qk, av, ao, small, bq, bc, bs = _even_proj(h, _even_w_in_aug(e_w_in[j]), tm)
            ya = _mlstm(qk, av, ao, small, e_a_conv[j], e_a_i_b[j], e_a_f_b[j], e_a_norm[j])
            cmp = _nsa_compress(bc, e_b_cmp_pos[j], e_b_cmp_w1[j], e_b_cmp_w2[j])
            yb = _nsa(bq, bs, cmp, small, e_b_g_b[j])
            w_out = e_w_out[j]
            mixes = [(ya.reshape(M, A_W), bf(w_out[:A_W])),
                     (yb.reshape(M, B_HEADS * LANES), bf(_pad_heads(w_out[A_W:], B_HEADS, B_DH, axis=0)))]
        else:
            qa, kv, qi, ki, wi = _odd_prep(h, pos3, o_w_in[j], o_q_norm[j], o_kv_norm[j], o_w_qb[j], o_w_uk[j],
                                           o_w_iq[j], o_ik_g[j], o_ik_b[j], tm)
            o = _dsa(qa, kv, qi, ki, wi, o_w_uv[j])
            mixes = [(o.reshape(M, C_HEADS * C_DV), bf(o_w_out[j]))]
        h = _post(h.reshape(M, D), mixes, ln1_g[i], ln1_b[i], bf(mlp_w1[i]), bf(mlp_w2[i]), ln2_g[i], ln2_b[i],
                  bf(ple_gate_w[i]), p[i].reshape(M, D_PLE), bf(ple_w[i]), tm).reshape(B, T, D)
    return h
```

```python
import functools

import numpy as np
import jax
import jax.numpy as jnp
from jax import lax
from jax.experimental import pallas as pl
from jax.experimental.pallas import tpu as pltpu

D_MODEL = 1024
DEPTH = 4
D_PLE = 256
D_FF = 4 * D_MODEL
DN_ALPHA = (2.0 * DEPTH) ** 0.25
LN_EPS = 1e-5
NEG = -1e30

A_HEADS = 4
A_DH = D_MODEL // 8
A_W = A_HEADS * A_DH
A_CONV = 4

B_HEADS = 8
B_DH = 64
B_KV = 2
B_HPG = B_HEADS // B_KV
B_CMP_LEN = 32
B_CMP_STRIDE = 16
B_CMP_HID = 128
B_SEL_BLK = 64
B_SEL_N = 16
B_WIN = 512

C_HEADS = 8
C_DN = 128
C_DR = 64
C_DV = 128
C_QL = 512
C_KVL = 256
C_IDX_HEADS = 8
C_IDX_DH = 64
C_IDX_DR = 32
C_TOPK = 256
ROPE_BASE = 10000.0

LANES = 128
SUBLANES = 8
PACKED_ROWS = 16
VMEM_LIMIT_BYTES = 56 * 2**20
MXU_DT = jnp.bfloat16
INT_MIN = -2**31

MLSTM_CHUNK = 256
ROW_TILE = 256
Q_BLOCK = 256
DSA_Q_BLOCK = 256
DSA_HEAD_GROUP = 4
KEY_CHUNK = 512
KEY_SUB = 512
C_KPAD = 384

LOG2E = 1.4426950408889634
SOFTMAX_M0 = 0.5 * NEG
SOFTMAX_TINY = 1e-30


def _cparams(*sem):
    return pltpu.CompilerParams(dimension_semantics=sem, vmem_limit_bytes=VMEM_LIMIT_BYTES)


def _const_spec(shape):
    nd = len(shape)
    return pl.BlockSpec(shape, lambda *_: (0,) * nd, pipeline_mode=pl.Buffered(1))


def _dot(a, b):
    return jnp.dot(a, b, preferred_element_type=jnp.float32)


def _dot_nt(a, b):
    return lax.dot_general(a, b, (((1,), (1,)), ((), ())), preferred_element_type=jnp.float32)


def _dot_tn(a, b):
    return lax.dot_general(a, b, (((0,), (0,)), ((), ())), preferred_element_type=jnp.float32)


def _dot_f32(a, b):
    return jnp.dot(a, b, preferred_element_type=jnp.float32, precision=lax.Precision.HIGHEST)


def _layer_norm(x, g, b):
    mu = jnp.mean(x, axis=-1, keepdims=True)
    var = jnp.mean(jnp.square(x - mu), axis=-1, keepdims=True)
    return (x - mu) * lax.rsqrt(var + LN_EPS) * g + b


def _masked_softmax2(s, bias):
    s = s + bias
    m = jnp.maximum(jnp.max(s, axis=-1, keepdims=True), SOFTMAX_M0)
    e = jnp.exp2(s - m)
    return e / jnp.maximum(jnp.sum(e, axis=-1, keepdims=True), SOFTMAX_TINY)


def _iota(shape, dim):
    return lax.broadcasted_iota(jnp.int32, shape, dim)


def _pad_heads(w, n_heads, dh, axis=-1):
    axis = axis % w.ndim
    shp = w.shape[:axis] + (n_heads, dh) + w.shape[axis + 1:]
    w = w.reshape(shp)
    pad = [(0, 0)] * w.ndim
    pad[axis + 1] = (0, LANES - dh)
    w = jnp.pad(w, pad)
    return w.reshape(shp[:axis] + (n_heads * LANES,) + shp[axis + 2:])


def _pad_cols(w, width):
    return jnp.pad(w, [(0, 0)] * (w.ndim - 1) + [(0, width - w.shape[-1])])


def _even_proj_kernel(h_ref, w_ref, qk_ref, av_ref, ao_ref, sm_ref, bq_ref, bc_ref, bs_ref):
    z = _dot(h_ref[0].astype(MXU_DT), w_ref[...])
    o = 0
    qk_ref[0] = z[:, o:o + 2 * A_W]; o += 2 * A_W
    av_ref[0] = z[:, o:o + A_W].astype(av_ref.dtype); o += A_W
    ao_ref[0] = z[:, o:o + A_W]; o += A_W
    sm_ref[0] = z[:, o:o + LANES]; o += LANES
    for hd in range(B_HEADS):
        bq_ref[0, hd] = z[:, o:o + LANES].astype(bq_ref.dtype); o += LANES
    for j in range(2 * B_KV):
        bc_ref[0, j] = z[:, o:o + LANES]; o += LANES
    for j in range(4 * B_KV):
        bs_ref[0, j] = z[:, o:o + LANES].astype(bs_ref.dtype); o += LANES


def _even_proj(h3, w_aug, tm):
    B, T, D = h3.shape
    n = w_aug.shape[1]
    f32 = jnp.float32
    out_shape = (
        jax.ShapeDtypeStruct((B, T, 2 * A_W), f32),
        jax.ShapeDtypeStruct((B, T, A_W), MXU_DT),
        jax.ShapeDtypeStruct((B, T, A_W), f32),
        jax.ShapeDtypeStruct((B, T, LANES), f32),
        jax.ShapeDtypeStruct((B, B_HEADS, T, LANES), MXU_DT),
        jax.ShapeDtypeStruct((B, 2 * B_KV, T, LANES), f32),
        jax.ShapeDtypeStruct((B, 4 * B_KV, T, LANES), MXU_DT),
    )
    row = lambda w: pl.BlockSpec((1, tm, w), lambda b, i: (b, i, 0))
    hm = lambda nh: pl.BlockSpec((1, nh, tm, LANES), lambda b, i: (b, 0, i, 0))
    return pl.pallas_call(
        _even_proj_kernel,
        grid=(B, T // tm),
        in_specs=[row(D), _const_spec((D, n))],
        out_specs=(row(2 * A_W), row(A_W), row(A_W), row(LANES), hm(B_HEADS), hm(2 * B_KV), hm(4 * B_KV)),
        out_shape=out_shape,
        compiler_params=_cparams("parallel", "parallel"),
    )(h3, w_aug)


def _even_w_in_aug(w_in):
    sizes = (A_W, A_W, A_W, A_W, A_HEADS, A_HEADS, B_HEADS * B_DH) + (B_KV * B_DH,) * 6 + (3 * B_HEADS,)
    parts, o = [], 0
    for s in sizes:
        parts.append(w_in[:, o:o + s]); o += s
    aq, ak, av, ao, ai, af, bq, bkc, bvc, bks, bvs, bkw, bvw, bg = parts
    small = _pad_cols(jnp.concatenate([ai, af, bg], -1), LANES)
    ph = lambda w: _pad_heads(w, B_KV, B_DH)
    bq = bq * (B_DH ** -0.5 * LOG2E)
    cols = [aq, ak, av, ao, small, _pad_heads(bq, B_HEADS, B_DH),
            ph(bkc), ph(bvc), ph(bks), ph(bvs), ph(bkw), ph(bvw)]
    return jnp.concatenate(cols, -1).astype(MXU_DT)


def _mlstm_kernel(qk_ref, v_ref, o_ref, gc_ref, gr_ref, cw_ref, bc_ref, br_ref, ng_ref, tri_ref, out_ref,
                  xs_scr, ct_scr, n_scr, m_scr, *, L):
    c = pl.program_id(1)
    f32 = jnp.float32
    W2 = 2 * A_W

    @pl.when(c == 0)
    def _():
        xs_scr[0:SUBLANES, :] = jnp.zeros((SUBLANES, W2), f32)
        ct_scr[...] = jnp.zeros(ct_scr.shape, f32)
        n_scr[...] = jnp.zeros(n_scr.shape, f32)
        m_scr[...] = jnp.zeros(m_scr.shape, f32)

    xs_scr[SUBLANES:SUBLANES + L, :] = qk_ref[0]
    cw = cw_ref[...]
    base = SUBLANES - (A_CONV - 1)
    acc = xs_scr[base:base + L, :] * cw[0:1, :]
    for j in range(1, A_CONV):
        acc = acc + xs_scr[base + j:base + j + L, :] * cw[j:j + 1, :]
    qk = acc * jax.nn.sigmoid(acc)
    xs_scr[0:SUBLANES, :] = xs_scr[L:L + SUBLANES, :]

    gc = gc_ref[0] + bc_ref[...]
    gr = gr_ref[0, 0] + br_ref[...]
    tri = tri_ref[...]
    ig_c = gc[:, 0:A_HEADS]
    b_c = _dot_f32(tri, jax.nn.log_sigmoid(gc[:, A_HEADS:2 * A_HEADS]))
    ig_r = gr[0:A_HEADS, :]
    b_r = _dot_f32(jax.nn.log_sigmoid(gr[A_HEADS:2 * A_HEADS, :]), tri.T)
    causal = _iota((L, L), 1) <= _iota((L, L), 0)

    for hd in range(A_HEADS):
        sl = slice(hd * A_DH, (hd + 1) * A_DH)
        q_h = (qk[:, sl] * A_DH ** -0.5).astype(MXU_DT)
        k_f = qk[:, A_W + hd * A_DH:A_W + (hd + 1) * A_DH]
        k_h = k_f.astype(MXU_DT)
        v_h = v_ref[0][:, sl]
        bi = b_c[:, hd:hd + 1]
        ic = ig_c[:, hd:hd + 1]
        dmat = jnp.where(causal, bi - b_r[hd:hd + 1, :] + ig_r[hd:hd + 1, :], NEG)
        m_prev = m_scr[hd][:, 0:1]
        m_inter = bi + m_prev
        m_t = jnp.maximum(m_inter, jnp.max(dmat, axis=1, keepdims=True))
        e_inter = jnp.exp(m_inter - m_t)
        s = _dot_nt(q_h, k_h) * jnp.exp(dmat - m_t)
        ct = ct_scr[hd]
        nrow = n_scr[hd]
        num = e_inter * _dot(q_h, ct.astype(MXU_DT)) + _dot(s.astype(MXU_DT), v_h)
        den = e_inter * jnp.sum(q_h.astype(f32) * nrow, axis=1, keepdims=True) + jnp.sum(s, axis=1, keepdims=True)
        hc = num / jnp.maximum(jnp.abs(den), jnp.exp(-m_t))
        b_last = bi[L - 1:L, :]
        dec = b_last - bi + ic
        m_new = jnp.maximum(b_last + m_prev, jnp.max(dec, axis=0, keepdims=True))
        wgt = jnp.exp(dec - m_new)
        e_st = jnp.exp(b_last + m_prev - m_new)
        kw = k_f * wgt
        ct_scr[hd] = e_st * ct + _dot_tn(kw.astype(MXU_DT), v_h)
        n_scr[hd] = e_st * nrow + jnp.sum(kw, axis=0, keepdims=True)
        m_scr[hd] = jnp.broadcast_to(m_new, (1, LANES))
        mu = jnp.mean(hc, axis=1, keepdims=True)
        var = jnp.mean(jnp.square(hc - mu), axis=1, keepdims=True)
        hn = (hc - mu) * lax.rsqrt(var + LN_EPS) * ng_ref[:, sl]
        out_ref[0, :, sl] = (hn * jax.nn.sigmoid(o_ref[0][:, sl])).astype(out_ref.dtype)


def _mlstm(qk, av, ao, small, conv_w, i_b, f_b, norm_g):
    B, T, _ = qk.shape
    L = min(MLSTM_CHUNK, T)
    N = T // L
    f32 = jnp.float32
    gates = small[..., 0:2 * A_HEADS]
    gates_r = gates.reshape(B, N, L, 2 * A_HEADS).transpose(0, 1, 3, 2)
    bias = jnp.concatenate([i_b, f_b]).astype(f32)
    tri = jnp.tril(jnp.ones((L, L), f32))
    row = lambda w: pl.BlockSpec((1, L, w), lambda b, c: (b, c, 0))
    kern = functools.partial(_mlstm_kernel, L=L)
    return pl.pallas_call(
        kern,
        grid=(B, N),
        in_specs=[row(2 * A_W), row(A_W), row(A_W), row(2 * A_HEADS),
                  pl.BlockSpec((1, 1, 2 * A_HEADS, L), lambda b, c: (b, c, 0, 0)),
                  _const_spec((A_CONV, 2 * A_W)), _const_spec((1, 2 * A_HEADS)), _const_spec((2 * A_HEADS, 1)),
                  _const_spec((1, A_W)), _const_spec((L, L))],
        out_specs=row(A_W),
        out_shape=jax.ShapeDtypeStruct((B, T, A_W), MXU_DT),
        scratch_shapes=[pltpu.VMEM((L + 2 * SUBLANES, 2 * A_W), f32),
                        pltpu.VMEM((A_HEADS, A_DH, A_DH), f32),
                        pltpu.VMEM((A_HEADS, 1, A_DH), f32),
                        pltpu.VMEM((A_HEADS, 1, LANES), f32)],
        compiler_params=_cparams("parallel", "arbitrary"),
    )(qk, av, ao, gates, gates_r, conv_w, bias[None, :], bias[:, None], norm_g[None, :], tri)


def _nsa_cmp_kernel(x_ref, w1a_ref, w1b_ref, w2_ref, pos_ref, out_ref):
    n = x_ref.shape[2]
    half = x_ref.shape[3]
    for j in range(2):
        bias = (_dot(pos_ref[j, :, 0:half].astype(MXU_DT), w1a_ref[j])
                + _dot(pos_ref[j, :, half:2 * half].astype(MXU_DT), w1b_ref[j]))[0:1, :]
        for g in range(B_KV):
            u = x_ref[0, j * B_KV + g].astype(MXU_DT)
            a = _dot(u, w1a_ref[j])
            bm = _dot(u, w1b_ref[j])
            pre = a + pltpu.roll(bm, n - 1, 0) + bias
            hid = jax.nn.gelu(pre)
            out_ref[0, j * B_KV + g] = _dot(hid.astype(MXU_DT), w2_ref[j])


def _nsa_compress(bc, cmp_pos, cmp_w1, cmp_w2):
    B, _, T, _ = bc.shape
    nblk = T // B_CMP_STRIDE
    half = B_CMP_STRIDE * LANES
    x = bc.reshape(B, 2 * B_KV, nblk, half)
    w1 = jnp.pad(cmp_w1.reshape(2, B_CMP_LEN, B_DH, B_CMP_HID), ((0, 0), (0, 0), (0, LANES - B_DH), (0, 0)))
    w1 = w1.reshape(2, B_CMP_LEN * LANES, B_CMP_HID).astype(MXU_DT)
    w1a, w1b = w1[:, :half], w1[:, half:]
    w2 = _pad_cols(cmp_w2, LANES).astype(MXU_DT)
    pos = jnp.pad(cmp_pos, ((0, 0), (0, 0), (0, LANES - B_DH))).reshape(2, 1, B_CMP_LEN * LANES)
    pos = jnp.broadcast_to(pos, (2, SUBLANES, B_CMP_LEN * LANES))
    return pl.pallas_call(
        _nsa_cmp_kernel,
        grid=(B,),
        in_specs=[pl.BlockSpec((1, 2 * B_KV, nblk, half), lambda b: (b, 0, 0, 0)),
                  _const_spec(w1a.shape), _const_spec(w1b.shape), _const_spec(w2.shape), _const_spec(pos.shape)],
        out_specs=pl.BlockSpec((1, 2 * B_KV, nblk, LANES), lambda b: (b, 0, 0, 0)),
        out_shape=jax.ShapeDtypeStruct((B, 2 * B_KV, nblk, LANES), jnp.float32),
        compiler_params=_cparams("parallel"),
    )(x, w1a, w1b, w2, pos)


def _nsa_kernel(q_ref, kv_ref, cmp_ref, sm_ref, gb_ref, ovt_ref, exp_ref, out_ref, acc_scr, ocmp_scr,
                *, QB, KC, WL, NSB, NSEL):
    f32 = jnp.float32
    i = pl.program_id(1)
    s0 = i * QB
    R = B_HPG * QB
    tq_col = s0 + _iota((QB, 1), 0)
    tq_row = s0 + _iota((1, QB), 1)
    nch = (s0 + QB + KC - 1) // KC
    gates = jax.nn.sigmoid(sm_ref[0] + gb_ref[...])
    ncmp = cmp_ref.shape[2]
    cmp_end = _iota((1, ncmp), 1) * B_CMP_STRIDE + (B_CMP_LEN - 1)
    cmp_bias = jnp.where(cmp_end <= tq_col, 0.0, NEG)[None]
    jb = _iota((LANES, 1), 0)
    cur = lax.shift_right_logical(tq_row, int(np.log2(B_SEL_BLK)))
    forced = (jb == 0) | (jb == cur) | (jb == cur - 1)
    valid = jb * B_SEL_BLK <= tq_row
    wstart = pl.multiple_of(jnp.maximum(s0 + QB - WL, 0), QB)
    wpos = wstart + _iota((1, WL), 1)
    win_bias = jnp.where((wpos <= tq_col) & (wpos > tq_col - B_WIN), 0.0, NEG)[None]

    sel_b = []
    for g in range(B_KV):
        qs = q_ref[0, g * B_HPG:(g + 1) * B_HPG].reshape(R, LANES)
        kcm = cmp_ref[0, g].astype(MXU_DT)
        vcm = cmp_ref[0, B_KV + g].astype(MXU_DT)
        p_cmp = _masked_softmax2(_dot_nt(qs, kcm).reshape(B_HPG, QB, ncmp), cmp_bias)
        ocmp_scr[g * R:(g + 1) * R] = _dot(p_cmp.reshape(R, ncmp).astype(MXU_DT), vcm)
        psum = jnp.sum(p_cmp, axis=0)
        imp_t = lax.dot_general(ovt_ref[...], psum, (((1,), (1,)), ((), ())),
                                preferred_element_type=f32, precision=lax.Precision.HIGHEST)
        sc = jnp.where(forced, 1e6, imp_t)
        sc = jnp.where(valid, sc, NEG)
        sc = jnp.where(jb < NSB, sc, -jnp.inf)
        rank = jnp.zeros((LANES, QB), f32)
        for j in range(NSB):
            cj = jnp.broadcast_to(sc[j:j + 1, :], (LANES, QB))
            ahead = (cj > sc) | ((cj == sc) & (jb > j))
            rank = rank + jnp.where(ahead, 1.0, 0.0)
        sel_b.append(jnp.where(rank < NSEL, 1.0, 0.0).T.astype(MXU_DT))

    acc_scr[...] = jnp.zeros(acc_scr.shape, f32)

    def sel_body(c, carry):
        ms, ls = list(carry[0]), list(carry[1])
        ks = pl.multiple_of(c * KC, KC)
        causal = ks + _iota((1, KC), 1) <= tq_col
        for g in range(B_KV):
            qs = q_ref[0, g * B_HPG:(g + 1) * B_HPG].reshape(R, LANES)
            kc_ = kv_ref[0, 0 * B_KV + g, pl.ds(ks, KC), :]
            vc_ = kv_ref[0, 1 * B_KV + g, pl.ds(ks, KC), :]
            mk = (_dot(sel_b[g], exp_ref[c]) > 0.5) & causal
            s3 = _dot_nt(qs, kc_).reshape(B_HPG, QB, KC) + jnp.where(mk, 0.0, NEG)[None]
            m_new = jnp.maximum(ms[g], jnp.max(s3, axis=-1, keepdims=True))
            p = jnp.exp2(s3 - m_new)
            alpha = jnp.exp2(ms[g] - m_new)
            ls[g] = alpha * ls[g] + jnp.sum(p, axis=-1, keepdims=True)
            ms[g] = m_new
            rows = slice(g * R, (g + 1) * R)
            acc_scr[rows] = alpha.reshape(R, 1) * acc_scr[rows] + _dot(p.reshape(R, KC).astype(MXU_DT), vc_)
        return tuple(ms), tuple(ls)

    m0 = tuple(jnp.full((B_HPG, QB, 1), SOFTMAX_M0, f32) for _ in range(B_KV))
    l0 = tuple(jnp.zeros((B_HPG, QB, 1), f32) for _ in range(B_KV))
    _, l_fin = lax.fori_loop(0, nch, sel_body, (m0, l0))

    for g in range(B_KV):
        rows = slice(g * R, (g + 1) * R)
        qs = q_ref[0, g * B_HPG:(g + 1) * B_HPG].reshape(R, LANES)
        o_sel = acc_scr[rows] / l_fin[g].reshape(R, 1)
        o_cmp = ocmp_scr[rows]
        kw_ = kv_ref[0, 2 * B_KV + g, pl.ds(wstart, WL), :]
        vw_ = kv_ref[0, 3 * B_KV + g, pl.ds(wstart, WL), :]
        p_win = _masked_softmax2(_dot_nt(qs, kw_).reshape(B_HPG, QB, WL), win_bias)
        o_win = _dot(p_win.reshape(R, WL).astype(MXU_DT), vw_)
        for hd in range(B_HPG):
            c0 = 2 * A_HEADS + (g * B_HPG + hd) * 3
            rs = slice(hd * QB, (hd + 1) * QB)
            o = (gates[:, c0:c0 + 1] * o_cmp[rs] + gates[:, c0 + 1:c0 + 2] * o_sel[rs]
                 + gates[:, c0 + 2:c0 + 3] * o_win[rs])
            col = (g * B_HPG + hd) * LANES
            out_ref[0, :, col:col + LANES] = o.astype(out_ref.dtype)


def _nsa(bq, bs, cmp, small, g_b):
    B, _, T, _ = bq.shape
    QB = min(Q_BLOCK, T)
    KC = min(KEY_CHUNK, T)
    WL = min(B_WIN + QB, T)
    NSB = T // B_SEL_BLK
    NSEL = min(B_SEL_N, NSB)
    ncmp = cmp.shape[2]
    f32 = jnp.float32
    M = (T - B_CMP_LEN) // B_CMP_STRIDE + 1
    mi, jj = np.arange(ncmp)[None, :], np.arange(LANES)[:, None]
    ovt = ((mi * B_CMP_STRIDE < (jj + 1) * B_SEL_BLK) & (mi * B_CMP_STRIDE + B_CMP_LEN > jj * B_SEL_BLK)
           & (mi < M) & (jj < NSB)).astype(np.float32)
    kk = np.arange(T)
    expand = (kk[None, :] // B_SEL_BLK == np.arange(LANES)[:, None]).astype(np.float32)
    expand = jnp.asarray(expand.reshape(LANES, T // KC, KC).transpose(1, 0, 2), MXU_DT)
    gb = jnp.zeros((1, LANES), f32).at[0, 2 * A_HEADS:2 * A_HEADS + 3 * B_HEADS].set(g_b)
    kern = functools.partial(_nsa_kernel, QB=QB, KC=KC, WL=WL, NSB=NSB, NSEL=NSEL)
    return pl.pallas_call(
        kern,
        grid=(B, T // QB),
        in_specs=[pl.BlockSpec((1, B_HEADS, QB, LANES), lambda b, i: (b, 0, i, 0)),
                  pl.BlockSpec((1, 4 * B_KV, T, LANES), lambda b, i: (b, 0, 0, 0)),
                  pl.BlockSpec((1, 2 * B_KV, ncmp, LANES), lambda b, i: (b, 0, 0, 0)),
                  pl.BlockSpec((1, QB, LANES), lambda b, i: (b, i, 0)),
                  _const_spec((1, LANES)), _const_spec(ovt.shape), _const_spec(expand.shape)],
        out_specs=pl.BlockSpec((1, QB, B_HEADS * LANES), lambda b, i: (b, i, 0)),
        out_shape=jax.ShapeDtypeStruct((B, T, B_HEADS * LANES), MXU_DT),
        scratch_shapes=[pltpu.VMEM((B_HEADS * QB, LANES), f32), pltpu.VMEM((B_HEADS * QB, LANES), f32)],
        compiler_params=_cparams("parallel", "arbitrary"),
    )(bq, bs, cmp, small, gb, jnp.asarray(ovt), expand)


def _rope_rows(d):
    inv = ROPE_BASE ** (-jnp.arange(0, d, 2, dtype=jnp.float32) / d)
    z = jnp.zeros((C_DR - d,), jnp.float32)
    inv64 = jnp.concatenate([inv, inv, z])
    sgn64 = jnp.concatenate([-jnp.ones(d // 2), jnp.ones(d // 2), z]).astype(jnp.float32)
    return inv64, sgn64


def _odd_prep_kernel(h_ref, pos_ref, win_ref, wqb_ref, wiq_ref, wuk_ref, qn_ref, kvn_ref, ikg_ref, ikb_ref,
                     rope_ref, perm_ref, qa_ref, kv_ref, qi_ref, ki_ref, wi_ref):
    f32 = jnp.float32
    z = _dot(h_ref[0].astype(MXU_DT), win_ref[...])
    pos = pos_ref[0].astype(f32)
    rr = rope_ref[...]
    ang_q = pos * rr[0:1, :]
    cos_q, sin_q = jnp.cos(ang_q), jnp.sin(ang_q) * rr[1:2, :]
    ang_i = pos * rr[2:3, :]
    cos_i, sin_i = jnp.cos(ang_i), jnp.sin(ang_i) * rr[3:4, :]

    def rms(x, g):
        return x * lax.rsqrt(jnp.mean(jnp.square(x), axis=-1, keepdims=True) + LN_EPS) * g

    cq = rms(z[:, 0:C_QL], qn_ref[...])
    ckv = rms(z[:, C_QL:C_QL + C_KVL], kvn_ref[...])
    o = C_QL + C_KVL
    k_rope = z[:, o:o + LANES] * cos_q + z[:, o + LANES:o + 2 * LANES] * sin_q
    kv_ref[0, :, 0:C_KVL] = ckv.astype(kv_ref.dtype)
    kv_ref[0, :, C_KVL:C_KPAD] = k_rope.astype(kv_ref.dtype)
    ik = z[:, o + 2 * LANES:o + 3 * LANES]
    real = _iota((1, LANES), 1) < C_IDX_DH
    mu = jnp.sum(ik, axis=-1, keepdims=True) / C_IDX_DH
    dlt = jnp.where(real, ik - mu, 0.0)
    var = jnp.sum(jnp.square(dlt), axis=-1, keepdims=True) / C_IDX_DH
    ki = dlt * lax.rsqrt(var + LN_EPS) * ikg_ref[...] + ikb_ref[...]
    ki = ki * cos_i + _dot_f32(ki, perm_ref[...]) * sin_i
    ki_ref[0] = ki.astype(ki_ref.dtype)
    wi_ref[0] = z[:, o + 3 * LANES:o + 4 * LANES] * (C_IDX_HEADS ** -0.5 * C_IDX_DH ** -0.5)
    cqb = cq.astype(MXU_DT)
    qf = _dot(cqb, wqb_ref[...])
    qi = _dot(cqb, wiq_ref[...])
    scale = (C_DN + C_DR) ** -0.5 * LOG2E
    nh = C_HEADS * LANES
    for hd in range(C_HEADS):
        q_abs = _dot(qf[:, hd * C_DN:(hd + 1) * C_DN].astype(MXU_DT), wuk_ref[hd])
        cs = slice(hd * LANES, (hd + 1) * LANES)
        q_rope = qf[:, nh:2 * nh][:, cs] * cos_q + qf[:, 2 * nh:3 * nh][:, cs] * sin_q
        qa_ref[0, hd, :, 0:C_KVL] = (q_abs * scale).astype(qa_ref.dtype)
        qa_ref[0, hd, :, C_KVL:C_KPAD] = (q_rope * scale).astype(qa_ref.dtype)
        qi_h = qi[:, 0:nh][:, cs] * cos_i + qi[:, nh:2 * nh][:, cs] * sin_i
        qi_ref[0, hd] = qi_h.astype(qi_ref.dtype)


def _rot_cols(w, n_heads, dh, d):
    w = w.reshape(w.shape[0], n_heads, dh)
    h = d // 2
    return jnp.concatenate([w[..., h:d], w[..., 0:h], jnp.zeros_like(w[..., d:])], -1).reshape(w.shape[0], n_heads * dh)


def _odd_prep(h3, pos3, w_in, q_norm, kv_norm, w_qb, w_uk, w_iq, ik_g, ik_b, tm):
    B, T, D = h3.shape
    f32 = jnp.float32
    o = 0
    parts = []
    for s in (C_QL, C_KVL, C_DR, C_IDX_DH, C_IDX_HEADS):
        parts.append(w_in[:, o:o + s]); o += s
    w_cq, w_ckv, w_kr, w_ik, w_iw = parts
    pc = lambda w: _pad_cols(w, LANES)
    win = jnp.concatenate([w_cq, w_ckv, pc(w_kr), pc(_rot_cols(w_kr, 1, C_DR, C_DR)), pc(w_ik), pc(w_iw)],
                          -1).astype(MXU_DT)
    wq = w_qb.reshape(C_QL, C_HEADS, C_DN + C_DR)
    w_nope = wq[..., :C_DN].reshape(C_QL, C_HEADS * C_DN)
    w_rope = wq[..., C_DN:].reshape(C_QL, C_HEADS * C_DR)
    wqb = jnp.concatenate([w_nope, _pad_heads(w_rope, C_HEADS, C_DR),
                           _pad_heads(_rot_cols(w_rope, C_HEADS, C_DR, C_DR), C_HEADS, C_DR)], -1).astype(MXU_DT)
    wiq = jnp.concatenate([_pad_heads(w_iq, C_IDX_HEADS, C_IDX_DH),
                           _pad_heads(_rot_cols(w_iq, C_IDX_HEADS, C_IDX_DH, C_IDX_DR), C_IDX_HEADS, C_IDX_DH)],
                          -1).astype(MXU_DT)
    wuk = w_uk.transpose(1, 2, 0).astype(MXU_DT)
    inv_q, sgn_q = _rope_rows(C_DR)
    inv_i, sgn_i = _rope_rows(C_IDX_DR)
    rope = jnp.stack([jnp.tile(v, LANES // C_DR) for v in (inv_q, sgn_q, inv_i, sgn_i)])
    rope = jnp.concatenate([rope, jnp.zeros((SUBLANES - 4, LANES), f32)])
    hh = C_IDX_DR // 2
    src = np.arange(LANES)
    src[:hh] += hh
    src[hh:C_IDX_DR] -= hh
    perm = np.zeros((LANES, LANES), np.float32)
    perm[src, np.arange(LANES)] = 1.0
    ikg = _pad_cols(ik_g[None, :], LANES)
    ikb = _pad_cols(ik_b[None, :], LANES)
    out_shape = (
        jax.ShapeDtypeStruct((B, C_HEADS, T, C_KPAD), MXU_DT),
        jax.ShapeDtypeStruct((B, T, C_KPAD), MXU_DT),
        jax.ShapeDtypeStruct((B, C_IDX_HEADS, T, LANES), MXU_DT),
        jax.ShapeDtypeStruct((B, T, LANES), MXU_DT),
        jax.ShapeDtypeStruct((B, T, LANES), f32),
    )
    row = lambda w: pl.BlockSpec((1, tm, w), lambda b, i: (b, i, 0))
    hm = lambda w: pl.BlockSpec((1, C_HEADS, tm, w), lambda b, i: (b, 0, i, 0))
    return pl.pallas_call(
        _odd_prep_kernel,
        grid=(B, T // tm),
        in_specs=[row(D), row(1), _const_spec(win.shape), _const_spec(wqb.shape), _const_spec(wiq.shape),
                  _const_spec(wuk.shape), _const_spec((1, C_QL)), _const_spec((1, C_KVL)),
                  _const_spec((1, LANES)), _const_spec((1, LANES)), _const_spec(rope.shape),
                  _const_spec(perm.shape)],
        out_specs=(hm(C_KPAD), row(C_KPAD), hm(LANES), row(LANES), row(LANES)),
        out_shape=out_shape,
        compiler_params=_cparams("parallel", "parallel"),
    )(h3, pos3, win, wqb, wiq, wuk, q_norm[None, :], kv_norm[None, :], ikg, ikb, rope, jnp.asarray(perm))


def _dsa_kernel(qa_ref, qi_ref, wi_ref, kv_ref, ki_ref, wuv_ref, eye_ref, tri_ref, out_ref,
                key_scr, hi_scr, lo_scr, acc_scr, *, QB, KC, SUB, TOPK, HG):
    f32, i32, i16 = jnp.float32, jnp.int32, jnp.int16
    i = pl.program_id(1)
    s0 = i * QB
    H = C_HEADS
    NG = H // HG
    RG = HG * QB
    nch = (s0 + QB + KC - 1) // KC
    PK = PACKED_ROWS
    tq_row = s0 + _iota((1, QB), 1)
    w_t = wi_ref[0].T
    one, zero = jnp.ones((), MXU_DT), jnp.zeros((), MXU_DT)

    def idx_body(c, _):
        ks = pl.multiple_of(c * KC, KC)
        kic = ki_ref[0, pl.ds(ks, KC), :]
        isc = None
        for hd in range(C_IDX_HEADS):
            s = jnp.maximum(_dot_nt(kic, qi_ref[0, hd]), 0.0) * w_t[hd:hd + 1, :]
            isc = s if isc is None else isc + s
        isc = jnp.where(isc == 0.0, 0.0, isc)
        kpos = ks + _iota((KC, 1), 0)
        isc = jnp.where(kpos <= tq_row, isc, NEG)
        bits = lax.bitcast_convert_type(isc, i32)
        key = jnp.where(bits < 0, bits ^ jnp.int32(0x7FFFFFFF), bits)
        key_scr[c] = key
        k3 = key.reshape(KC // PK, PK, QB)
        hi_scr[c] = lax.shift_right_arithmetic(k3, 16).astype(i16)
        lo_scr[c] = ((k3 & 0xFFFF) - 32768).astype(i16)
        return 0

    lax.fori_loop(0, nch, idx_body, 0)

    def rep16(v):
        return jnp.broadcast_to(v, (PK, QB)).astype(i16)[None]

    def count16(pred):
        def body(c, acc):
            x = jnp.where(pred(hi_scr[c], lo_scr[c]), one, zero)
            for r in range(KC // PK):
                acc = acc + x[r]
            return acc
        acc = lax.fori_loop(0, nch, body, jnp.zeros((PK, QB), MXU_DT))
        return jnp.sum(acc.astype(f32), axis=0, keepdims=True)

    def bisect(pick, base):
        def body(b, t):
            cand = t + lax.shift_left(jnp.int32(1), 15 - b)
            c16 = rep16(cand)
            return jnp.where(base + count16(lambda h, l: pick(h, l) >= c16) >= TOPK, cand, t)
        return lax.fori_loop(0, 16, body, jnp.full((1, QB), -32768, i32))

    thi = bisect(lambda h, l: h, 0.0)
    thi16 = rep16(thi)

    def bucket_body(c, _):
        lo_scr[c] = jnp.where(hi_scr[c] == thi16, lo_scr[c], jnp.full((), -32768, i16))
        return 0

    lax.fori_loop(0, nch, bucket_body, 0)
    n_hi = count16(lambda h, l: h > thi16)
    tlo = bisect(lambda h, l: l, n_hi)
    tlo16 = rep16(tlo)
    thr = thi * 65536 + (tlo + 32768)
    room = TOPK - (n_hi + count16(lambda h, l: l > tlo16))
    n_eq = count16(lambda h, l: (h == thi16) & (l == tlo16))

    @pl.when(jnp.max(jnp.where(n_eq > room, 1, 0)) > 0)
    def _():
        def body(c, before):
            key = key_scr[c]
            eq = key == thr
            seen = before + _dot(tri_ref[...], jnp.where(eq, 1.0, 0.0).astype(MXU_DT))
            key_scr[c] = jnp.where(eq & (seen > room), key - 1, key)
            return seen[KC - 1:KC, :]
        lax.fori_loop(0, nch, body, jnp.zeros((1, QB), f32))

    acc_scr[...] = jnp.zeros(acc_scr.shape, f32)

    def att_body(c, carry):
        ms, ls = list(carry[0]), list(carry[1])
        ks = pl.multiple_of(c * KC, KC)
        key = key_scr[c]
        kpos = ks + _iota((KC, 1), 0)
        sel_t = (key >= thr) & (kpos <= tq_row)
        keep = _dot_nt(eye_ref[...], jnp.where(sel_t, 1.0, 0.0).astype(MXU_DT))
        bias = jnp.where(keep > 0.5, 0.0, NEG)
        for u in range(KC // SUB):
            kvc = kv_ref[0, pl.ds(pl.multiple_of(ks + u * SUB, SUB), SUB), :]
            b_u = bias[:, u * SUB:(u + 1) * SUB][None]
            for g in range(NG):
                qg = qa_ref[0, g * HG:(g + 1) * HG].reshape(RG, C_KPAD)
                s = _dot_nt(qg, kvc).reshape(HG, QB, SUB) + b_u
                m_new = jnp.maximum(ms[g], jnp.max(s, axis=-1, keepdims=True))
                p = jnp.exp2(s - m_new)
                alpha = jnp.exp2(ms[g] - m_new)
                ls[g] = alpha * ls[g] + jnp.sum(p, axis=-1, keepdims=True)
                ms[g] = m_new
                rows = slice(g * RG, (g + 1) * RG)
                pv = _dot(p.reshape(RG, SUB).astype(MXU_DT), kvc[:, 0:C_KVL])
                acc_scr[rows] = alpha.reshape(RG, 1) * acc_scr[rows] + pv
        return tuple(ms), tuple(ls)

    m0 = tuple(jnp.full((HG, QB, 1), SOFTMAX_M0, f32) for _ in range(NG))
    l0 = tuple(jnp.zeros((HG, QB, 1), f32) for _ in range(NG))
    _, l_fin = lax.fori_loop(0, nch, att_body, (m0, l0))
    for g in range(NG):
        o_lat = (acc_scr[g * RG:(g + 1) * RG] / l_fin[g].reshape(RG, 1)).astype(MXU_DT)
        for k in range(HG):
            hd = g * HG + k
            out_ref[0, :, hd * C_DV:(hd + 1) * C_DV] = _dot(o_lat[k * QB:(k + 1) * QB], wuv_ref[hd]).astype(out_ref.dtype)


def _dsa(qa, kv, qi, ki, wi, w_uv):
    B, H, T, _ = qa.shape
    QB = min(DSA_Q_BLOCK, T)
    KC = min(KEY_CHUNK, T)
    topk = min(C_TOPK, T // 4)
    wuv = w_uv.transpose(1, 0, 2).astype(MXU_DT)
    assert T // PACKED_ROWS <= 256
    eye = jnp.eye(QB, dtype=MXU_DT)
    tri = jnp.tril(jnp.ones((KC, KC), MXU_DT))
    kern = functools.partial(_dsa_kernel, QB=QB, KC=KC, SUB=min(KEY_SUB, KC), TOPK=topk, HG=DSA_HEAD_GROUP)
    half_words = pltpu.VMEM((T // KC, KC // PACKED_ROWS, PACKED_ROWS, QB), jnp.int16)
    return pl.pallas_call(
        kern,
        grid=(B, T // QB),
        in_specs=[pl.BlockSpec((1, H, QB, C_KPAD), lambda b, i: (b, 0, i, 0)),
                  pl.BlockSpec((1, H, QB, LANES), lambda b, i: (b, 0, i, 0)),
                  pl.BlockSpec((1, QB, LANES), lambda b, i: (b, i, 0)),
                  pl.BlockSpec((1, T, C_KPAD), lambda b, i: (b, 0, 0)),
                  pl.BlockSpec((1, T, LANES), lambda b, i: (b, 0, 0)),
                  _const_spec(wuv.shape), _const_spec(eye.shape), _const_spec(tri.shape)],
        out_specs=pl.BlockSpec((1, QB, H * C_DV), lambda b, i: (b, i, 0)),
        out_shape=jax.ShapeDtypeStruct((B, T, H * C_DV), MXU_DT),
        scratch_shapes=[pltpu.VMEM((T // KC, KC, QB), jnp.int32),
                        half_words, half_words,
                        pltpu.VMEM((H * QB, C_KVL), jnp.float32)],
        compiler_params=_cparams("parallel", "arbitrary"),
    )(qa, qi, wi, kv, ki, wuv, eye, tri)


def _post_kernel(*refs, n_mix, n_ff):
    h_ref = refs[0]
    mix = refs[1:1 + 2 * n_mix]
    g1, b1, w1_ref, w2_ref, g2, b2, wg_ref, p_ref, wp_ref, out_ref = refs[1 + 2 * n_mix:]
    h = h_ref[...]
    y = _dot(mix[0][...].astype(MXU_DT), mix[1][...])
    for k in range(1, n_mix):
        y = y + _dot(mix[2 * k][...].astype(MXU_DT), mix[2 * k + 1][...])
    h1 = _layer_norm(DN_ALPHA * h + y, g1[...], b1[...])
    h1b = h1.astype(MXU_DT)
    ff = D_FF // n_ff
    u = None
    for k in range(n_ff):
        a = jnp.square(jnp.maximum(_dot(h1b, w1_ref[:, k * ff:(k + 1) * ff]), 0.0))
        t = _dot(a.astype(MXU_DT), w2_ref[k * ff:(k + 1) * ff, :])
        u = t if u is None else u + t
    h2 = _layer_norm(DN_ALPHA * h1 + u, g2[...], b2[...])
    gate = jax.nn.sigmoid(_dot(h2.astype(MXU_DT), wg_ref[...]))
    out_ref[...] = h2 + gate * _dot(p_ref[...].astype(MXU_DT), wp_ref[...])


def _post(h2d, mixes, ln1_g, ln1_b, w1, w2, ln2_g, ln2_b, wg, p2d, wp, tm):
    M, D = h2d.shape
    row = lambda w: pl.BlockSpec((tm, w), lambda i: (i, 0))
    vec = lambda v: v[None, :]
    in_specs = [row(D)]
    args = [h2d]
    for x, w in mixes:
        in_specs += [row(x.shape[1]), _const_spec(w.shape)]
        args += [x, w]
    in_specs += [_const_spec((1, D)), _const_spec((1, D)), _const_spec(w1.shape), _const_spec(w2.shape),
                 _const_spec((1, D)), _const_spec((1, D)), _const_spec(wg.shape), row(D_PLE), _const_spec(wp.shape)]
    args += [vec(ln1_g), vec(ln1_b), w1, w2, vec(ln2_g), vec(ln2_b), wg, p2d, wp]
    kern = functools.partial(_post_kernel, n_mix=len(mixes), n_ff=4)
    return pl.pallas_call(
        kern,
        grid=(M // tm,),
        in_specs=in_specs,
        out_specs=row(D),
        out_shape=jax.ShapeDtypeStruct((M, D), jnp.float32),
        compiler_params=_cparams("parallel"),
    )(*args)


def kernel(x, p, positions, e_w_in, e_a_conv, e_a_i_b, e_a_f_b, e_a_norm, e_b_cmp_pos, e_b_cmp_w1, e_b_cmp_w2, e_b_g_b, e_w_out, o_w_in, o_q_norm, o_kv_norm, o_w_qb, o_w_uk, o_w_uv, o_w_iq, o_ik_g, o_ik_b, o_w_out, ln1_g, ln1_b, ln2_g, ln2_b, mlp_w1, mlp_w2, ple_gate_w, ple_w):
    B, T, D = x.shape
    M = B * T
    tm = min(ROW_TILE, T)
    h = x
    pos3 = positions[..., None]
    bf = lambda w: w.astype(MXU_DT)
    for i in range(DEPTH):
        j = i // 2
        if i % 2 == 0:
            qk, av, ao, small, bq, bc, bs = _even_proj(h, _even_w_in_aug(e_w_in[j]), tm)
            ya = _mlstm(qk, av, ao, small, e_a_conv[j], e_a_i_b[j], e_a_f_b[j], e_a_norm[j])
            cmp = _nsa_compress(bc, e_b_cmp_pos[j], e_b_cmp_w1[j], e_b_cmp_w2[j])
            yb = _nsa(bq, bs, cmp, small, e_b_g_b[j])
            w_out = e_w_out[j]
            mixes = [(ya.reshape(M, A_W), bf(w_out[:A_W])),
                     (yb.reshape(M, B_HEADS * LANES), bf(_pad_heads(w_out[A_W:], B_HEADS, B_DH, axis=0)))]
        else:
            qa, kv, qi, ki, wi = _odd_prep(h, pos3, o_w_in[j], o_q_norm[j], o_kv_norm[j], o_w_qb[j], o_w_uk[j],
                                           o_w_iq[j], o_ik_g[j], o_ik_b[j], tm)
            o = _dsa(qa, kv, qi, ki, wi, o_w_uv[j])
            mixes = [(o.reshape(M, C_HEADS * C_DV), bf(o_w_out[j]))]
        h = _post(h.reshape(M, D), mixes, ln1_g[i], ln1_b[i], bf(mlp_w1[i]), bf(mlp_w2[i]), ln2_g[i], ln2_b[i],
                  bf(ple_gate_w[i]), p[i].reshape(M, D_PLE), bf(ple_w[i]), tm).reshape(B, T, D)
    return h
```

```python
import functools

import numpy as np
import jax
import jax.numpy as jnp
from jax import lax
from jax.experimental import pallas as pl
from jax.experimental.pallas import tpu as pltpu

D_MODEL = 1024
DEPTH = 4
D_PLE = 256
D_FF = 4 * D_MODEL
DN_ALPHA = (2.0 * DEPTH) ** 0.25
LN_EPS = 1e-5
NEG = -1e30

A_HEADS = 4
A_DH = D_MODEL // 8
A_W = A_HEADS * A_DH
A_CONV = 4

B_HEADS = 8
B_DH = 64
B_KV = 2
B_HPG = B_HEADS // B_KV
B_CMP_LEN = 32
B_CMP_STRIDE = 16
B_CMP_HID = 128
B_SEL_BLK = 64
B_SEL_N = 16
B_WIN = 512

C_HEADS = 8
C_DN = 128
C_DR = 64
C_DV = 128
C_QL = 512
C_KVL = 256
C_IDX_HEADS = 8
C_IDX_DH = 64
C_IDX_DR = 32
C_TOPK = 256
ROPE_BASE = 10000.0

LANES = 128
SUBLANES = 8
PACKED_ROWS = 16
VMEM_LIMIT_BYTES = 56 * 2**20
MXU_DT = jnp.bfloat16
INT_MIN = -2**31

MLSTM_CHUNK = 256
ROW_TILE = 256
MLP_ROW_TILE = 512
Q_BLOCK = 256
DSA_Q_BLOCK = 256
DSA_HEAD_GROUP = 4
KEY_CHUNK = 512
KEY_SUB = 512
C_KPAD = 384

LOG2E = 1.4426950408889634
SOFTMAX_M0 = 0.5 * NEG
SOFTMAX_TINY = 1e-30


def _cparams(*sem):
    return pltpu.CompilerParams(dimension_semantics=sem, vmem_limit_bytes=VMEM_LIMIT_BYTES)


def _const_spec(shape):
    nd = len(shape)
    return pl.BlockSpec(shape, lambda *_: (0,) * nd, pipeline_mode=pl.Buffered(1))


def _dot(a, b):
    return jnp.dot(a, b, preferred_element_type=jnp.float32)


def _dot_nt(a, b):
    return lax.dot_general(a, b, (((1,), (1,)), ((), ())), preferred_element_type=jnp.float32)


def _dot_tn(a, b):
    return lax.dot_general(a, b, (((0,), (0,)), ((), ())), preferred_element_type=jnp.float32)


def _dot_f32(a, b):
    return jnp.dot(a, b, preferred_element_type=jnp.float32, precision=lax.Precision.HIGHEST)


def _layer_norm(x, g, b):
    mu = jnp.mean(x, axis=-1, keepdims=True)
    var = jnp.mean(jnp.square(x - mu), axis=-1, keepdims=True)
    return (x - mu) * lax.rsqrt(var + LN_EPS) * g + b


def _masked_softmax2(s, bias):
    s = s + bias
    m = jnp.maximum(jnp.max(s, axis=-1, keepdims=True), SOFTMAX_M0)
    e = jnp.exp2(s - m)
    return e / jnp.maximum(jnp.sum(e, axis=-1, keepdims=True), SOFTMAX_TINY)


def _iota(shape, dim):
    return lax.broadcasted_iota(jnp.int32, shape, dim)


def _pad_heads(w, n_heads, dh, axis=-1):
    axis = axis % w.ndim
    shp = w.shape[:axis] + (n_heads, dh) + w.shape[axis + 1:]
    w = w.reshape(shp)
    pad = [(0, 0)] * w.ndim
    pad[axis + 1] = (0, LANES - dh)
    w = jnp.pad(w, pad)
    return w.reshape(shp[:axis] + (n_heads * LANES,) + shp[axis + 2:])


def _pad_cols(w, width):
    return jnp.pad(w, [(0, 0)] * (w.ndim - 1) + [(0, width - w.shape[-1])])


def _even_proj_kernel(h_ref, w_ref, qk_ref, av_ref, ao_ref, sm_ref, bq_ref, bc_ref, bs_ref):
    z = _dot(h_ref[0].astype(MXU_DT), w_ref[...])
    o = 0
    qk_ref[0] = z[:, o:o + 2 * A_W]; o += 2 * A_W
    av_ref[0] = z[:, o:o + A_W].astype(av_ref.dtype); o += A_W
    ao_ref[0] = z[:, o:o + A_W]; o += A_W
    sm_ref[0] = z[:, o:o + LANES]; o += LANES
    for hd in range(B_HEADS):
        bq_ref[0, hd] = z[:, o:o + LANES].astype(bq_ref.dtype); o += LANES
    for j in range(2 * B_KV):
        bc_ref[0, j] = z[:, o:o + LANES]; o += LANES
    for j in range(4 * B_KV):
        bs_ref[0, j] = z[:, o:o + LANES].astype(bs_ref.dtype); o += LANES


def _even_proj(h3, w_aug, tm):
    B, T, D = h3.shape
    n = w_aug.shape[1]
    f32 = jnp.float32
    out_shape = (
        jax.ShapeDtypeStruct((B, T, 2 * A_W), f32),
        jax.ShapeDtypeStruct((B, T, A_W), MXU_DT),
        jax.ShapeDtypeStruct((B, T, A_W), f32),
        jax.ShapeDtypeStruct((B, T, LANES), f32),
        jax.ShapeDtypeStruct((B, B_HEADS, T, LANES), MXU_DT),
        jax.ShapeDtypeStruct((B, 2 * B_KV, T, LANES), f32),
        jax.ShapeDtypeStruct((B, 4 * B_KV, T, LANES), MXU_DT),
    )
    row = lambda w: pl.BlockSpec((1, tm, w), lambda b, i: (b, i, 0))
    hm = lambda nh: pl.BlockSpec((1, nh, tm, LANES), lambda b, i: (b, 0, i, 0))
    return pl.pallas_call(
        _even_proj_kernel,
        grid=(B, T // tm),
        in_specs=[row(D), _const_spec((D, n))],
        out_specs=(row(2 * A_W), row(A_W), row(A_W), row(LANES), hm(B_HEADS), hm(2 * B_KV), hm(4 * B_KV)),
        out_shape=out_shape,
        compiler_params=_cparams("parallel", "parallel"),
    )(h3, w_aug)


def _even_w_in_aug(w_in):
    sizes = (A_W, A_W, A_W, A_W, A_HEADS, A_HEADS, B_HEADS * B_DH) + (B_KV * B_DH,) * 6 + (3 * B_HEADS,)
    parts, o = [], 0
    for s in sizes:
        parts.append(w_in[:, o:o + s]); o += s
    aq, ak, av, ao, ai, af, bq, bkc, bvc, bks, bvs, bkw, bvw, bg = parts
    small = _pad_cols(jnp.concatenate([ai, af, bg], -1), LANES)
    ph = lambda w: _pad_heads(w, B_KV, B_DH)
    bq = bq * (B_DH ** -0.5 * LOG2E)
    cols = [aq, ak, av, ao, small, _pad_heads(bq, B_HEADS, B_DH),
            ph(bkc), ph(bvc), ph(bks), ph(bvs), ph(bkw), ph(bvw)]
    return jnp.concatenate(cols, -1).astype(MXU_DT)


def _mlstm_kernel(qk_ref, v_ref, o_ref, gc_ref, gr_ref, cw_ref, bc_ref, br_ref, ng_ref, tri_ref, out_ref,
                  xs_scr, ct_scr, n_scr, m_scr, *, L):
    c = pl.program_id(1)
    f32 = jnp.float32
    W2 = 2 * A_W

    @pl.when(c == 0)
    def _():
        xs_scr[0:SUBLANES, :] = jnp.zeros((SUBLANES, W2), f32)
        ct_scr[...] = jnp.zeros(ct_scr.shape, f32)
        n_scr[...] = jnp.zeros(n_scr.shape, f32)
        m_scr[...] = jnp.zeros(m_scr.shape, f32)

    xs_scr[SUBLANES:SUBLANES + L, :] = qk_ref[0]
    cw = cw_ref[...]
    base = SUBLANES - (A_CONV - 1)
    acc = xs_scr[base:base + L, :] * cw[0:1, :]
    for j in range(1, A_CONV):
        acc = acc + xs_scr[base + j:base + j + L, :] * cw[j:j + 1, :]
    qk = acc * jax.nn.sigmoid(acc)
    xs_scr[0:SUBLANES, :] = xs_scr[L:L + SUBLANES, :]

    gc = gc_ref[0] + bc_ref[...]
    gr = gr_ref[0, 0] + br_ref[...]
    tri = tri_ref[...]
    ig_c = gc[:, 0:A_HEADS]
    b_c = _dot_f32(tri, jax.nn.log_sigmoid(gc[:, A_HEADS:2 * A_HEADS]))
    ig_r = gr[0:A_HEADS, :]
    b_r = _dot_f32(jax.nn.log_sigmoid(gr[A_HEADS:2 * A_HEADS, :]), tri.T)
    causal = _iota((L, L), 1) <= _iota((L, L), 0)

    for hd in range(A_HEADS):
        sl = slice(hd * A_DH, (hd + 1) * A_DH)
        q_h = (qk[:, sl] * A_DH ** -0.5).astype(MXU_DT)
        k_f = qk[:, A_W + hd * A_DH:A_W + (hd + 1) * A_DH]
        k_h = k_f.astype(MXU_DT)
        v_h = v_ref[0][:, sl]
        bi = b_c[:, hd:hd + 1]
        ic = ig_c[:, hd:hd + 1]
        dmat = jnp.where(causal, bi - b_r[hd:hd + 1, :] + ig_r[hd:hd + 1, :], NEG)
        m_prev = m_scr[hd][:, 0:1]
        m_inter = bi + m_prev
        m_t = jnp.maximum(m_inter, jnp.max(dmat, axis=1, keepdims=True))
        e_inter = jnp.exp(m_inter - m_t)
        s = _dot_nt(q_h, k_h) * jnp.exp(dmat - m_t)
        ct = ct_scr[hd]
        nrow = n_scr[hd]
        num = e_inter * _dot(q_h, ct.astype(MXU_DT)) + _dot(s.astype(MXU_DT), v_h)
        den = e_inter * jnp.sum(q_h.astype(f32) * nrow, axis=1, keepdims=True) + jnp.sum(s, axis=1, keepdims=True)
        hc = num / jnp.maximum(jnp.abs(den), jnp.exp(-m_t))
        b_last = bi[L - 1:L, :]
        dec = b_last - bi + ic
        m_new = jnp.maximum(b_last + m_prev, jnp.max(dec, axis=0, keepdims=True))
        wgt = jnp.exp(dec - m_new)
        e_st = jnp.exp(b_last + m_prev - m_new)
        kw = k_f * wgt
        ct_scr[hd] = e_st * ct + _dot_tn(kw.astype(MXU_DT), v_h)
        n_scr[hd] = e_st * nrow + jnp.sum(kw, axis=0, keepdims=True)
        m_scr[hd] = jnp.broadcast_to(m_new, (1, LANES))
        mu = jnp.mean(hc, axis=1, keepdims=True)
        var = jnp.mean(jnp.square(hc - mu), axis=1, keepdims=True)
        hn = (hc - mu) * lax.rsqrt(var + LN_EPS) * ng_ref[:, sl]
        out_ref[0, :, sl] = (hn * jax.nn.sigmoid(o_ref[0][:, sl])).astype(out_ref.dtype)


def _mlstm(qk, av, ao, small, conv_w, i_b, f_b, norm_g):
    B, T, _ = qk.shape
    L = min(MLSTM_CHUNK, T)
    N = T // L
    f32 = jnp.float32
    gates = small[..., 0:2 * A_HEADS]
    gates_r = gates.reshape(B, N, L, 2 * A_HEADS).transpose(0, 1, 3, 2)
    bias = jnp.concatenate([i_b, f_b]).astype(f32)
    tri = jnp.tril(jnp.ones((L, L), f32))
    row = lambda w: pl.BlockSpec((1, L, w), lambda b, c: (b, c, 0))
    kern = functools.partial(_mlstm_kernel, L=L)
    return pl.pallas_call(
        kern,
        grid=(B, N),
        in_specs=[row(2 * A_W), row(A_W), row(A_W), row(2 * A_HEADS),
                  pl.BlockSpec((1, 1, 2 * A_HEADS, L), lambda b, c: (b, c, 0, 0)),
                  _const_spec((A_CONV, 2 * A_W)), _const_spec((1, 2 * A_HEADS)), _const_spec((2 * A_HEADS, 1)),
                  _const_spec((1, A_W)), _const_spec((L, L))],
        out_specs=row(A_W),
        out_shape=jax.ShapeDtypeStruct((B, T, A_W), MXU_DT),
        scratch_shapes=[pltpu.VMEM((L + 2 * SUBLANES, 2 * A_W), f32),
                        pltpu.VMEM((A_HEADS, A_DH, A_DH), f32),
                        pltpu.VMEM((A_HEADS, 1, A_DH), f32),
                        pltpu.VMEM((A_HEADS, 1, LANES), f32)],
        compiler_params=_cparams("parallel", "arbitrary"),
    )(qk, av, ao, gates, gates_r, conv_w, bias[None, :], bias[:, None], norm_g[None, :], tri)


def _nsa_cmp_kernel(x_ref, w1a_ref, w1b_ref, w2_ref, pos_ref, out_ref):
    n = x_ref.shape[2] // B_CMP_STRIDE
    half = B_CMP_STRIDE * LANES
    for j in range(2):
        bias = (_dot(pos_ref[j, :, 0:half].astype(MXU_DT), w1a_ref[j])
                + _dot(pos_ref[j, :, half:2 * half].astype(MXU_DT), w1b_ref[j]))[0:1, :]
        for g in range(B_KV):
            u = jnp.concatenate([x_ref[0, j * B_KV + g, pl.ds(r, n, stride=B_CMP_STRIDE), :]
                                 for r in range(B_CMP_STRIDE)], axis=1).astype(MXU_DT)
            a = _dot(u, w1a_ref[j])
            bm = _dot(u, w1b_ref[j])
            pre = a + pltpu.roll(bm, n - 1, 0) + bias
            hid = jax.nn.gelu(pre)
            out_ref[0, j * B_KV + g] = _dot(hid.astype(MXU_DT), w2_ref[j])


def _nsa_compress(bc, cmp_pos, cmp_w1, cmp_w2):
    B, _, T, _ = bc.shape
    nblk = T // B_CMP_STRIDE
    half = B_CMP_STRIDE * LANES
    w1 = jnp.pad(cmp_w1.reshape(2, B_CMP_LEN, B_DH, B_CMP_HID), ((0, 0), (0, 0), (0, LANES - B_DH), (0, 0)))
    w1 = w1.reshape(2, B_CMP_LEN * LANES, B_CMP_HID).astype(MXU_DT)
    w1a, w1b = w1[:, :half], w1[:, half:]
    w2 = _pad_cols(cmp_w2, LANES).astype(MXU_DT)
    pos = jnp.pad(cmp_pos, ((0, 0), (0, 0), (0, LANES - B_DH))).reshape(2, 1, B_CMP_LEN * LANES)
    pos = jnp.broadcast_to(pos, (2, SUBLANES, B_CMP_LEN * LANES))
    return pl.pallas_call(
        _nsa_cmp_kernel,
        grid=(B,),
        in_specs=[pl.BlockSpec((1, 2 * B_KV, T, LANES), lambda b: (b, 0, 0, 0)),
                  _const_spec(w1a.shape), _const_spec(w1b.shape), _const_spec(w2.shape), _const_spec(pos.shape)],
        out_specs=pl.BlockSpec((1, 2 * B_KV, nblk, LANES), lambda b: (b, 0, 0, 0)),
        out_shape=jax.ShapeDtypeStruct((B, 2 * B_KV, nblk, LANES), jnp.float32),
        compiler_params=_cparams("parallel"),
    )(bc, w1a, w1b, w2, pos)


def _nsa_kernel(q_ref, kv_ref, cmp_ref, sm_ref, gb_ref, ovt_ref, exp_ref, out_ref, acc_scr, ocmp_scr,
                *, QB, KC, WL, NSB, NSEL):
    f32 = jnp.float32
    i = pl.program_id(1)
    s0 = i * QB
    R = B_HPG * QB
    tq_col = s0 + _iota((QB, 1), 0)
    tq_row = s0 + _iota((1, QB), 1)
    nch = (s0 + QB + KC - 1) // KC
    gates = jax.nn.sigmoid(sm_ref[0] + gb_ref[...])
    ncmp = cmp_ref.shape[2]
    cmp_end = _iota((1, ncmp), 1) * B_CMP_STRIDE + (B_CMP_LEN - 1)
    cmp_bias = jnp.where(cmp_end <= tq_col, 0.0, NEG)[None]
    jb = _iota((NSB, 1), 0)
    cur = lax.shift_right_logical(tq_row, int(np.log2(B_SEL_BLK)))
    forced = (jb == 0) | (jb == cur) | (jb == cur - 1)
    valid = jb * B_SEL_BLK <= tq_row
    wstart = pl.multiple_of(jnp.maximum(s0 + QB - WL, 0), QB)
    wpos = wstart + _iota((1, WL), 1)
    win_bias = jnp.where((wpos <= tq_col) & (wpos > tq_col - B_WIN), 0.0, NEG)[None]

    sel_b = []
    for g in range(B_KV):
        qs = q_ref[0, g * B_HPG:(g + 1) * B_HPG].reshape(R, LANES)
        kcm = cmp_ref[0, g].astype(MXU_DT)
        vcm = cmp_ref[0, B_KV + g].astype(MXU_DT)
        p_cmp = _masked_softmax2(_dot_nt(qs, kcm).reshape(B_HPG, QB, ncmp), cmp_bias)
        ocmp_scr[g * R:(g + 1) * R] = _dot(p_cmp.reshape(R, ncmp).astype(MXU_DT), vcm)
        psum = jnp.sum(p_cmp, axis=0)
        imp_t = lax.dot_general(ovt_ref[...], psum, (((1,), (1,)), ((), ())),
                                preferred_element_type=f32, precision=lax.Precision.HIGHEST)
        sc = jnp.where(forced, 1e6, imp_t)
        sc = jnp.where(valid, sc, NEG)
        rank = jnp.zeros((NSB, QB), f32)
        for j in range(NSB):
            cj = jnp.broadcast_to(sc[j:j + 1, :], (NSB, QB))
            ahead = (cj > sc) | ((cj == sc) & (jb > j))
            rank = rank + jnp.where(ahead, 1.0, 0.0)
        taken = jnp.where(rank < NSEL, 1.0, 0.0)
        if NSB < LANES:
            taken = jnp.concatenate([taken, jnp.zeros((LANES - NSB, QB), f32)], axis=0)
        sel_b.append(taken.T.astype(MXU_DT))

    acc_scr[...] = jnp.zeros(acc_scr.shape, f32)

    def sel_body(c, carry):
        ms, ls = list(carry[0]), list(carry[1])
        ks = pl.multiple_of(c * KC, KC)
        causal = ks + _iota((1, KC), 1) <= tq_col
        for g in range(B_KV):
            qs = q_ref[0, g * B_HPG:(g + 1) * B_HPG].reshape(R, LANES)
            kc_ = kv_ref[0, 0 * B_KV + g, pl.ds(ks, KC), :]
            vc_ = kv_ref[0, 1 * B_KV + g, pl.ds(ks, KC), :]
            mk = (_dot(sel_b[g], exp_ref[c]) > 0.5) & causal
            s3 = _dot_nt(qs, kc_).reshape(B_HPG, QB, KC) + jnp.where(mk, 0.0, NEG)[None]
            m_new = jnp.maximum(ms[g], jnp.max(s3, axis=-1, keepdims=True))
            p = jnp.exp2(s3 - m_new)
            alpha = jnp.exp2(ms[g] - m_new)
            ls[g] = alpha * ls[g] + jnp.sum(p, axis=-1, keepdims=True)
            ms[g] = m_new
            rows = slice(g * R, (g + 1) * R)
            acc_scr[rows] = alpha.reshape(R, 1) * acc_scr[rows] + _dot(p.reshape(R, KC).astype(MXU_DT), vc_)
        return tuple(ms), tuple(ls)

    m0 = tuple(jnp.full((B_HPG, QB, 1), SOFTMAX_M0, f32) for _ in range(B_KV))
    l0 = tuple(jnp.zeros((B_HPG, QB, 1), f32) for _ in range(B_KV))
    _, l_fin = lax.fori_loop(0, nch, sel_body, (m0, l0))

    for g in range(B_KV):
        rows = slice(g * R, (g + 1) * R)
        qs = q_ref[0, g * B_HPG:(g + 1) * B_HPG].reshape(R, LANES)
        o_sel = acc_scr[rows] / l_fin[g].reshape(R, 1)
        o_cmp = ocmp_scr[rows]
        kw_ = kv_ref[0, 2 * B_KV + g, pl.ds(wstart, WL), :]
        vw_ = kv_ref[0, 3 * B_KV + g, pl.ds(wstart, WL), :]
        p_win = _masked_softmax2(_dot_nt(qs, kw_).reshape(B_HPG, QB, WL), win_bias)
        o_win = _dot(p_win.reshape(R, WL).astype(MXU_DT), vw_)
        for hd in range(B_HPG):
            c0 = 2 * A_HEADS + (g * B_HPG + hd) * 3
            rs = slice(hd * QB, (hd + 1) * QB)
            o = (gates[:, c0:c0 + 1] * o_cmp[rs] + gates[:, c0 + 1:c0 + 2] * o_sel[rs]
                 + gates[:, c0 + 2:c0 + 3] * o_win[rs])
            col = (g * B_HPG + hd) * LANES
            out_ref[0, :, col:col + LANES] = o.astype(out_ref.dtype)


def _nsa(bq, bs, cmp, small, g_b):
    B, _, T, _ = bq.shape
    QB = min(Q_BLOCK, T)
    KC = min(KEY_CHUNK, T)
    WL = min(B_WIN + QB, T)
    NSB = T // B_SEL_BLK
    NSEL = min(B_SEL_N, NSB)
    ncmp = cmp.shape[2]
    f32 = jnp.float32
    M = (T - B_CMP_LEN) // B_CMP_STRIDE + 1
    assert NSB % SUBLANES == 0 and NSB <= LANES
    mi, jj = np.arange(ncmp)[None, :], np.arange(NSB)[:, None]
    ovt = ((mi * B_CMP_STRIDE < (jj + 1) * B_SEL_BLK) & (mi * B_CMP_STRIDE + B_CMP_LEN > jj * B_SEL_BLK)
           & (mi < M)).astype(np.float32)
    kk = np.arange(T)
    expand = (kk[None, :] // B_SEL_BLK == np.arange(LANES)[:, None]).astype(np.float32)
    expand = jnp.asarray(expand.reshape(LANES, T // KC, KC).transpose(1, 0, 2), MXU_DT)
    gb = jnp.zeros((1, LANES), f32).at[0, 2 * A_HEADS:2 * A_HEADS + 3 * B_HEADS].set(g_b)
    kern = functools.partial(_nsa_kernel, QB=QB, KC=KC, WL=WL, NSB=NSB, NSEL=NSEL)
    return pl.pallas_call(
        kern,
        grid=(B, T // QB),
        in_specs=[pl.BlockSpec((1, B_HEADS, QB, LANES), lambda b, i: (b, 0, i, 0)),
                  pl.BlockSpec((1, 4 * B_KV, T, LANES), lambda b, i: (b, 0, 0, 0)),
                  pl.BlockSpec((1, 2 * B_KV, ncmp, LANES), lambda b, i: (b, 0, 0, 0)),
                  pl.BlockSpec((1, QB, LANES), lambda b, i: (b, i, 0)),
                  _const_spec((1, LANES)), _const_spec(ovt.shape), _const_spec(expand.shape)],
        out_specs=pl.BlockSpec((1, QB, B_HEADS * LANES), lambda b, i: (b, i, 0)),
        out_shape=jax.ShapeDtypeStruct((B, T, B_HEADS * LANES), MXU_DT),
        scratch_shapes=[pltpu.VMEM((B_HEADS * QB, LANES), f32), pltpu.VMEM((B_HEADS * QB, LANES), f32)],
        compiler_params=_cparams("parallel", "arbitrary"),
    )(bq, bs, cmp, small, gb, jnp.asarray(ovt), expand)


def _rope_rows(d):
    inv = ROPE_BASE ** (-jnp.arange(0, d, 2, dtype=jnp.float32) / d)
    z = jnp.zeros((C_DR - d,), jnp.float32)
    inv64 = jnp.concatenate([inv, inv, z])
    sgn64 = jnp.concatenate([-jnp.ones(d // 2), jnp.ones(d // 2), z]).astype(jnp.float32)
    return inv64, sgn64


def _odd_prep_kernel(h_ref, pos_ref, win_ref, wqb_ref, wiq_ref, wuk_ref, qn_ref, kvn_ref, ikg_ref, ikb_ref,
                     rope_ref, perm_ref, qa_ref, kv_ref, qi_ref, ki_ref, wi_ref):
    f32 = jnp.float32
    z = _dot(h_ref[0].astype(MXU_DT), win_ref[...])
    pos = pos_ref[0].astype(f32)
    rr = rope_ref[...]
    ang_q = pos * rr[0:1, :]
    cos_q, sin_q = jnp.cos(ang_q), jnp.sin(ang_q) * rr[1:2, :]
    ang_i = pos * rr[2:3, :]
    cos_i, sin_i = jnp.cos(ang_i), jnp.sin(ang_i) * rr[3:4, :]

    def rms(x, g):
        return x * lax.rsqrt(jnp.mean(jnp.square(x), axis=-1, keepdims=True) + LN_EPS) * g

    cq = rms(z[:, 0:C_QL], qn_ref[...])
    ckv = rms(z[:, C_QL:C_QL + C_KVL], kvn_ref[...])
    o = C_QL + C_KVL
    k_rope = z[:, o:o + LANES] * cos_q + z[:, o + LANES:o + 2 * LANES] * sin_q
    kv_ref[0, :, 0:C_KVL] = ckv.astype(kv_ref.dtype)
    kv_ref[0, :, C_KVL:C_KPAD] = k_rope.astype(kv_ref.dtype)
    ik = z[:, o + 2 * LANES:o + 3 * LANES]
    real = _iota((1, LANES), 1) < C_IDX_DH
    mu = jnp.sum(ik, axis=-1, keepdims=True) / C_IDX_DH
    dlt = jnp.where(real, ik - mu, 0.0)
    var = jnp.sum(jnp.square(dlt), axis=-1, keepdims=True) / C_IDX_DH
    ki = dlt * lax.rsqrt(var + LN_EPS) * ikg_ref[...] + ikb_ref[...]
    ki = ki * cos_i + _dot_f32(ki, perm_ref[...]) * sin_i
    ki_ref[0] = ki.astype(ki_ref.dtype)
    wi_ref[0] = z[:, o + 3 * LANES:o + 4 * LANES] * (C_IDX_HEADS ** -0.5 * C_IDX_DH ** -0.5)
    cqb = cq.astype(MXU_DT)
    qf = _dot(cqb, wqb_ref[...])
    qi = _dot(cqb, wiq_ref[...])
    scale = (C_DN + C_DR) ** -0.5 * LOG2E
    nh = C_HEADS * LANES
    for hd in range(C_HEADS):
        q_abs = _dot(qf[:, hd * C_DN:(hd + 1) * C_DN].astype(MXU_DT), wuk_ref[hd])
        cs = slice(hd * LANES, (hd + 1) * LANES)
        q_rope = qf[:, nh:2 * nh][:, cs] * cos_q + qf[:, 2 * nh:3 * nh][:, cs] * sin_q
        qa_ref[0, hd, :, 0:C_KVL] = (q_abs * scale).astype(qa_ref.dtype)
        qa_ref[0, hd, :, C_KVL:C_KPAD] = (q_rope * scale).astype(qa_ref.dtype)
        qi_h = qi[:, 0:nh][:, cs] * cos_i + qi[:, nh:2 * nh][:, cs] * sin_i
        qi_ref[0, hd] = qi_h.astype(qi_ref.dtype)


def _rot_cols(w, n_heads, dh, d):
    w = w.reshape(w.shape[0], n_heads, dh)
    h = d // 2
    return jnp.concatenate([w[..., h:d], w[..., 0:h], jnp.zeros_like(w[..., d:])], -1).reshape(w.shape[0], n_heads * dh)


def _odd_prep(h3, pos3, w_in, q_norm, kv_norm, w_qb, w_uk, w_iq, ik_g, ik_b, tm):
    B, T, D = h3.shape
    f32 = jnp.float32
    o = 0
    parts = []
    for s in (C_QL, C_KVL, C_DR, C_IDX_DH, C_IDX_HEADS):
        parts.append(w_in[:, o:o + s]); o += s
    w_cq, w_ckv, w_kr, w_ik, w_iw = parts
    pc = lambda w: _pad_cols(w, LANES)
    win = jnp.concatenate([w_cq, w_ckv, pc(w_kr), pc(_rot_cols(w_kr, 1, C_DR, C_DR)), pc(w_ik), pc(w_iw)],
                          -1).astype(MXU_DT)
    wq = w_qb.reshape(C_QL, C_HEADS, C_DN + C_DR)
    w_nope = wq[..., :C_DN].reshape(C_QL, C_HEADS * C_DN)
    w_rope = wq[..., C_DN:].reshape(C_QL, C_HEADS * C_DR)
    wqb = jnp.concatenate([w_nope, _pad_heads(w_rope, C_HEADS, C_DR),
                           _pad_heads(_rot_cols(w_rope, C_HEADS, C_DR, C_DR), C_HEADS, C_DR)], -1).astype(MXU_DT)
    wiq = jnp.concatenate([_pad_heads(w_iq, C_IDX_HEADS, C_IDX_DH),
                           _pad_heads(_rot_cols(w_iq, C_IDX_HEADS, C_IDX_DH, C_IDX_DR), C_IDX_HEADS, C_IDX_DH)],
                          -1).astype(MXU_DT)
    wuk = w_uk.transpose(1, 2, 0).astype(MXU_DT)
    inv_q, sgn_q = _rope_rows(C_DR)
    inv_i, sgn_i = _rope_rows(C_IDX_DR)
    rope = jnp.stack([jnp.tile(v, LANES // C_DR) for v in (inv_q, sgn_q, inv_i, sgn_i)])
    rope = jnp.concatenate([rope, jnp.zeros((SUBLANES - 4, LANES), f32)])
    hh = C_IDX_DR // 2
    src = np.arange(LANES)
    src[:hh] += hh
    src[hh:C_IDX_DR] -= hh
    perm = np.zeros((LANES, LANES), np.float32)
    perm[src, np.arange(LANES)] = 1.0
    ikg = _pad_cols(ik_g[None, :], LANES)
    ikb = _pad_cols(ik_b[None, :], LANES)
    out_shape = (
        jax.ShapeDtypeStruct((B, C_HEADS, T, C_KPAD), MXU_DT),
        jax.ShapeDtypeStruct((B, T, C_KPAD), MXU_DT),
        jax.ShapeDtypeStruct((B, C_IDX_HEADS, T, LANES), MXU_DT),
        jax.ShapeDtypeStruct((B, T, LANES), MXU_DT),
        jax.ShapeDtypeStruct((B, T, LANES), f32),
    )
    row = lambda w: pl.BlockSpec((1, tm, w), lambda b, i: (b, i, 0))
    hm = lambda w: pl.BlockSpec((1, C_HEADS, tm, w), lambda b, i: (b, 0, i, 0))
    return pl.pallas_call(
        _odd_prep_kernel,
        grid=(B, T // tm),
        in_specs=[row(D), row(1), _const_spec(win.shape), _const_spec(wqb.shape), _const_spec(wiq.shape),
                  _const_spec(wuk.shape), _const_spec((1, C_QL)), _const_spec((1, C_KVL)),
                  _const_spec((1, LANES)), _const_spec((1, LANES)), _const_spec(rope.shape),
                  _const_spec(perm.shape)],
        out_specs=(hm(C_KPAD), row(C_KPAD), hm(LANES), row(LANES), row(LANES)),
        out_shape=out_shape,
        compiler_params=_cparams("parallel", "parallel"),
    )(h3, pos3, win, wqb, wiq, wuk, q_norm[None, :], kv_norm[None, :], ikg, ikb, rope, jnp.asarray(perm))


def _dsa_kernel(qa_ref, qi_ref, wi_ref, kv_ref, ki_ref, wuv_ref, eye_ref, tri_ref, out_ref,
                key_scr, hi_scr, lo_scr, acc_scr, *, QB, KC, SUB, TOPK, HG):
    f32, i32, i16 = jnp.float32, jnp.int32, jnp.int16
    i = pl.program_id(1)
    s0 = i * QB
    H = C_HEADS
    NG = H // HG
    RG = HG * QB
    nch = (s0 + QB + KC - 1) // KC
    PK = PACKED_ROWS
    tq_row = s0 + _iota((1, QB), 1)
    w_t = wi_ref[0].T
    one, zero = jnp.ones((), MXU_DT), jnp.zeros((), MXU_DT)

    def idx_body(c, _):
        ks = pl.multiple_of(c * KC, KC)
        kic = ki_ref[0, pl.ds(ks, KC), :]
        isc = None
        for hd in range(C_IDX_HEADS):
            s = jnp.maximum(_dot_nt(kic, qi_ref[0, hd]), 0.0) * w_t[hd:hd + 1, :]
            isc = s if isc is None else isc + s
        isc = jnp.where(isc == 0.0, 0.0, isc)
        kpos = ks + _iota((KC, 1), 0)
        isc = jnp.where(kpos <= tq_row, isc, NEG)
        bits = lax.bitcast_convert_type(isc, i32)
        key = jnp.where(bits < 0, bits ^ jnp.int32(0x7FFFFFFF), bits)
        key_scr[c] = key
        k3 = key.reshape(KC // PK, PK, QB)
        hi_scr[c] = lax.shift_right_arithmetic(k3, 16).astype(i16)
        lo_scr[c] = ((k3 & 0xFFFF) - 32768).astype(i16)
        return 0

    lax.fori_loop(0, nch, idx_body, 0)

    def rep16(v):
        return jnp.broadcast_to(v, (PK, QB)).astype(i16)[None]

    def count16(pred):
        def body(c, acc):
            x = jnp.where(pred(hi_scr[c], lo_scr[c]), one, zero)
            for r in range(KC // PK):
                acc = acc + x[r]
            return acc
        acc = lax.fori_loop(0, nch, body, jnp.zeros((PK, QB), MXU_DT))
        return jnp.sum(acc.astype(f32), axis=0, keepdims=True)

    def bisect(pick, base):
        def body(b, t):
            cand = t + lax.shift_left(jnp.int32(1), 15 - b)
            c16 = rep16(cand)
            return jnp.where(base + count16(lambda h, l: pick(h, l) >= c16) >= TOPK, cand, t)
        return lax.fori_loop(0, 16, body, jnp.full((1, QB), -32768, i32))

    thi = bisect(lambda h, l: h, 0.0)
    thi16 = rep16(thi)

    def bucket_body(c, _):
        lo_scr[c] = jnp.where(hi_scr[c] == thi16, lo_scr[c], jnp.full((), -32768, i16))
        return 0

    lax.fori_loop(0, nch, bucket_body, 0)
    n_hi = count16(lambda h, l: h > thi16)
    tlo = bisect(lambda h, l: l, n_hi)
    tlo16 = rep16(tlo)
    thr = thi * 65536 + (tlo + 32768)
    room = TOPK - (n_hi + count16(lambda h, l: l > tlo16))
    n_eq = count16(lambda h, l: (h == thi16) & (l == tlo16))

    @pl.when(jnp.max(jnp.where(n_eq > room, 1, 0)) > 0)
    def _():
        def body(c, before):
            key = key_scr[c]
            eq = key == thr
            seen = before + _dot(tri_ref[...], jnp.where(eq, 1.0, 0.0).astype(MXU_DT))
            key_scr[c] = jnp.where(eq & (seen > room), key - 1, key)
            return seen[KC - 1:KC, :]
        lax.fori_loop(0, nch, body, jnp.zeros((1, QB), f32))

    acc_scr[...] = jnp.zeros(acc_scr.shape, f32)

    def att_body(c, carry):
        ms, ls = list(carry[0]), list(carry[1])
        ks = pl.multiple_of(c * KC, KC)
        key = key_scr[c]
        kpos = ks + _iota((KC, 1), 0)
        sel_t = (key >= thr) & (kpos <= tq_row)
        keep = _dot_nt(eye_ref[...], jnp.where(sel_t, 1.0, 0.0).astype(MXU_DT))
        bias = jnp.where(keep > 0.5, 0.0, NEG)
        for u in range(KC // SUB):
            kvc = kv_ref[0, pl.ds(pl.multiple_of(ks + u * SUB, SUB), SUB), :]
            b_u = bias[:, u * SUB:(u + 1) * SUB][None]
            for g in range(NG):
                qg = qa_ref[0, g * HG:(g + 1) * HG].reshape(RG, C_KPAD)
                s = _dot_nt(qg, kvc).reshape(HG, QB, SUB) + b_u
                m_new = jnp.maximum(ms[g], jnp.max(s, axis=-1, keepdims=True))
                p = jnp.exp2(s - m_new)
                alpha = jnp.exp2(ms[g] - m_new)
                ls[g] = alpha * ls[g] + jnp.sum(p, axis=-1, keepdims=True)
                ms[g] = m_new
                rows = slice(g * RG, (g + 1) * RG)
                pv = _dot(p.reshape(RG, SUB).astype(MXU_DT), kvc[:, 0:C_KVL])
                acc_scr[rows] = alpha.reshape(RG, 1) * acc_scr[rows] + pv
        return tuple(ms), tuple(ls)

    m0 = tuple(jnp.full((HG, QB, 1), SOFTMAX_M0, f32) for _ in range(NG))
    l0 = tuple(jnp.zeros((HG, QB, 1), f32) for _ in range(NG))
    _, l_fin = lax.fori_loop(0, nch, att_body, (m0, l0))
    for g in range(NG):
        o_lat = (acc_scr[g * RG:(g + 1) * RG] / l_fin[g].reshape(RG, 1)).astype(MXU_DT)
        for k in range(HG):
            hd = g * HG + k
            out_ref[0, :, hd * C_DV:(hd + 1) * C_DV] = _dot(o_lat[k * QB:(k + 1) * QB], wuv_ref[hd]).astype(out_ref.dtype)


def _dsa(qa, kv, qi, ki, wi, w_uv):
    B, H, T, _ = qa.shape
    QB = min(DSA_Q_BLOCK, T)
    KC = min(KEY_CHUNK, T)
    topk = min(C_TOPK, T // 4)
    wuv = w_uv.transpose(1, 0, 2).astype(MXU_DT)
    assert T // PACKED_ROWS <= 256
    eye = jnp.eye(QB, dtype=MXU_DT)
    tri = jnp.tril(jnp.ones((KC, KC), MXU_DT))
    kern = functools.partial(_dsa_kernel, QB=QB, KC=KC, SUB=min(KEY_SUB, KC), TOPK=topk, HG=DSA_HEAD_GROUP)
    half_words = pltpu.VMEM((T // KC, KC // PACKED_ROWS, PACKED_ROWS, QB), jnp.int16)
    return pl.pallas_call(
        kern,
        grid=(B, T // QB),
        in_specs=[pl.BlockSpec((1, H, QB, C_KPAD), lambda b, i: (b, 0, i, 0)),
                  pl.BlockSpec((1, H, QB, LANES), lambda b, i: (b, 0, i, 0)),
                  pl.BlockSpec((1, QB, LANES), lambda b, i: (b, i, 0)),
                  pl.BlockSpec((1, T, C_KPAD), lambda b, i: (b, 0, 0)),
                  pl.BlockSpec((1, T, LANES), lambda b, i: (b, 0, 0)),
                  _const_spec(wuv.shape), _const_spec(eye.shape), _const_spec(tri.shape)],
        out_specs=pl.BlockSpec((1, QB, H * C_DV), lambda b, i: (b, i, 0)),
        out_shape=jax.ShapeDtypeStruct((B, T, H * C_DV), MXU_DT),
        scratch_shapes=[pltpu.VMEM((T // KC, KC, QB), jnp.int32),
                        half_words, half_words,
                        pltpu.VMEM((H * QB, C_KVL), jnp.float32)],
        compiler_params=_cparams("parallel", "arbitrary"),
    )(qa, qi, wi, kv, ki, wuv, eye, tri)


def _post_kernel(*refs, n_mix, n_ff):
    h_ref = refs[0]
    mix = refs[1:1 + 2 * n_mix]
    g1, b1, w1_ref, w2_ref, g2, b2, wg_ref, p_ref, wp_ref, out_ref = refs[1 + 2 * n_mix:]
    h = h_ref[...]
    y = _dot(mix[0][...].astype(MXU_DT), mix[1][...])
    for k in range(1, n_mix):
        y = y + _dot(mix[2 * k][...].astype(MXU_DT), mix[2 * k + 1][...])
    h1 = _layer_norm(DN_ALPHA * h + y, g1[...], b1[...])
    h1b = h1.astype(MXU_DT)
    ff = D_FF // n_ff
    u = None
    for k in range(n_ff):
        a = jnp.square(jnp.maximum(_dot(h1b, w1_ref[:, k * ff:(k + 1) * ff]), 0.0))
        t = _dot(a.astype(MXU_DT), w2_ref[k * ff:(k + 1) * ff, :])
        u = t if u is None else u + t
    h2 = _layer_norm(DN_ALPHA * h1 + u, g2[...], b2[...])
    gate = jax.nn.sigmoid(_dot(h2.astype(MXU_DT), wg_ref[...]))
    out_ref[...] = h2 + gate * _dot(p_ref[...].astype(MXU_DT), wp_ref[...])


def _post(h2d, mixes, ln1_g, ln1_b, w1, w2, ln2_g, ln2_b, wg, p2d, wp, tm):
    M, D = h2d.shape
    row = lambda w: pl.BlockSpec((tm, w), lambda i: (i, 0))
    vec = lambda v: v[None, :]
    in_specs = [row(D)]
    args = [h2d]
    for x, w in mixes:
        in_specs += [row(x.shape[1]), _const_spec(w.shape)]
        args += [x, w]
    in_specs += [_const_spec((1, D)), _const_spec((1, D)), _const_spec(w1.shape), _const_spec(w2.shape),
                 _const_spec((1, D)), _const_spec((1, D)), _const_spec(wg.shape), row(D_PLE), _const_spec(wp.shape)]
    args += [vec(ln1_g), vec(ln1_b), w1, w2, vec(ln2_g), vec(ln2_b), wg, p2d, wp]
    kern = functools.partial(_post_kernel, n_mix=len(mixes), n_ff=4)
    return pl.pallas_call(
        kern,
        grid=(M // tm,),
        in_specs=in_specs,
        out_specs=row(D),
        out_shape=jax.ShapeDtypeStruct((M, D), jnp.float32),
        compiler_params=_cparams("parallel"),
    )(*args)


def kernel(x, p, positions, e_w_in, e_a_conv, e_a_i_b, e_a_f_b, e_a_norm, e_b_cmp_pos, e_b_cmp_w1, e_b_cmp_w2, e_b_g_b, e_w_out, o_w_in, o_q_norm, o_kv_norm, o_w_qb, o_w_uk, o_w_uv, o_w_iq, o_ik_g, o_ik_b, o_w_out, ln1_g, ln1_b, ln2_g, ln2_b, mlp_w1, mlp_w2, ple_gate_w, ple_w):
    B, T, D = x.shape
    M = B * T
    tm = min(ROW_TILE, T)
    h = x
    pos3 = positions[..., None]
    bf = lambda w: w.astype(MXU_DT)
    for i in range(DEPTH):
        j = i // 2
        if i % 2 == 0:
            qk, av, ao, small, bq, bc, bs = _even_proj(h, _even_w_in_aug(e_w_in[j]), tm)
            ya = _mlstm(qk, av, ao, small, e_a_conv[j], e_a_i_b[j], e_a_f_b[j], e_a_norm[j])
            cmp = _nsa_compress(bc, e_b_cmp_pos[j], e_b_cmp_w1[j], e_b_cmp_w2[j])
            yb = _nsa(bq, bs, cmp, small, e_b_g_b[j])
            w_out = e_w_out[j]
            mixes = [(ya.reshape(M, A_W), bf(w_out[:A_W])),
                     (yb.reshape(M, B_HEADS * LANES), bf(_pad_heads(w_out[A_W:], B_HEADS, B_DH, axis=0)))]
        else:
            qa, kv, qi, ki, wi = _odd_prep(h, pos3, o_w_in[j], o_q_norm[j], o_kv_norm[j], o_w_qb[j], o_w_uk[j],
                                           o_w_iq[j], o_ik_g[j], o_ik_b[j], tm)
            o = _dsa(qa, kv, qi, ki, wi, o_w_uv[j])
            mixes = [(o.reshape(M, C_HEADS * C_DV), bf(o_w_out[j]))]
        h = _post(h.reshape(M, D), mixes, ln1_g[i], ln1_b[i], bf(mlp_w1[i]), bf(mlp_w2[i]), ln2_g[i], ln2_b[i],
                  bf(ple_gate_w[i]), p[i].reshape(M, D_PLE), bf(ple_w[i]), min(MLP_ROW_TILE, T)).reshape(B, T, D)
    return h
```

```python
import functools

import numpy as np
import jax
import jax.numpy as jnp
from jax import lax
from jax.experimental import pallas as pl
from jax.experimental.pallas import tpu as pltpu

D_MODEL = 1024
DEPTH = 4
D_PLE = 256
D_FF = 4 * D_MODEL
DN_ALPHA = (2.0 * DEPTH) ** 0.25
LN_EPS = 1e-5
NEG = -1e30

A_HEADS = 4
A_DH = D_MODEL // 8
A_W = A_HEADS * A_DH
A_CONV = 4

B_HEADS = 8
B_DH = 64
B_KV = 2
B_HPG = B_HEADS // B_KV
B_CMP_LEN = 32
B_CMP_STRIDE = 16
B_CMP_HID = 128
B_SEL_BLK = 64
B_SEL_N = 16
B_WIN = 512

C_HEADS = 8
C_DN = 128
C_DR = 64
C_DV = 128
C_QL = 512
C_KVL = 256
C_IDX_HEADS = 8
C_IDX_DH = 64
C_IDX_DR = 32
C_TOPK = 256
ROPE_BASE = 10000.0

LANES = 128
SUBLANES = 8
PACKED_ROWS = 16
VMEM_LIMIT_BYTES = 56 * 2**20
MXU_DT = jnp.bfloat16
INT_MIN = -2**31

MLSTM_CHUNK = 256
MLSTM_SEQS = 1
ROW_TILE = 256
MLP_ROW_TILE = 512
Q_BLOCK = 256
DSA_Q_BLOCK = 256
DSA_HEAD_GROUP = 4
KEY_CHUNK = 512
KEY_SUB = 512
C_KPAD = 384

LOG2E = 1.4426950408889634
SOFTMAX_M0 = 0.5 * NEG
SOFTMAX_TINY = 1e-30


def _cparams(*sem):
    return pltpu.CompilerParams(dimension_semantics=sem, vmem_limit_bytes=VMEM_LIMIT_BYTES)


def _const_spec(shape):
    nd = len(shape)
    return pl.BlockSpec(shape, lambda *_: (0,) * nd, pipeline_mode=pl.Buffered(1))


def _dot(a, b):
    return jnp.dot(a, b, preferred_element_type=jnp.float32)


def _dot_nt(a, b):
    return lax.dot_general(a, b, (((1,), (1,)), ((), ())), preferred_element_type=jnp.float32)


def _dot_tn(a, b):
    return lax.dot_general(a, b, (((0,), (0,)), ((), ())), preferred_element_type=jnp.float32)


def _dot_f32(a, b):
    return jnp.dot(a, b, preferred_element_type=jnp.float32, precision=lax.Precision.HIGHEST)


def _layer_norm(x, g, b):
    mu = jnp.mean(x, axis=-1, keepdims=True)
    var = jnp.mean(jnp.square(x - mu), axis=-1, keepdims=True)
    return (x - mu) * lax.rsqrt(var + LN_EPS) * g + b


def _masked_softmax2(s, bias):
    s = s + bias
    m = jnp.maximum(jnp.max(s, axis=-1, keepdims=True), SOFTMAX_M0)
    e = jnp.exp2(s - m)
    return e / jnp.maximum(jnp.sum(e, axis=-1, keepdims=True), SOFTMAX_TINY)


def _iota(shape, dim):
    return lax.broadcasted_iota(jnp.int32, shape, dim)


def _pad_heads(w, n_heads, dh, axis=-1):
    axis = axis % w.ndim
    shp = w.shape[:axis] + (n_heads, dh) + w.shape[axis + 1:]
    w = w.reshape(shp)
    pad = [(0, 0)] * w.ndim
    pad[axis + 1] = (0, LANES - dh)
    w = jnp.pad(w, pad)
    return w.reshape(shp[:axis] + (n_heads * LANES,) + shp[axis + 2:])


def _pad_cols(w, width):
    return jnp.pad(w, [(0, 0)] * (w.ndim - 1) + [(0, width - w.shape[-1])])


def _even_proj_kernel(h_ref, w_ref, qk_ref, av_ref, ao_ref, sm_ref, bq_ref, bc_ref, bs_ref):
    z = _dot(h_ref[0].astype(MXU_DT), w_ref[...])
    o = 0
    qk_ref[0] = z[:, o:o + 2 * A_W]; o += 2 * A_W
    av_ref[0] = z[:, o:o + A_W].astype(av_ref.dtype); o += A_W
    ao_ref[0] = z[:, o:o + A_W]; o += A_W
    sm_ref[0] = z[:, o:o + LANES]; o += LANES
    for hd in range(B_HEADS):
        bq_ref[0, hd] = z[:, o:o + LANES].astype(bq_ref.dtype); o += LANES
    for j in range(2 * B_KV):
        bc_ref[0, j] = z[:, o:o + LANES]; o += LANES
    for j in range(4 * B_KV):
        bs_ref[0, j] = z[:, o:o + LANES].astype(bs_ref.dtype); o += LANES


def _even_proj(h3, w_aug, tm):
    B, T, D = h3.shape
    n = w_aug.shape[1]
    f32 = jnp.float32
    out_shape = (
        jax.ShapeDtypeStruct((B, T, 2 * A_W), f32),
        jax.ShapeDtypeStruct((B, T, A_W), MXU_DT),
        jax.ShapeDtypeStruct((B, T, A_W), f32),
        jax.ShapeDtypeStruct((B, T, LANES), f32),
        jax.ShapeDtypeStruct((B, B_HEADS, T, LANES), MXU_DT),
        jax.ShapeDtypeStruct((B, 2 * B_KV, T, LANES), f32),
        jax.ShapeDtypeStruct((B, 4 * B_KV, T, LANES), MXU_DT),
    )
    row = lambda w: pl.BlockSpec((1, tm, w), lambda b, i: (b, i, 0))
    hm = lambda nh: pl.BlockSpec((1, nh, tm, LANES), lambda b, i: (b, 0, i, 0))
    return pl.pallas_call(
        _even_proj_kernel,
        grid=(B, T // tm),
        in_specs=[row(D), _const_spec((D, n))],
        out_specs=(row(2 * A_W), row(A_W), row(A_W), row(LANES), hm(B_HEADS), hm(2 * B_KV), hm(4 * B_KV)),
        out_shape=out_shape,
        compiler_params=_cparams("parallel", "parallel"),
    )(h3, w_aug)


def _even_w_in_aug(w_in):
    sizes = (A_W, A_W, A_W, A_W, A_HEADS, A_HEADS, B_HEADS * B_DH) + (B_KV * B_DH,) * 6 + (3 * B_HEADS,)
    parts, o = [], 0
    for s in sizes:
        parts.append(w_in[:, o:o + s]); o += s
    aq, ak, av, ao, ai, af, bq, bkc, bvc, bks, bvs, bkw, bvw, bg = parts
    small = _pad_cols(jnp.concatenate([ai, af, bg], -1), LANES)
    ph = lambda w: _pad_heads(w, B_KV, B_DH)
    bq = bq * (B_DH ** -0.5 * LOG2E)
    cols = [aq, ak, av, ao, small, _pad_heads(bq, B_HEADS, B_DH),
            ph(bkc), ph(bvc), ph(bks), ph(bvs), ph(bkw), ph(bvw)]
    return jnp.concatenate(cols, -1).astype(MXU_DT)


def _mlstm_kernel(qk_ref, v_ref, o_ref, gc_ref, gr_ref, cw_ref, bc_ref, br_ref, ng_ref, tri_ref, out_ref,
                  xs_scr, ct_scr, n_scr, m_scr, *, L, NB):
    c = pl.program_id(1)
    f32 = jnp.float32
    W2 = 2 * A_W

    @pl.when(c == 0)
    def _():
        xs_scr[:, 0:SUBLANES, :] = jnp.zeros((NB, SUBLANES, W2), f32)
        ct_scr[...] = jnp.zeros(ct_scr.shape, f32)
        n_scr[...] = jnp.zeros(n_scr.shape, f32)
        m_scr[...] = jnp.zeros(m_scr.shape, f32)

    tri = tri_ref[...]
    causal = _iota((L, L), 1) <= _iota((L, L), 0)
    cw = cw_ref[...]
    for bb in range(NB):
        _mlstm_chunk(bb, qk_ref, v_ref, o_ref, gc_ref, gr_ref, cw, bc_ref, br_ref, ng_ref, tri, causal, out_ref,
                     xs_scr, ct_scr, n_scr, m_scr, L)


def _mlstm_chunk(bb, qk_ref, v_ref, o_ref, gc_ref, gr_ref, cw, bc_ref, br_ref, ng_ref, tri, causal, out_ref,
                 xs_scr, ct_scr, n_scr, m_scr, L):
    f32 = jnp.float32
    xs_scr[bb, SUBLANES:SUBLANES + L, :] = qk_ref[bb]
    base = SUBLANES - (A_CONV - 1)
    acc = xs_scr[bb, base:base + L, :] * cw[0:1, :]
    for j in range(1, A_CONV):
        acc = acc + xs_scr[bb, base + j:base + j + L, :] * cw[j:j + 1, :]
    qk = acc * jax.nn.sigmoid(acc)
    xs_scr[bb, 0:SUBLANES, :] = xs_scr[bb, L:L + SUBLANES, :]

    gc = gc_ref[bb] + bc_ref[...]
    gr = gr_ref[bb, 0] + br_ref[...]
    ig_c = gc[:, 0:A_HEADS]
    b_c = _dot_f32(tri, jax.nn.log_sigmoid(gc[:, A_HEADS:2 * A_HEADS]))
    ig_r = gr[0:A_HEADS, :]
    b_r = _dot_f32(jax.nn.log_sigmoid(gr[A_HEADS:2 * A_HEADS, :]), tri.T)

    for hd in range(A_HEADS):
        sl = slice(hd * A_DH, (hd + 1) * A_DH)
        q_h = (qk[:, sl] * A_DH ** -0.5).astype(MXU_DT)
        k_f = qk[:, A_W + hd * A_DH:A_W + (hd + 1) * A_DH]
        k_h = k_f.astype(MXU_DT)
        v_h = v_ref[bb][:, sl]
        bi = b_c[:, hd:hd + 1]
        ic = ig_c[:, hd:hd + 1]
        dmat = jnp.where(causal, bi - b_r[hd:hd + 1, :] + ig_r[hd:hd + 1, :], NEG)
        m_prev = m_scr[bb, hd][:, 0:1]
        m_inter = bi + m_prev
        m_t = jnp.maximum(m_inter, jnp.max(dmat, axis=1, keepdims=True))
        e_inter = jnp.exp(m_inter - m_t)
        s = _dot_nt(q_h, k_h) * jnp.exp(dmat - m_t)
        ct = ct_scr[bb, hd]
        nrow = n_scr[bb, hd]
        num = e_inter * _dot(q_h, ct.astype(MXU_DT)) + _dot(s.astype(MXU_DT), v_h)
        den = e_inter * jnp.sum(q_h.astype(f32) * nrow, axis=1, keepdims=True) + jnp.sum(s, axis=1, keepdims=True)
        hc = num / jnp.maximum(jnp.abs(den), jnp.exp(-m_t))
        b_last = bi[L - 1:L, :]
        dec = b_last - bi + ic
        m_new = jnp.maximum(b_last + m_prev, jnp.max(dec, axis=0, keepdims=True))
        wgt = jnp.exp(dec - m_new)
        e_st = jnp.exp(b_last + m_prev - m_new)
        kw = k_f * wgt
        ct_scr[bb, hd] = e_st * ct + _dot_tn(kw.astype(MXU_DT), v_h)
        n_scr[bb, hd] = e_st * nrow + jnp.sum(kw, axis=0, keepdims=True)
        m_scr[bb, hd] = jnp.broadcast_to(m_new, (1, LANES))
        mu = jnp.mean(hc, axis=1, keepdims=True)
        var = jnp.mean(jnp.square(hc - mu), axis=1, keepdims=True)
        hn = (hc - mu) * lax.rsqrt(var + LN_EPS) * ng_ref[:, sl]
        out_ref[bb, :, sl] = (hn * jax.nn.sigmoid(o_ref[bb][:, sl])).astype(out_ref.dtype)


def _mlstm(qk, av, ao, small, conv_w, i_b, f_b, norm_g):
    B, T, _ = qk.shape
    L = min(MLSTM_CHUNK, T)
    N = T // L
    f32 = jnp.float32
    gates = small[..., 0:2 * A_HEADS]
    gates_r = gates.reshape(B, N, L, 2 * A_HEADS).transpose(0, 1, 3, 2)
    bias = jnp.concatenate([i_b, f_b]).astype(f32)
    tri = jnp.tril(jnp.ones((L, L), f32))
    NB = MLSTM_SEQS if B % MLSTM_SEQS == 0 else 1
    row = lambda w: pl.BlockSpec((NB, L, w), lambda b, c: (b, c, 0))
    kern = functools.partial(_mlstm_kernel, L=L, NB=NB)
    return pl.pallas_call(
        kern,
        grid=(B // NB, N),
        in_specs=[row(2 * A_W), row(A_W), row(A_W), row(2 * A_HEADS),
                  pl.BlockSpec((NB, 1, 2 * A_HEADS, L), lambda b, c: (b, c, 0, 0)),
                  _const_spec((A_CONV, 2 * A_W)), _const_spec((1, 2 * A_HEADS)), _const_spec((2 * A_HEADS, 1)),
                  _const_spec((1, A_W)), _const_spec((L, L))],
        out_specs=row(A_W),
        out_shape=jax.ShapeDtypeStruct((B, T, A_W), MXU_DT),
        scratch_shapes=[pltpu.VMEM((NB, L + 2 * SUBLANES, 2 * A_W), f32),
                        pltpu.VMEM((NB, A_HEADS, A_DH, A_DH), f32),
                        pltpu.VMEM((NB, A_HEADS, 1, A_DH), f32),
                        pltpu.VMEM((NB, A_HEADS, 1, LANES), f32)],
        compiler_params=_cparams("parallel", "arbitrary"),
    )(qk, av, ao, gates, gates_r, conv_w, bias[None, :], bias[:, None], norm_g[None, :], tri)


def _nsa_cmp_kernel(x_ref, w1a_ref, w1b_ref, w2_ref, pos_ref, out_ref):
    n = x_ref.shape[2] // B_CMP_STRIDE
    half = B_CMP_STRIDE * LANES
    for j in range(2):
        bias = (_dot(pos_ref[j, :, 0:half].astype(MXU_DT), w1a_ref[j])
                + _dot(pos_ref[j, :, half:2 * half].astype(MXU_DT), w1b_ref[j]))[0:1, :]
        for g in range(B_KV):
            u = jnp.concatenate([x_ref[0, j * B_KV + g, pl.ds(r, n, stride=B_CMP_STRIDE), :]
                                 for r in range(B_CMP_STRIDE)], axis=1).astype(MXU_DT)
            a = _dot(u, w1a_ref[j])
            bm = _dot(u, w1b_ref[j])
            pre = a + pltpu.roll(bm, n - 1, 0) + bias
            hid = jax.nn.gelu(pre)
            out_ref[0, j * B_KV + g] = _dot(hid.astype(MXU_DT), w2_ref[j])


def _nsa_compress(bc, cmp_pos, cmp_w1, cmp_w2):
    B, _, T, _ = bc.shape
    nblk = T // B_CMP_STRIDE
    half = B_CMP_STRIDE * LANES
    w1 = jnp.pad(cmp_w1.reshape(2, B_CMP_LEN, B_DH, B_CMP_HID), ((0, 0), (0, 0), (0, LANES - B_DH), (0, 0)))
    w1 = w1.reshape(2, B_CMP_LEN * LANES, B_CMP_HID).astype(MXU_DT)
    w1a, w1b = w1[:, :half], w1[:, half:]
    w2 = _pad_cols(cmp_w2, LANES).astype(MXU_DT)
    pos = jnp.pad(cmp_pos, ((0, 0), (0, 0), (0, LANES - B_DH))).reshape(2, 1, B_CMP_LEN * LANES)
    pos = jnp.broadcast_to(pos, (2, SUBLANES, B_CMP_LEN * LANES))
    return pl.pallas_call(
        _nsa_cmp_kernel,
        grid=(B,),
        in_specs=[pl.BlockSpec((1, 2 * B_KV, T, LANES), lambda b: (b, 0, 0, 0)),
                  _const_spec(w1a.shape), _const_spec(w1b.shape), _const_spec(w2.shape), _const_spec(pos.shape)],
        out_specs=pl.BlockSpec((1, 2 * B_KV, nblk, LANES), lambda b: (b, 0, 0, 0)),
        out_shape=jax.ShapeDtypeStruct((B, 2 * B_KV, nblk, LANES), jnp.float32),
        compiler_params=_cparams("parallel"),
    )(bc, w1a, w1b, w2, pos)


def _nsa_kernel(q_ref, kv_ref, cmp_ref, sm_ref, gb_ref, ovt_ref, exp_ref, out_ref, acc_scr, ocmp_scr,
                *, QB, KC, WL, NSB, NSEL):
    f32 = jnp.float32
    i = pl.program_id(1)
    s0 = i * QB
    R = B_HPG * QB
    tq_col = s0 + _iota((QB, 1), 0)
    tq_row = s0 + _iota((1, QB), 1)
    nch = (s0 + QB + KC - 1) // KC
    gates = jax.nn.sigmoid(sm_ref[0] + gb_ref[...])
    ncmp = cmp_ref.shape[2]
    cmp_end = _iota((1, ncmp), 1) * B_CMP_STRIDE + (B_CMP_LEN - 1)
    cmp_bias = jnp.where(cmp_end <= tq_col, 0.0, NEG)[None]
    jb = _iota((NSB, 1), 0)
    cur = lax.shift_right_logical(tq_row, int(np.log2(B_SEL_BLK)))
    forced = (jb == 0) | (jb == cur) | (jb == cur - 1)
    valid = jb * B_SEL_BLK <= tq_row
    wstart = pl.multiple_of(jnp.maximum(s0 + QB - WL, 0), QB)
    wpos = wstart + _iota((1, WL), 1)
    win_bias = jnp.where((wpos <= tq_col) & (wpos > tq_col - B_WIN), 0.0, NEG)[None]

    sel_b = []
    for g in range(B_KV):
        qs = q_ref[0, g * B_HPG:(g + 1) * B_HPG].reshape(R, LANES)
        kcm = cmp_ref[0, g].astype(MXU_DT)
        vcm = cmp_ref[0, B_KV + g].astype(MXU_DT)
        p_cmp = _masked_softmax2(_dot_nt(qs, kcm).reshape(B_HPG, QB, ncmp), cmp_bias)
        ocmp_scr[g * R:(g + 1) * R] = _dot(p_cmp.reshape(R, ncmp).astype(MXU_DT), vcm)
        psum = jnp.sum(p_cmp, axis=0)
        imp_t = lax.dot_general(ovt_ref[...], psum, (((1,), (1,)), ((), ())),
                                preferred_element_type=f32, precision=lax.Precision.HIGHEST)
        sc = jnp.where(forced, 1e6, imp_t)
        sc = jnp.where(valid, sc, NEG)
        rank = jnp.zeros((NSB, QB), f32)
        for j in range(NSB):
            cj = jnp.broadcast_to(sc[j:j + 1, :], (NSB, QB))
            ahead = (cj > sc) | ((cj == sc) & (jb > j))
            rank = rank + jnp.where(ahead, 1.0, 0.0)
        taken = jnp.where(rank < NSEL, 1.0, 0.0)
        if NSB < LANES:
            taken = jnp.concatenate([taken, jnp.zeros((LANES - NSB, QB), f32)], axis=0)
        sel_b.append(taken.T.astype(MXU_DT))

    acc_scr[...] = jnp.zeros(acc_scr.shape, f32)

    def sel_body(c, carry):
        ms, ls = list(carry[0]), list(carry[1])
        ks = pl.multiple_of(c * KC, KC)
        causal = ks + _iota((1, KC), 1) <= tq_col
        for g in range(B_KV):
            qs = q_ref[0, g * B_HPG:(g + 1) * B_HPG].reshape(R, LANES)
            kc_ = kv_ref[0, 0 * B_KV + g, pl.ds(ks, KC), :]
            vc_ = kv_ref[0, 1 * B_KV + g, pl.ds(ks, KC), :]
            mk = (_dot(sel_b[g], exp_ref[c]) > 0.5) & causal
            s3 = _dot_nt(qs, kc_).reshape(B_HPG, QB, KC) + jnp.where(mk, 0.0, NEG)[None]
            m_new = jnp.maximum(ms[g], jnp.max(s3, axis=-1, keepdims=True))
            p = jnp.exp2(s3 - m_new)
            alpha = jnp.exp2(ms[g] - m_new)
            ls[g] = alpha * ls[g] + jnp.sum(p, axis=-1, keepdims=True)
            ms[g] = m_new
            rows = slice(g * R, (g + 1) * R)
            acc_scr[rows] = alpha.reshape(R, 1) * acc_scr[rows] + _dot(p.reshape(R, KC).astype(MXU_DT), vc_)
        return tuple(ms), tuple(ls)

    m0 = tuple(jnp.full((B_HPG, QB, 1), SOFTMAX_M0, f32) for _ in range(B_KV))
    l0 = tuple(jnp.zeros((B_HPG, QB, 1), f32) for _ in range(B_KV))
    carry = lax.fori_loop(0, nch // 2, lambda c2, cr: sel_body(2 * c2 + 1, sel_body(2 * c2, cr)), (m0, l0))
    _, l_fin = lax.cond(nch % 2 == 1, lambda cr: sel_body(nch - 1, cr), lambda cr: cr, carry)

    for g in range(B_KV):
        rows = slice(g * R, (g + 1) * R)
        qs = q_ref[0, g * B_HPG:(g + 1) * B_HPG].reshape(R, LANES)
        o_sel = acc_scr[rows] / l_fin[g].reshape(R, 1)
        o_cmp = ocmp_scr[rows]
        kw_ = kv_ref[0, 2 * B_KV + g, pl.ds(wstart, WL), :]
        vw_ = kv_ref[0, 3 * B_KV + g, pl.ds(wstart, WL), :]
        p_win = _masked_softmax2(_dot_nt(qs, kw_).reshape(B_HPG, QB, WL), win_bias)
        o_win = _dot(p_win.reshape(R, WL).astype(MXU_DT), vw_)
        for hd in range(B_HPG):
            c0 = 2 * A_HEADS + (g * B_HPG + hd) * 3
            rs = slice(hd * QB, (hd + 1) * QB)
            o = (gates[:, c0:c0 + 1] * o_cmp[rs] + gates[:, c0 + 1:c0 + 2] * o_sel[rs]
                 + gates[:, c0 + 2:c0 + 3] * o_win[rs])
            col = (g * B_HPG + hd) * LANES
            out_ref[0, :, col:col + LANES] = o.astype(out_ref.dtype)


def _nsa(bq, bs, cmp, small, g_b):
    B, _, T, _ = bq.shape
    QB = min(Q_BLOCK, T)
    KC = min(KEY_CHUNK, T)
    WL = min(B_WIN + QB, T)
    NSB = T // B_SEL_BLK
    NSEL = min(B_SEL_N, NSB)
    ncmp = cmp.shape[2]
    f32 = jnp.float32
    M = (T - B_CMP_LEN) // B_CMP_STRIDE + 1
    assert NSB % SUBLANES == 0 and NSB <= LANES
    mi, jj = np.arange(ncmp)[None, :], np.arange(NSB)[:, None]
    ovt = ((mi * B_CMP_STRIDE < (jj + 1) * B_SEL_BLK) & (mi * B_CMP_STRIDE + B_CMP_LEN > jj * B_SEL_BLK)
           & (mi < M)).astype(np.float32)
    kk = np.arange(T)
    expand = (kk[None, :] // B_SEL_BLK == np.arange(LANES)[:, None]).astype(np.float32)
    expand = jnp.asarray(expand.reshape(LANES, T // KC, KC).transpose(1, 0, 2), MXU_DT)
    gb = jnp.zeros((1, LANES), f32).at[0, 2 * A_HEADS:2 * A_HEADS + 3 * B_HEADS].set(g_b)
    kern = functools.partial(_nsa_kernel, QB=QB, KC=KC, WL=WL, NSB=NSB, NSEL=NSEL)
    return pl.pallas_call(
        kern,
        grid=(B, T // QB),
        in_specs=[pl.BlockSpec((1, B_HEADS, QB, LANES), lambda b, i: (b, 0, i, 0)),
                  pl.BlockSpec((1, 4 * B_KV, T, LANES), lambda b, i: (b, 0, 0, 0)),
                  pl.BlockSpec((1, 2 * B_KV, ncmp, LANES), lambda b, i: (b, 0, 0, 0)),
                  pl.BlockSpec((1, QB, LANES), lambda b, i: (b, i, 0)),
                  _const_spec((1, LANES)), _const_spec(ovt.shape), _const_spec(expand.shape)],
        out_specs=pl.BlockSpec((1, QB, B_HEADS * LANES), lambda b, i: (b, i, 0)),
        out_shape=jax.ShapeDtypeStruct((B, T, B_HEADS * LANES), MXU_DT),
        scratch_shapes=[pltpu.VMEM((B_HEADS * QB, LANES), f32), pltpu.VMEM((B_HEADS * QB, LANES), f32)],
        compiler_params=_cparams("parallel", "arbitrary"),
    )(bq, bs, cmp, small, gb, jnp.asarray(ovt), expand)


def _rope_rows(d):
    inv = ROPE_BASE ** (-jnp.arange(0, d, 2, dtype=jnp.float32) / d)
    z = jnp.zeros((C_DR - d,), jnp.float32)
    inv64 = jnp.concatenate([inv, inv, z])
    sgn64 = jnp.concatenate([-jnp.ones(d // 2), jnp.ones(d // 2), z]).astype(jnp.float32)
    return inv64, sgn64


def _odd_prep_kernel(h_ref, pos_ref, win_ref, wqb_ref, wiq_ref, wuk_ref, qn_ref, kvn_ref, ikg_ref, ikb_ref,
                     rope_ref, perm_ref, qa_ref, kv_ref, qi_ref, ki_ref, wi_ref):
    f32 = jnp.float32
    z = _dot(h_ref[0].astype(MXU_DT), win_ref[...])
    pos = pos_ref[0].astype(f32)
    rr = rope_ref[...]
    ang_q = pos * rr[0:1, :]
    cos_q, sin_q = jnp.cos(ang_q), jnp.sin(ang_q) * rr[1:2, :]
    ang_i = pos * rr[2:3, :]
    cos_i, sin_i = jnp.cos(ang_i), jnp.sin(ang_i) * rr[3:4, :]

    def rms(x, g):
        return x * lax.rsqrt(jnp.mean(jnp.square(x), axis=-1, keepdims=True) + LN_EPS) * g

    cq = rms(z[:, 0:C_QL], qn_ref[...])
    ckv = rms(z[:, C_QL:C_QL + C_KVL], kvn_ref[...])
    o = C_QL + C_KVL
    k_rope = z[:, o:o + LANES] * cos_q + z[:, o + LANES:o + 2 * LANES] * sin_q
    kv_ref[0, :, 0:C_KVL] = ckv.astype(kv_ref.dtype)
    kv_ref[0, :, C_KVL:C_KPAD] = k_rope.astype(kv_ref.dtype)
    ik = z[:, o + 2 * LANES:o + 3 * LANES]
    real = _iota((1, LANES), 1) < C_IDX_DH
    mu = jnp.sum(ik, axis=-1, keepdims=True) / C_IDX_DH
    dlt = jnp.where(real, ik - mu, 0.0)
    var = jnp.sum(jnp.square(dlt), axis=-1, keepdims=True) / C_IDX_DH
    ki = dlt * lax.rsqrt(var + LN_EPS) * ikg_ref[...] + ikb_ref[...]
    ki = ki * cos_i + _dot_f32(ki, perm_ref[...]) * sin_i
    ki_ref[0] = ki.astype(ki_ref.dtype)
    wi_ref[0] = z[:, o + 3 * LANES:o + 4 * LANES] * (C_IDX_HEADS ** -0.5 * C_IDX_DH ** -0.5)
    cqb = cq.astype(MXU_DT)
    qf = _dot(cqb, wqb_ref[...])
    qi = _dot(cqb, wiq_ref[...])
    scale = (C_DN + C_DR) ** -0.5 * LOG2E
    nh = C_HEADS * LANES
    for hd in range(C_HEADS):
        q_abs = _dot(qf[:, hd * C_DN:(hd + 1) * C_DN].astype(MXU_DT), wuk_ref[hd])
        cs = slice(hd * LANES, (hd + 1) * LANES)
        q_rope = qf[:, nh:2 * nh][:, cs] * cos_q + qf[:, 2 * nh:3 * nh][:, cs] * sin_q
        qa_ref[0, hd, :, 0:C_KVL] = (q_abs * scale).astype(qa_ref.dtype)
        qa_ref[0, hd, :, C_KVL:C_KPAD] = (q_rope * scale).astype(qa_ref.dtype)
        qi_h = qi[:, 0:nh][:, cs] * cos_i + qi[:, nh:2 * nh][:, cs] * sin_i
        qi_ref[0, hd] = qi_h.astype(qi_ref.dtype)


def _rot_cols(w, n_heads, dh, d):
    w = w.reshape(w.shape[0], n_heads, dh)
    h = d // 2
    return jnp.concatenate([w[..., h:d], w[..., 0:h], jnp.zeros_like(w[..., d:])], -1).reshape(w.shape[0], n_heads * dh)


def _odd_prep(h3, pos3, w_in, q_norm, kv_norm, w_qb, w_uk, w_iq, ik_g, ik_b, tm):
    B, T, D = h3.shape
    f32 = jnp.float32
    o = 0
    parts = []
    for s in (C_QL, C_KVL, C_DR, C_IDX_DH, C_IDX_HEADS):
        parts.append(w_in[:, o:o + s]); o += s
    w_cq, w_ckv, w_kr, w_ik, w_iw = parts
    pc = lambda w: _pad_cols(w, LANES)
    win = jnp.concatenate([w_cq, w_ckv, pc(w_kr), pc(_rot_cols(w_kr, 1, C_DR, C_DR)), pc(w_ik), pc(w_iw)],
                          -1).astype(MXU_DT)
    wq = w_qb.reshape(C_QL, C_HEADS, C_DN + C_DR)
    w_nope = wq[..., :C_DN].reshape(C_QL, C_HEADS * C_DN)
    w_rope = wq[..., C_DN:].reshape(C_QL, C_HEADS * C_DR)
    wqb = jnp.concatenate([w_nope, _pad_heads(w_rope, C_HEADS, C_DR),
                           _pad_heads(_rot_cols(w_rope, C_HEADS, C_DR, C_DR), C_HEADS, C_DR)], -1).astype(MXU_DT)
    wiq = jnp.concatenate([_pad_heads(w_iq, C_IDX_HEADS, C_IDX_DH),
                           _pad_heads(_rot_cols(w_iq, C_IDX_HEADS, C_IDX_DH, C_IDX_DR), C_IDX_HEADS, C_IDX_DH)],
                          -1).astype(MXU_DT)
    wuk = w_uk.transpose(1, 2, 0).astype(MXU_DT)
    inv_q, sgn_q = _rope_rows(C_DR)
    inv_i, sgn_i = _rope_rows(C_IDX_DR)
    rope = jnp.stack([jnp.tile(v, LANES // C_DR) for v in (inv_q, sgn_q, inv_i, sgn_i)])
    rope = jnp.concatenate([rope, jnp.zeros((SUBLANES - 4, LANES), f32)])
    hh = C_IDX_DR // 2
    src = np.arange(LANES)
    src[:hh] += hh
    src[hh:C_IDX_DR] -= hh
    perm = np.zeros((LANES, LANES), np.float32)
    perm[src, np.arange(LANES)] = 1.0
    ikg = _pad_cols(ik_g[None, :], LANES)
    ikb = _pad_cols(ik_b[None, :], LANES)
    out_shape = (
        jax.ShapeDtypeStruct((B, C_HEADS, T, C_KPAD), MXU_DT),
        jax.ShapeDtypeStruct((B, T, C_KPAD), MXU_DT),
        jax.ShapeDtypeStruct((B, C_IDX_HEADS, T, LANES), MXU_DT),
        jax.ShapeDtypeStruct((B, T, LANES), MXU_DT),
        jax.ShapeDtypeStruct((B, T, LANES), f32),
    )
    row = lambda w: pl.BlockSpec((1, tm, w), lambda b, i: (b, i, 0))
    hm = lambda w: pl.BlockSpec((1, C_HEADS, tm, w), lambda b, i: (b, 0, i, 0))
    return pl.pallas_call(
        _odd_prep_kernel,
        grid=(B, T // tm),
        in_specs=[row(D), row(1), _const_spec(win.shape), _const_spec(wqb.shape), _const_spec(wiq.shape),
                  _const_spec(wuk.shape), _const_spec((1, C_QL)), _const_spec((1, C_KVL)),
                  _const_spec((1, LANES)), _const_spec((1, LANES)), _const_spec(rope.shape),
                  _const_spec(perm.shape)],
        out_specs=(hm(C_KPAD), row(C_KPAD), hm(LANES), row(LANES), row(LANES)),
        out_shape=out_shape,
        compiler_params=_cparams("parallel", "parallel"),
    )(h3, pos3, win, wqb, wiq, wuk, q_norm[None, :], kv_norm[None, :], ikg, ikb, rope, jnp.asarray(perm))


def _dsa_kernel(qa_ref, qi_ref, wi_ref, kv_ref, ki_ref, wuv_ref, eye_ref, tri_ref, out_ref,
                key_scr, hi_scr, lo_scr, acc_scr, *, QB, KC, SUB, TOPK, HG):
    f32, i32, i16 = jnp.float32, jnp.int32, jnp.int16
    i = pl.program_id(1)
    s0 = i * QB
    H = C_HEADS
    NG = H // HG
    RG = HG * QB
    nch = (s0 + QB + KC - 1) // KC
    PK = PACKED_ROWS
    tq_row = s0 + _iota((1, QB), 1)
    w_t = wi_ref[0].T
    one, zero = jnp.ones((), MXU_DT), jnp.zeros((), MXU_DT)

    def idx_body(c, _):
        ks = pl.multiple_of(c * KC, KC)
        kic = ki_ref[0, pl.ds(ks, KC), :]
        isc = None
        for hd in range(C_IDX_HEADS):
            s = jnp.maximum(_dot_nt(kic, qi_ref[0, hd]), 0.0) * w_t[hd:hd + 1, :]
            isc = s if isc is None else isc + s
        isc = jnp.where(isc == 0.0, 0.0, isc)
        kpos = ks + _iota((KC, 1), 0)
        isc = jnp.where(kpos <= tq_row, isc, NEG)
        bits = lax.bitcast_convert_type(isc, i32)
        key = jnp.where(bits < 0, bits ^ jnp.int32(0x7FFFFFFF), bits)
        key_scr[c] = key
        k3 = key.reshape(KC // PK, PK, QB)
        hi_scr[c] = lax.shift_right_arithmetic(k3, 16).astype(i16)
        lo_scr[c] = ((k3 & 0xFFFF) - 32768).astype(i16)
        return 0

    lax.fori_loop(0, nch, idx_body, 0)

    def rep16(v):
        return jnp.broadcast_to(v, (PK, QB)).astype(i16)[None]

    def count16(pred):
        def body(c, accs):
            x = jnp.where(pred(hi_scr[c], lo_scr[c]), one, zero)
            accs = list(accs)
            for r in range(KC // PK):
                accs[r % len(accs)] = accs[r % len(accs)] + x[r]
            return tuple(accs)
        accs = lax.fori_loop(0, nch, body, tuple(jnp.zeros((PK, QB), MXU_DT) for _ in range(4)))
        acc = (accs[0] + accs[1]) + (accs[2] + accs[3])
        return jnp.sum(acc.astype(f32), axis=0, keepdims=True)

    def bisect(pick, base):
        def body(b, t):
            cand = t + lax.shift_left(jnp.int32(1), 15 - b)
            c16 = rep16(cand)
            return jnp.where(base + count16(lambda h, l: pick(h, l) >= c16) >= TOPK, cand, t)
        return lax.fori_loop(0, 16, body, jnp.full((1, QB), -32768, i32))

    thi = bisect(lambda h, l: h, 0.0)
    thi16 = rep16(thi)

    def bucket_body(c, _):
        lo_scr[c] = jnp.where(hi_scr[c] == thi16, lo_scr[c], jnp.full((), -32768, i16))
        return 0

    lax.fori_loop(0, nch, bucket_body, 0)
    n_hi = count16(lambda h, l: h > thi16)
    tlo = bisect(lambda h, l: l, n_hi)
    tlo16 = rep16(tlo)
    thr = thi * 65536 + (tlo + 32768)
    room = TOPK - (n_hi + count16(lambda h, l: l > tlo16))
    n_eq = count16(lambda h, l: (h == thi16) & (l == tlo16))

    @pl.when(jnp.max(jnp.where(n_eq > room, 1, 0)) > 0)
    def _():
        def body(c, before):
            key = key_scr[c]
            eq = key == thr
            seen = before + _dot(tri_ref[...], jnp.where(eq, 1.0, 0.0).astype(MXU_DT))
            key_scr[c] = jnp.where(eq & (seen > room), key - 1, key)
            return seen[KC - 1:KC, :]
        lax.fori_loop(0, nch, body, jnp.zeros((1, QB), f32))

    acc_scr[...] = jnp.zeros(acc_scr.shape, f32)

    def att_body(c, carry):
        ms, ls = list(carry[0]), list(carry[1])
        ks = pl.multiple_of(c * KC, KC)
        key = key_scr[c]
        kpos = ks + _iota((KC, 1), 0)
        sel_t = (key >= thr) & (kpos <= tq_row)
        keep = _dot_nt(eye_ref[...], jnp.where(sel_t, 1.0, 0.0).astype(MXU_DT))
        bias = jnp.where(keep > 0.5, 0.0, NEG)
        for u in range(KC // SUB):
            kvc = kv_ref[0, pl.ds(pl.multiple_of(ks + u * SUB, SUB), SUB), :]
            b_u = bias[:, u * SUB:(u + 1) * SUB][None]
            for g in range(NG):
                qg = qa_ref[0, g * HG:(g + 1) * HG].reshape(RG, C_KPAD)
                s = _dot_nt(qg, kvc).reshape(HG, QB, SUB) + b_u
                m_new = jnp.maximum(ms[g], jnp.max(s, axis=-1, keepdims=True))
                p = jnp.exp2(s - m_new)
                alpha = jnp.exp2(ms[g] - m_new)
                ls[g] = alpha * ls[g] + jnp.sum(p, axis=-1, keepdims=True)
                ms[g] = m_new
                rows = slice(g * RG, (g + 1) * RG)
                pv = _dot(p.reshape(RG, SUB).astype(MXU_DT), kvc[:, 0:C_KVL])
                acc_scr[rows] = alpha.reshape(RG, 1) * acc_scr[rows] + pv
        return tuple(ms), tuple(ls)

    m0 = tuple(jnp.full((HG, QB, 1), SOFTMAX_M0, f32) for _ in range(NG))
    l0 = tuple(jnp.zeros((HG, QB, 1), f32) for _ in range(NG))
    carry = lax.fori_loop(0, nch // 2, lambda c2, cr: att_body(2 * c2 + 1, att_body(2 * c2, cr)), (m0, l0))
    _, l_fin = lax.cond(nch % 2 == 1, lambda cr: att_body(nch - 1, cr), lambda cr: cr, carry)
    for g in range(NG):
        o_lat = (acc_scr[g * RG:(g + 1) * RG] / l_fin[g].reshape(RG, 1)).astype(MXU_DT)
        for k in range(HG):
            hd = g * HG + k
            out_ref[0, :, hd * C_DV:(hd + 1) * C_DV] = _dot(o_lat[k * QB:(k + 1) * QB], wuv_ref[hd]).astype(out_ref.dtype)


def _dsa(qa, kv, qi, ki, wi, w_uv):
    B, H, T, _ = qa.shape
    QB = min(DSA_Q_BLOCK, T)
    KC = min(KEY_CHUNK, T)
    topk = min(C_TOPK, T // 4)
    wuv = w_uv.transpose(1, 0, 2).astype(MXU_DT)
    assert T // PACKED_ROWS <= 256
    eye = jnp.eye(QB, dtype=MXU_DT)
    tri = jnp.tril(jnp.ones((KC, KC), MXU_DT))
    kern = functools.partial(_dsa_kernel, QB=QB, KC=KC, SUB=min(KEY_SUB, KC), TOPK=topk, HG=DSA_HEAD_GROUP)
    half_words = pltpu.VMEM((T // KC, KC // PACKED_ROWS, PACKED_ROWS, QB), jnp.int16)
    return pl.pallas_call(
        kern,
        grid=(B, T // QB),
        in_specs=[pl.BlockSpec((1, H, QB, C_KPAD), lambda b, i: (b, 0, i, 0)),
                  pl.BlockSpec((1, H, QB, LANES), lambda b, i: (b, 0, i, 0)),
                  pl.BlockSpec((1, QB, LANES), lambda b, i: (b, i, 0)),
                  pl.BlockSpec((1, T, C_KPAD), lambda b, i: (b, 0, 0)),
                  pl.BlockSpec((1, T, LANES), lambda b, i: (b, 0, 0)),
                  _const_spec(wuv.shape), _const_spec(eye.shape), _const_spec(tri.shape)],
        out_specs=pl.BlockSpec((1, QB, H * C_DV), lambda b, i: (b, i, 0)),
        out_shape=jax.ShapeDtypeStruct((B, T, H * C_DV), MXU_DT),
        scratch_shapes=[pltpu.VMEM((T // KC, KC, QB), jnp.int32),
                        half_words, half_words,
                        pltpu.VMEM((H * QB, C_KVL), jnp.float32)],
        compiler_params=_cparams("parallel", "arbitrary"),
    )(qa, qi, wi, kv, ki, wuv, eye, tri)


def _post_kernel(*refs, n_mix, n_ff):
    h_ref = refs[0]
    mix = refs[1:1 + 2 * n_mix]
    g1, b1, w1_ref, w2_ref, g2, b2, wg_ref, p_ref, wp_ref, out_ref = refs[1 + 2 * n_mix:]
    h = h_ref[...]
    y = _dot(mix[0][...].astype(MXU_DT), mix[1][...])
    for k in range(1, n_mix):
        y = y + _dot(mix[2 * k][...].astype(MXU_DT), mix[2 * k + 1][...])
    h1 = _layer_norm(DN_ALPHA * h + y, g1[...], b1[...])
    h1b = h1.astype(MXU_DT)
    ff = D_FF // n_ff
    u = None
    for k in range(n_ff):
        a = jnp.square(jnp.maximum(_dot(h1b, w1_ref[:, k * ff:(k + 1) * ff]), 0.0))
        t = _dot(a.astype(MXU_DT), w2_ref[k * ff:(k + 1) * ff, :])
        u = t if u is None else u + t
    h2 = _layer_norm(DN_ALPHA * h1 + u, g2[...], b2[...])
    gate = jax.nn.sigmoid(_dot(h2.astype(MXU_DT), wg_ref[...]))
    out_ref[...] = h2 + gate * _dot(p_ref[...].astype(MXU_DT), wp_ref[...])


def _post(h2d, mixes, ln1_g, ln1_b, w1, w2, ln2_g, ln2_b, wg, p2d, wp, tm):
    M, D = h2d.shape
    row = lambda w: pl.BlockSpec((tm, w), lambda i: (i, 0))
    vec = lambda v: v[None, :]
    in_specs = [row(D)]
    args = [h2d]
    for x, w in mixes:
        in_specs += [row(x.shape[1]), _const_spec(w.shape)]
        args += [x, w]
    in_specs += [_const_spec((1, D)), _const_spec((1, D)), _const_spec(w1.shape), _const_spec(w2.shape),
                 _const_spec((1, D)), _const_spec((1, D)), _const_spec(wg.shape), row(D_PLE), _const_spec(wp.shape)]
    args += [vec(ln1_g), vec(ln1_b), w1, w2, vec(ln2_g), vec(ln2_b), wg, p2d, wp]
    kern = functools.partial(_post_kernel, n_mix=len(mixes), n_ff=4)
    return pl.pallas_call(
        kern,
        grid=(M // tm,),
        in_specs=in_specs,
        out_specs=row(D),
        out_shape=jax.ShapeDtypeStruct((M, D), jnp.float32),
        compiler_params=_cparams("parallel"),
    )(*args)


def kernel(x, p, positions, e_w_in, e_a_conv, e_a_i_b, e_a_f_b, e_a_norm, e_b_cmp_pos, e_b_cmp_w1, e_b_cmp_w2, e_b_g_b, e_w_out, o_w_in, o_q_norm, o_kv_norm, o_w_qb, o_w_uk, o_w_uv, o_w_iq, o_ik_g, o_ik_b, o_w_out, ln1_g, ln1_b, ln2_g, ln2_b, mlp_w1, mlp_w2, ple_gate_w, ple_w):
    B, T, D = x.shape
    M = B * T
    tm = min(ROW_TILE, T)
    h = x
    pos3 = positions[..., None]
    bf = lambda w: w.astype(MXU_DT)
    for i in range(DEPTH):
        j = i // 2
        if i % 2 == 0:
            qk, av, ao, small, bq, bc, bs = _even_proj(h, _even_w_in_aug(e_w_in[j]), tm)
            ya = _mlstm(qk, av, ao, small, e_a_conv[j], e_a_i_b[j], e_a_f_b[j], e_a_norm[j])
            cmp = _nsa_compress(bc, e_b_cmp_pos[j], e_b_cmp_w1[j], e_b_cmp_w2[j])
            yb = _nsa(bq, bs, cmp, small, e_b_g_b[j])
            w_out = e_w_out[j]
            mixes = [(ya.reshape(M, A_W), bf(w_out[:A_W])),
                     (yb.reshape(M, B_HEADS * LANES), bf(_pad_heads(w_out[A_W:], B_HEADS, B_DH, axis=0)))]
        else:
            qa, kv, qi, ki, wi = _odd_prep(h, pos3, o_w_in[j], o_q_norm[j], o_kv_norm[j], o_w_qb[j], o_w_uk[j],
                                           o_w_iq[j], o_ik_g[j], o_ik_b[j], tm)
            o = _dsa(qa, kv, qi, ki, wi, o_w_uv[j])
            mixes = [(o.reshape(M, C_HEADS * C_DV), bf(o_w_out[j]))]
        h = _post(h.reshape(M, D), mixes, ln1_g[i], ln1_b[i], bf(mlp_w1[i]), bf(mlp_w2[i]), ln2_g[i], ln2_b[i],
                  bf(ple_gate_w[i]), p[i].reshape(M, D_PLE), bf(ple_w[i]), min(MLP_ROW_TILE, T)).reshape(B, T, D)
    return h
```

```python
import functools

import numpy as np
import jax
import jax.numpy as jnp
from jax import lax
from jax.experimental import pallas as pl
from jax.experimental.pallas import tpu as pltpu

D_MODEL = 1024
DEPTH = 4
D_PLE = 256
D_FF = 4 * D_MODEL
DN_ALPHA = (2.0 * DEPTH) ** 0.25
LN_EPS = 1e-5
NEG = -1e30

A_HEADS = 4
A_DH = D_MODEL // 8
A_W = A_HEADS * A_DH
A_CONV = 4

B_HEADS = 8
B_DH = 64
B_KV = 2
B_HPG = B_HEADS // B_KV
B_CMP_LEN = 32
B_CMP_STRIDE = 16
B_CMP_HID = 128
B_SEL_BLK = 64
B_SEL_N = 16
B_WIN = 512

C_HEADS = 8
C_DN = 128
C_DR = 64
C_DV = 128
C_QL = 512
C_KVL = 256
C_IDX_HEADS = 8
C_IDX_DH = 64
C_IDX_DR = 32
C_TOPK = 256
ROPE_BASE = 10000.0

LANES = 128
SUBLANES = 8
PACKED_ROWS = 16
VMEM_LIMIT_BYTES = 56 * 2**20
MXU_DT = jnp.bfloat16
INT_MIN = -2**31

MLSTM_CHUNK = 256
MLSTM_SEQS = 1
ROW_TILE = 512
MLP_ROW_TILE = 512
Q_BLOCK = 256
DSA_Q_BLOCK = 256
DSA_HEAD_GROUP = 4
KEY_CHUNK = 512
KEY_SUB = 512
C_KDIM = C_DN + LANES

LOG2E = 1.4426950408889634
SOFTMAX_M0 = 0.5 * NEG
SOFTMAX_TINY = 1e-30


def _cparams(*sem):
    return pltpu.CompilerParams(dimension_semantics=sem, vmem_limit_bytes=VMEM_LIMIT_BYTES)


def _const_spec(shape):
    nd = len(shape)
    return pl.BlockSpec(shape, lambda *_: (0,) * nd, pipeline_mode=pl.Buffered(1))


def _dot(a, b):
    return jnp.dot(a, b, preferred_element_type=jnp.float32)


def _dot_nt(a, b):
    return lax.dot_general(a, b, (((1,), (1,)), ((), ())), preferred_element_type=jnp.float32)


def _dot_tn(a, b):
    return lax.dot_general(a, b, (((0,), (0,)), ((), ())), preferred_element_type=jnp.float32)


def _dot_f32(a, b):
    return jnp.dot(a, b, preferred_element_type=jnp.float32, precision=lax.Precision.HIGHEST)


def _layer_norm(x, g, b):
    mu = jnp.mean(x, axis=-1, keepdims=True)
    var = jnp.mean(jnp.square(x - mu), axis=-1, keepdims=True)
    return (x - mu) * lax.rsqrt(var + LN_EPS) * g + b


def _masked_softmax2(s, bias):
    s = s + bias
    m = jnp.maximum(jnp.max(s, axis=-1, keepdims=True), SOFTMAX_M0)
    e = jnp.exp2(s - m)
    return e / jnp.maximum(jnp.sum(e, axis=-1, keepdims=True), SOFTMAX_TINY)


def _iota(shape, dim):
    return lax.broadcasted_iota(jnp.int32, shape, dim)


def _pad_heads(w, n_heads, dh, axis=-1):
    axis = axis % w.ndim
    shp = w.shape[:axis] + (n_heads, dh) + w.shape[axis + 1:]
    w = w.reshape(shp)
    pad = [(0, 0)] * w.ndim
    pad[axis + 1] = (0, LANES - dh)
    w = jnp.pad(w, pad)
    return w.reshape(shp[:axis] + (n_heads * LANES,) + shp[axis + 2:])


def _pad_cols(w, width):
    return jnp.pad(w, [(0, 0)] * (w.ndim - 1) + [(0, width - w.shape[-1])])


def _even_proj_kernel(h_ref, w_ref, qk_ref, av_ref, ao_ref, sm_ref, bq_ref, bc_ref, bs_ref):
    z = _dot(h_ref[0].astype(MXU_DT), w_ref[...])
    o = 0
    qk_ref[0] = z[:, o:o + 2 * A_W]; o += 2 * A_W
    av_ref[0] = z[:, o:o + A_W].astype(av_ref.dtype); o += A_W
    ao_ref[0] = z[:, o:o + A_W]; o += A_W
    sm_ref[0] = z[:, o:o + LANES]; o += LANES
    for hd in range(B_HEADS):
        bq_ref[0, hd] = z[:, o:o + LANES].astype(bq_ref.dtype); o += LANES
    for j in range(2 * B_KV):
        bc_ref[0, j] = z[:, o:o + LANES]; o += LANES
    for j in range(4 * B_KV):
        bs_ref[0, j] = z[:, o:o + LANES].astype(bs_ref.dtype); o += LANES


def _even_proj(h3, w_aug, tm):
    B, T, D = h3.shape
    n = w_aug.shape[1]
    f32 = jnp.float32
    out_shape = (
        jax.ShapeDtypeStruct((B, T, 2 * A_W), f32),
        jax.ShapeDtypeStruct((B, T, A_W), MXU_DT),
        jax.ShapeDtypeStruct((B, T, A_W), f32),
        jax.ShapeDtypeStruct((B, T, LANES), f32),
        jax.ShapeDtypeStruct((B, B_HEADS, T, LANES), MXU_DT),
        jax.ShapeDtypeStruct((B, 2 * B_KV, T, LANES), f32),
        jax.ShapeDtypeStruct((B, 4 * B_KV, T, LANES), MXU_DT),
    )
    row = lambda w: pl.BlockSpec((1, tm, w), lambda b, i: (b, i, 0))
    hm = lambda nh: pl.BlockSpec((1, nh, tm, LANES), lambda b, i: (b, 0, i, 0))
    return pl.pallas_call(
        _even_proj_kernel,
        grid=(B, T // tm),
        in_specs=[row(D), _const_spec((D, n))],
        out_specs=(row(2 * A_W), row(A_W), row(A_W), row(LANES), hm(B_HEADS), hm(2 * B_KV), hm(4 * B_KV)),
        out_shape=out_shape,
        compiler_params=_cparams("parallel", "parallel"),
    )(h3, w_aug)


def _even_w_in_aug(w_in):
    sizes = (A_W, A_W, A_W, A_W, A_HEADS, A_HEADS, B_HEADS * B_DH) + (B_KV * B_DH,) * 6 + (3 * B_HEADS,)
    parts, o = [], 0
    for s in sizes:
        parts.append(w_in[:, o:o + s]); o += s
    aq, ak, av, ao, ai, af, bq, bkc, bvc, bks, bvs, bkw, bvw, bg = parts
    small = _pad_cols(jnp.concatenate([ai, af, bg], -1), LANES)
    ph = lambda w: _pad_heads(w, B_KV, B_DH)
    bq = bq * (B_DH ** -0.5 * LOG2E)
    cols = [aq, ak, av, ao, small, _pad_heads(bq, B_HEADS, B_DH),
            ph(bkc), ph(bvc), ph(bks), ph(bvs), ph(bkw), ph(bvw)]
    return jnp.concatenate(cols, -1).astype(MXU_DT)


def _mlstm_kernel(qk_ref, v_ref, o_ref, gc_ref, gr_ref, cw_ref, bc_ref, br_ref, ng_ref, tri_ref, out_ref,
                  xs_scr, ct_scr, n_scr, m_scr, *, L, NB):
    c = pl.program_id(1)
    f32 = jnp.float32
    W2 = 2 * A_W

    @pl.when(c == 0)
    def _():
        xs_scr[:, 0:SUBLANES, :] = jnp.zeros((NB, SUBLANES, W2), f32)
        ct_scr[...] = jnp.zeros(ct_scr.shape, f32)
        n_scr[...] = jnp.zeros(n_scr.shape, f32)
        m_scr[...] = jnp.zeros(m_scr.shape, f32)

    tri = tri_ref[...]
    causal = _iota((L, L), 1) <= _iota((L, L), 0)
    cw = cw_ref[...]
    for bb in range(NB):
        _mlstm_chunk(bb, qk_ref, v_ref, o_ref, gc_ref, gr_ref, cw, bc_ref, br_ref, ng_ref, tri, causal, out_ref,
                     xs_scr, ct_scr, n_scr, m_scr, L)


def _mlstm_chunk(bb, qk_ref, v_ref, o_ref, gc_ref, gr_ref, cw, bc_ref, br_ref, ng_ref, tri, causal, out_ref,
                 xs_scr, ct_scr, n_scr, m_scr, L):
    f32 = jnp.float32
    xs_scr[bb, SUBLANES:SUBLANES + L, :] = qk_ref[bb]
    base = SUBLANES - (A_CONV - 1)
    acc = xs_scr[bb, base:base + L, :] * cw[0:1, :]
    for j in range(1, A_CONV):
        acc = acc + xs_scr[bb, base + j:base + j + L, :] * cw[j:j + 1, :]
    qk = acc * jax.nn.sigmoid(acc)
    xs_scr[bb, 0:SUBLANES, :] = xs_scr[bb, L:L + SUBLANES, :]

    gc = gc_ref[bb] + bc_ref[...]
    gr = gr_ref[bb, 0] + br_ref[...]
    ig_c = gc[:, 0:A_HEADS]
    b_c = _dot_f32(tri, jax.nn.log_sigmoid(gc[:, A_HEADS:2 * A_HEADS]))
    ig_r = gr[0:A_HEADS, :]
    b_r = _dot_f32(jax.nn.log_sigmoid(gr[A_HEADS:2 * A_HEADS, :]), tri.T)

    for hd in range(A_HEADS):
        sl = slice(hd * A_DH, (hd + 1) * A_DH)
        q_h = (qk[:, sl] * A_DH ** -0.5).astype(MXU_DT)
        k_f = qk[:, A_W + hd * A_DH:A_W + (hd + 1) * A_DH]
        k_h = k_f.astype(MXU_DT)
        v_h = v_ref[bb][:, sl]
        bi = b_c[:, hd:hd + 1]
        ic = ig_c[:, hd:hd + 1]
        dmat = jnp.where(causal, bi - b_r[hd:hd + 1, :] + ig_r[hd:hd + 1, :], NEG)
        m_prev = m_scr[bb, hd][:, 0:1]
        m_inter = bi + m_prev
        m_t = jnp.maximum(m_inter, jnp.max(dmat, axis=1, keepdims=True))
        e_inter = jnp.exp(m_inter - m_t)
        s = _dot_nt(q_h, k_h) * jnp.exp(dmat - m_t)
        ct = ct_scr[bb, hd]
        nrow = n_scr[bb, hd]
        num = e_inter * _dot(q_h, ct.astype(MXU_DT)) + _dot(s.astype(MXU_DT), v_h)
        den = e_inter * jnp.sum(q_h.astype(f32) * nrow, axis=1, keepdims=True) + jnp.sum(s, axis=1, keepdims=True)
        hc = num / jnp.maximum(jnp.abs(den), jnp.exp(-m_t))
        b_last = bi[L - 1:L, :]
        dec = b_last - bi + ic
        m_new = jnp.maximum(b_last + m_prev, jnp.max(dec, axis=0, keepdims=True))
        wgt = jnp.exp(dec - m_new)
        e_st = jnp.exp(b_last + m_prev - m_new)
        kw = k_f * wgt
        ct_scr[bb, hd] = e_st * ct + _dot_tn(kw.astype(MXU_DT), v_h)
        n_scr[bb, hd] = e_st * nrow + jnp.sum(kw, axis=0, keepdims=True)
        m_scr[bb, hd] = jnp.broadcast_to(m_new, (1, LANES))
        mu = jnp.mean(hc, axis=1, keepdims=True)
        var = jnp.mean(jnp.square(hc - mu), axis=1, keepdims=True)
        hn = (hc - mu) * lax.rsqrt(var + LN_EPS) * ng_ref[:, sl]
        out_ref[bb, :, sl] = (hn * jax.nn.sigmoid(o_ref[bb][:, sl])).astype(out_ref.dtype)


def _mlstm(qk, av, ao, small, conv_w, i_b, f_b, norm_g):
    B, T, _ = qk.shape
    L = min(MLSTM_CHUNK, T)
    N = T // L
    f32 = jnp.float32
    gates = small[..., 0:2 * A_HEADS]
    gates_r = gates.reshape(B, N, L, 2 * A_HEADS).transpose(0, 1, 3, 2)
    bias = jnp.concatenate([i_b, f_b]).astype(f32)
    tri = jnp.tril(jnp.ones((L, L), f32))
    NB = MLSTM_SEQS if B % MLSTM_SEQS == 0 else 1
    row = lambda w: pl.BlockSpec((NB, L, w), lambda b, c: (b, c, 0))
    kern = functools.partial(_mlstm_kernel, L=L, NB=NB)
    return pl.pallas_call(
        kern,
        grid=(B // NB, N),
        in_specs=[row(2 * A_W), row(A_W), row(A_W), row(2 * A_HEADS),
                  pl.BlockSpec((NB, 1, 2 * A_HEADS, L), lambda b, c: (b, c, 0, 0)),
                  _const_spec((A_CONV, 2 * A_W)), _const_spec((1, 2 * A_HEADS)), _const_spec((2 * A_HEADS, 1)),
                  _const_spec((1, A_W)), _const_spec((L, L))],
        out_specs=row(A_W),
        out_shape=jax.ShapeDtypeStruct((B, T, A_W), MXU_DT),
        scratch_shapes=[pltpu.VMEM((NB, L + 2 * SUBLANES, 2 * A_W), f32),
                        pltpu.VMEM((NB, A_HEADS, A_DH, A_DH), f32),
                        pltpu.VMEM((NB, A_HEADS, 1, A_DH), f32),
                        pltpu.VMEM((NB, A_HEADS, 1, LANES), f32)],
        compiler_params=_cparams("parallel", "arbitrary"),
    )(qk, av, ao, gates, gates_r, conv_w, bias[None, :], bias[:, None], norm_g[None, :], tri)


def _nsa_cmp_kernel(x_ref, w1a_ref, w1b_ref, w2_ref, pos_ref, out_ref):
    n = x_ref.shape[2] // B_CMP_STRIDE
    half = B_CMP_STRIDE * LANES
    for j in range(2):
        bias = (_dot(pos_ref[j, :, 0:half].astype(MXU_DT), w1a_ref[j])
                + _dot(pos_ref[j, :, half:2 * half].astype(MXU_DT), w1b_ref[j]))[0:1, :]
        for g in range(B_KV):
            u = jnp.concatenate([x_ref[0, j * B_KV + g, pl.ds(r, n, stride=B_CMP_STRIDE), :]
                                 for r in range(B_CMP_STRIDE)], axis=1).astype(MXU_DT)
            a = _dot(u, w1a_ref[j])
            bm = _dot(u, w1b_ref[j])
            pre = a + pltpu.roll(bm, n - 1, 0) + bias
            hid = jax.nn.gelu(pre)
            out_ref[0, j * B_KV + g] = _dot(hid.astype(MXU_DT), w2_ref[j])


def _nsa_compress(bc, cmp_pos, cmp_w1, cmp_w2):
    B, _, T, _ = bc.shape
    nblk = T // B_CMP_STRIDE
    half = B_CMP_STRIDE * LANES
    w1 = jnp.pad(cmp_w1.reshape(2, B_CMP_LEN, B_DH, B_CMP_HID), ((0, 0), (0, 0), (0, LANES - B_DH), (0, 0)))
    w1 = w1.reshape(2, B_CMP_LEN * LANES, B_CMP_HID).astype(MXU_DT)
    w1a, w1b = w1[:, :half], w1[:, half:]
    w2 = _pad_cols(cmp_w2, LANES).astype(MXU_DT)
    pos = jnp.pad(cmp_pos, ((0, 0), (0, 0), (0, LANES - B_DH))).reshape(2, 1, B_CMP_LEN * LANES)
    pos = jnp.broadcast_to(pos, (2, SUBLANES, B_CMP_LEN * LANES))
    return pl.pallas_call(
        _nsa_cmp_kernel,
        grid=(B,),
        in_specs=[pl.BlockSpec((1, 2 * B_KV, T, LANES), lambda b: (b, 0, 0, 0)),
                  _const_spec(w1a.shape), _const_spec(w1b.shape), _const_spec(w2.shape), _const_spec(pos.shape)],
        out_specs=pl.BlockSpec((1, 2 * B_KV, nblk, LANES), lambda b: (b, 0, 0, 0)),
        out_shape=jax.ShapeDtypeStruct((B, 2 * B_KV, nblk, LANES), jnp.float32),
        compiler_params=_cparams("parallel"),
    )(bc, w1a, w1b, w2, pos)


def _nsa_kernel(q_ref, kv_ref, cmp_ref, sm_ref, gb_ref, ovt_ref, exp_ref, out_ref, acc_scr, ocmp_scr,
                *, QB, KC, WL, NSB, NSEL):
    f32 = jnp.float32
    i = pl.program_id(1)
    s0 = i * QB
    R = B_HPG * QB
    tq_col = s0 + _iota((QB, 1), 0)
    tq_row = s0 + _iota((1, QB), 1)
    nch = (s0 + QB + KC - 1) // KC
    gates = jax.nn.sigmoid(sm_ref[0] + gb_ref[...])
    ncmp = cmp_ref.shape[2]
    cmp_end = _iota((1, ncmp), 1) * B_CMP_STRIDE + (B_CMP_LEN - 1)
    cmp_bias = jnp.where(cmp_end <= tq_col, 0.0, NEG)[None]
    jb = _iota((NSB, 1), 0)
    cur = lax.shift_right_logical(tq_row, int(np.log2(B_SEL_BLK)))
    forced = (jb == 0) | (jb == cur) | (jb == cur - 1)
    valid = jb * B_SEL_BLK <= tq_row
    wstart = pl.multiple_of(jnp.maximum(s0 + QB - WL, 0), QB)
    wpos = wstart + _iota((1, WL), 1)
    win_bias = jnp.where((wpos <= tq_col) & (wpos > tq_col - B_WIN), 0.0, NEG)[None]

    sel_b = []
    for g in range(B_KV):
        qs = q_ref[0, g * B_HPG:(g + 1) * B_HPG].reshape(R, LANES)
        kcm = cmp_ref[0, g].astype(MXU_DT)
        vcm = cmp_ref[0, B_KV + g].astype(MXU_DT)
        p_cmp = _masked_softmax2(_dot_nt(qs, kcm).reshape(B_HPG, QB, ncmp), cmp_bias)
        ocmp_scr[g * R:(g + 1) * R] = _dot(p_cmp.reshape(R, ncmp).astype(MXU_DT), vcm)
        psum = jnp.sum(p_cmp, axis=0)
        imp_t = lax.dot_general(ovt_ref[...], psum, (((1,), (1,)), ((), ())),
                                preferred_element_type=f32, precision=lax.Precision.HIGHEST)
        sc = jnp.where(forced, 1e6, imp_t)
        sc = jnp.where(valid, sc, NEG)
        rank = jnp.zeros((NSB, QB), f32)
        for j in range(NSB):
            cj = jnp.broadcast_to(sc[j:j + 1, :], (NSB, QB))
            ahead = (cj > sc) | ((cj == sc) & (jb > j))
            rank = rank + jnp.where(ahead, 1.0, 0.0)
        taken = jnp.where(rank < NSEL, 1.0, 0.0)
        if NSB < LANES:
            taken = jnp.concatenate([taken, jnp.zeros((LANES - NSB, QB), f32)], axis=0)
        sel_b.append(taken.T.astype(MXU_DT))

    acc_scr[...] = jnp.zeros(acc_scr.shape, f32)

    def sel_body(c, carry):
        ms, ls = list(carry[0]), list(carry[1])
        ks = pl.multiple_of(c * KC, KC)
        causal = ks + _iota((1, KC), 1) <= tq_col
        for g in range(B_KV):
            qs = q_ref[0, g * B_HPG:(g + 1) * B_HPG].reshape(R, LANES)
            kc_ = kv_ref[0, 0 * B_KV + g, pl.ds(ks, KC), :]
            vc_ = kv_ref[0, 1 * B_KV + g, pl.ds(ks, KC), :]
            mk = (_dot(sel_b[g], exp_ref[c]) > 0.5) & causal
            s3 = _dot_nt(qs, kc_).reshape(B_HPG, QB, KC) + jnp.where(mk, 0.0, NEG)[None]
            m_new = jnp.maximum(ms[g], jnp.max(s3, axis=-1, keepdims=True))
            p = jnp.exp2(s3 - m_new)
            alpha = jnp.exp2(ms[g] - m_new)
            ls[g] = alpha * ls[g] + jnp.sum(p, axis=-1, keepdims=True)
            ms[g] = m_new
            rows = slice(g * R, (g + 1) * R)
            acc_scr[rows] = alpha.reshape(R, 1) * acc_scr[rows] + _dot(p.reshape(R, KC).astype(MXU_DT), vc_)
        return tuple(ms), tuple(ls)

    m0 = tuple(jnp.full((B_HPG, QB, 1), SOFTMAX_M0, f32) for _ in range(B_KV))
    l0 = tuple(jnp.zeros((B_HPG, QB, 1), f32) for _ in range(B_KV))
    carry = lax.fori_loop(0, nch // 2, lambda c2, cr: sel_body(2 * c2 + 1, sel_body(2 * c2, cr)), (m0, l0))
    _, l_fin = lax.cond(nch % 2 == 1, lambda cr: sel_body(nch - 1, cr), lambda cr: cr, carry)

    for g in range(B_KV):
        rows = slice(g * R, (g + 1) * R)
        qs = q_ref[0, g * B_HPG:(g + 1) * B_HPG].reshape(R, LANES)
        o_sel = acc_scr[rows] / l_fin[g].reshape(R, 1)
        o_cmp = ocmp_scr[rows]
        kw_ = kv_ref[0, 2 * B_KV + g, pl.ds(wstart, WL), :]
        vw_ = kv_ref[0, 3 * B_KV + g, pl.ds(wstart, WL), :]
        p_win = _masked_softmax2(_dot_nt(qs, kw_).reshape(B_HPG, QB, WL), win_bias)
        o_win = _dot(p_win.reshape(R, WL).astype(MXU_DT), vw_)
        for hd in range(B_HPG):
            c0 = 2 * A_HEADS + (g * B_HPG + hd) * 3
            rs = slice(hd * QB, (hd + 1) * QB)
            o = (gates[:, c0:c0 + 1] * o_cmp[rs] + gates[:, c0 + 1:c0 + 2] * o_sel[rs]
                 + gates[:, c0 + 2:c0 + 3] * o_win[rs])
            col = (g * B_HPG + hd) * LANES
            out_ref[0, :, col:col + LANES] = o.astype(out_ref.dtype)


def _nsa(bq, bs, cmp, small, g_b):
    B, _, T, _ = bq.shape
    QB = min(Q_BLOCK, T)
    KC = min(KEY_CHUNK, T)
    WL = min(B_WIN + QB, T)
    NSB = T // B_SEL_BLK
    NSEL = min(B_SEL_N, NSB)
    ncmp = cmp.shape[2]
    f32 = jnp.float32
    M = (T - B_CMP_LEN) // B_CMP_STRIDE + 1
    assert NSB % SUBLANES == 0 and NSB <= LANES
    mi, jj = np.arange(ncmp)[None, :], np.arange(NSB)[:, None]
    ovt = ((mi * B_CMP_STRIDE < (jj + 1) * B_SEL_BLK) & (mi * B_CMP_STRIDE + B_CMP_LEN > jj * B_SEL_BLK)
           & (mi < M)).astype(np.float32)
    kk = np.arange(T)
    expand = (kk[None, :] // B_SEL_BLK == np.arange(LANES)[:, None]).astype(np.float32)
    expand = jnp.asarray(expand.reshape(LANES, T // KC, KC).transpose(1, 0, 2), MXU_DT)
    gb = jnp.zeros((1, LANES), f32).at[0, 2 * A_HEADS:2 * A_HEADS + 3 * B_HEADS].set(g_b)
    kern = functools.partial(_nsa_kernel, QB=QB, KC=KC, WL=WL, NSB=NSB, NSEL=NSEL)
    return pl.pallas_call(
        kern,
        grid=(B, T // QB),
        in_specs=[pl.BlockSpec((1, B_HEADS, QB, LANES), lambda b, i: (b, 0, i, 0)),
                  pl.BlockSpec((1, 4 * B_KV, T, LANES), lambda b, i: (b, 0, 0, 0)),
                  pl.BlockSpec((1, 2 * B_KV, ncmp, LANES), lambda b, i: (b, 0, 0, 0)),
                  pl.BlockSpec((1, QB, LANES), lambda b, i: (b, i, 0)),
                  _const_spec((1, LANES)), _const_spec(ovt.shape), _const_spec(expand.shape)],
        out_specs=pl.BlockSpec((1, QB, B_HEADS * LANES), lambda b, i: (b, i, 0)),
        out_shape=jax.ShapeDtypeStruct((B, T, B_HEADS * LANES), MXU_DT),
        scratch_shapes=[pltpu.VMEM((B_HEADS * QB, LANES), f32), pltpu.VMEM((B_HEADS * QB, LANES), f32)],
        compiler_params=_cparams("parallel", "arbitrary"),
    )(bq, bs, cmp, small, gb, jnp.asarray(ovt), expand)


def _rope_rows(d):
    inv = ROPE_BASE ** (-jnp.arange(0, d, 2, dtype=jnp.float32) / d)
    z = jnp.zeros((C_DR - d,), jnp.float32)
    inv64 = jnp.concatenate([inv, inv, z])
    sgn64 = jnp.concatenate([-jnp.ones(d // 2), jnp.ones(d // 2), z]).astype(jnp.float32)
    return inv64, sgn64


def _odd_prep_kernel(h_ref, pos_ref, win_ref, wqb_ref, wiq_ref, wuk_ref, qn_ref, kvn_ref, ikg_ref, ikb_ref,
                     rope_ref, perm_ref, qa_ref, kh_ref, kv_ref, qi_ref, ki_ref, wi_ref):
    f32 = jnp.float32
    z = _dot(h_ref[0].astype(MXU_DT), win_ref[...])
    pos = pos_ref[0].astype(f32)
    rr = rope_ref[...]
    ang_q = pos * rr[0:1, :]
    cos_q, sin_q = jnp.cos(ang_q), jnp.sin(ang_q) * rr[1:2, :]
    ang_i = pos * rr[2:3, :]
    cos_i, sin_i = jnp.cos(ang_i), jnp.sin(ang_i) * rr[3:4, :]

    def rms(x, g):
        return x * lax.rsqrt(jnp.mean(jnp.square(x), axis=-1, keepdims=True) + LN_EPS) * g

    cq = rms(z[:, 0:C_QL], qn_ref[...])
    ckv = rms(z[:, C_QL:C_QL + C_KVL], kvn_ref[...])
    o = C_QL + C_KVL
    k_rope = (z[:, o:o + LANES] * cos_q + z[:, o + LANES:o + 2 * LANES] * sin_q).astype(kh_ref.dtype)
    ckvb = ckv.astype(kv_ref.dtype)
    kv_ref[0] = ckvb
    for hd in range(C_HEADS):
        kh_ref[0, hd, :, 0:C_DN] = _dot(ckvb, wuk_ref[hd]).astype(kh_ref.dtype)
        kh_ref[0, hd, :, C_DN:C_KDIM] = k_rope
    ik = z[:, o + 2 * LANES:o + 3 * LANES]
    real = _iota((1, LANES), 1) < C_IDX_DH
    mu = jnp.sum(ik, axis=-1, keepdims=True) / C_IDX_DH
    dlt = jnp.where(real, ik - mu, 0.0)
    var = jnp.sum(jnp.square(dlt), axis=-1, keepdims=True) / C_IDX_DH
    ki = dlt * lax.rsqrt(var + LN_EPS) * ikg_ref[...] + ikb_ref[...]
    ki = ki * cos_i + _dot_f32(ki, perm_ref[...]) * sin_i
    ki_ref[0] = ki.astype(ki_ref.dtype)
    wi_ref[0] = z[:, o + 3 * LANES:o + 4 * LANES] * (C_IDX_HEADS ** -0.5 * C_IDX_DH ** -0.5)
    cqb = cq.astype(MXU_DT)
    qf = _dot(cqb, wqb_ref[...])
    qi = _dot(cqb, wiq_ref[...])
    scale = (C_DN + C_DR) ** -0.5 * LOG2E
    nh = C_HEADS * LANES
    for hd in range(C_HEADS):
        cs = slice(hd * LANES, (hd + 1) * LANES)
        q_rope = qf[:, nh:2 * nh][:, cs] * cos_q + qf[:, 2 * nh:3 * nh][:, cs] * sin_q
        qa_ref[0, hd, :, 0:C_DN] = (qf[:, hd * C_DN:(hd + 1) * C_DN] * scale).astype(qa_ref.dtype)
        qa_ref[0, hd, :, C_DN:C_KDIM] = (q_rope * scale).astype(qa_ref.dtype)
        qi_h = qi[:, 0:nh][:, cs] * cos_i + qi[:, nh:2 * nh][:, cs] * sin_i
        qi_ref[0, hd] = qi_h.astype(qi_ref.dtype)


def _rot_cols(w, n_heads, dh, d):
    w = w.reshape(w.shape[0], n_heads, dh)
    h = d // 2
    return jnp.concatenate([w[..., h:d], w[..., 0:h], jnp.zeros_like(w[..., d:])], -1).reshape(w.shape[0], n_heads * dh)


def _odd_prep(h3, pos3, w_in, q_norm, kv_norm, w_qb, w_uk, w_iq, ik_g, ik_b, tm):
    B, T, D = h3.shape
    f32 = jnp.float32
    o = 0
    parts = []
    for s in (C_QL, C_KVL, C_DR, C_IDX_DH, C_IDX_HEADS):
        parts.append(w_in[:, o:o + s]); o += s
    w_cq, w_ckv, w_kr, w_ik, w_iw = parts
    pc = lambda w: _pad_cols(w, LANES)
    win = jnp.concatenate([w_cq, w_ckv, pc(w_kr), pc(_rot_cols(w_kr, 1, C_DR, C_DR)), pc(w_ik), pc(w_iw)],
                          -1).astype(MXU_DT)
    wq = w_qb.reshape(C_QL, C_HEADS, C_DN + C_DR)
    w_nope = wq[..., :C_DN].reshape(C_QL, C_HEADS * C_DN)
    w_rope = wq[..., C_DN:].reshape(C_QL, C_HEADS * C_DR)
    wqb = jnp.concatenate([w_nope, _pad_heads(w_rope, C_HEADS, C_DR),
                           _pad_heads(_rot_cols(w_rope, C_HEADS, C_DR, C_DR), C_HEADS, C_DR)], -1).astype(MXU_DT)
    wiq = jnp.concatenate([_pad_heads(w_iq, C_IDX_HEADS, C_IDX_DH),
                           _pad_heads(_rot_cols(w_iq, C_IDX_HEADS, C_IDX_DH, C_IDX_DR), C_IDX_HEADS, C_IDX_DH)],
                          -1).astype(MXU_DT)
    wuk = w_uk.transpose(1, 0, 2).astype(MXU_DT)
    inv_q, sgn_q = _rope_rows(C_DR)
    inv_i, sgn_i = _rope_rows(C_IDX_DR)
    rope = jnp.stack([jnp.tile(v, LANES // C_DR) for v in (inv_q, sgn_q, inv_i, sgn_i)])
    rope = jnp.concatenate([rope, jnp.zeros((SUBLANES - 4, LANES), f32)])
    hh = C_IDX_DR // 2
    src = np.arange(LANES)
    src[:hh] += hh
    src[hh:C_IDX_DR] -= hh
    perm = np.zeros((LANES, LANES), np.float32)
    perm[src, np.arange(LANES)] = 1.0
    ikg = _pad_cols(ik_g[None, :], LANES)
    ikb = _pad_cols(ik_b[None, :], LANES)
    out_shape = (
        jax.ShapeDtypeStruct((B, C_HEADS, T, C_KDIM), MXU_DT),
        jax.ShapeDtypeStruct((B, C_HEADS, T, C_KDIM), MXU_DT),
        jax.ShapeDtypeStruct((B, T, C_KVL), MXU_DT),
        jax.ShapeDtypeStruct((B, C_IDX_HEADS, T, LANES), MXU_DT),
        jax.ShapeDtypeStruct((B, T, LANES), MXU_DT),
        jax.ShapeDtypeStruct((B, T, LANES), f32),
    )
    row = lambda w: pl.BlockSpec((1, tm, w), lambda b, i: (b, i, 0))
    hm = lambda w: pl.BlockSpec((1, C_HEADS, tm, w), lambda b, i: (b, 0, i, 0))
    return pl.pallas_call(
        _odd_prep_kernel,
        grid=(B, T // tm),
        in_specs=[row(D), row(1), _const_spec(win.shape), _const_spec(wqb.shape), _const_spec(wiq.shape),
                  _const_spec(wuk.shape), _const_spec((1, C_QL)), _const_spec((1, C_KVL)),
                  _const_spec((1, LANES)), _const_spec((1, LANES)), _const_spec(rope.shape),
                  _const_spec(perm.shape)],
        out_specs=(hm(C_KDIM), hm(C_KDIM), row(C_KVL), hm(LANES), row(LANES), row(LANES)),
        out_shape=out_shape,
        compiler_params=_cparams("parallel", "parallel"),
    )(h3, pos3, win, wqb, wiq, wuk, q_norm[None, :], kv_norm[None, :], ikg, ikb, rope, jnp.asarray(perm))


def _dsa_kernel(qa_ref, qi_ref, wi_ref, kh_ref, kv_ref, ki_ref, wuv_ref, eye_ref, tri_ref, out_ref,
                key_scr, hi_scr, lo_scr, acc_scr, *, QB, KC, SUB, TOPK, HG):
    f32, i32, i16 = jnp.float32, jnp.int32, jnp.int16
    i = pl.program_id(1)
    s0 = i * QB
    H = C_HEADS
    NG = H // HG
    RG = HG * QB
    nch = (s0 + QB + KC - 1) // KC
    PK = PACKED_ROWS
    tq_row = s0 + _iota((1, QB), 1)
    w_t = wi_ref[0].T
    one, zero = jnp.ones((), MXU_DT), jnp.zeros((), MXU_DT)

    def idx_body(c, _):
        ks = pl.multiple_of(c * KC, KC)
        kic = ki_ref[0, pl.ds(ks, KC), :]
        isc = None
        for hd in range(C_IDX_HEADS):
            s = jnp.maximum(_dot_nt(kic, qi_ref[0, hd]), 0.0) * w_t[hd:hd + 1, :]
            isc = s if isc is None else isc + s
        isc = jnp.where(isc == 0.0, 0.0, isc)
        kpos = ks + _iota((KC, 1), 0)
        isc = jnp.where(kpos <= tq_row, isc, NEG)
        bits = lax.bitcast_convert_type(isc, i32)
        key = jnp.where(bits < 0, bits ^ jnp.int32(0x7FFFFFFF), bits)
        key_scr[c] = key
        k3 = key.reshape(KC // PK, PK, QB)
        hi_scr[c] = lax.shift_right_arithmetic(k3, 16).astype(i16)
        lo_scr[c] = ((k3 & 0xFFFF) - 32768).astype(i16)
        return 0

    lax.fori_loop(0, nch // 2, lambda c2, z: idx_body(2 * c2 + 1, idx_body(2 * c2, z)), 0)
    lax.cond(nch % 2 == 1, lambda z: idx_body(nch - 1, z), lambda z: z, 0)

    def rep16(v):
        return jnp.broadcast_to(v, (PK, QB)).astype(i16)[None]

    def count16(pred):
        def body(c, accs):
            x = jnp.where(pred(hi_scr[c], lo_scr[c]), one, zero)
            accs = list(accs)
            for r in range(KC // PK):
                accs[r % len(accs)] = accs[r % len(accs)] + x[r]
            return tuple(accs)
        accs = lax.fori_loop(0, nch, body, tuple(jnp.zeros((PK, QB), MXU_DT) for _ in range(4)))
        acc = (accs[0] + accs[1]) + (accs[2] + accs[3])
        return jnp.sum(acc.astype(f32), axis=0, keepdims=True)

    def bisect(pick, base):
        def body(b, t):
            cand = t + lax.shift_left(jnp.int32(1), 15 - b)
            c16 = rep16(cand)
            return jnp.where(base + count16(lambda h, l: pick(h, l) >= c16) >= TOPK, cand, t)
        return lax.fori_loop(0, 16, body, jnp.full((1, QB), -32768, i32))

    thi = bisect(lambda h, l: h, 0.0)
    thi16 = rep16(thi)

    def bucket_body(c, _):
        lo_scr[c] = jnp.where(hi_scr[c] == thi16, lo_scr[c], jnp.full((), -32768, i16))
        return 0

    lax.fori_loop(0, nch, bucket_body, 0)
    n_hi = count16(lambda h, l: h > thi16)
    tlo = bisect(lambda h, l: l, n_hi)
    tlo16 = rep16(tlo)
    thr = thi * 65536 + (tlo + 32768)
    room = TOPK - (n_hi + count16(lambda h, l: l > tlo16))
    n_eq = count16(lambda h, l: (h == thi16) & (l == tlo16))

    @pl.when(jnp.max(jnp.where(n_eq > room, 1, 0)) > 0)
    def _():
        def body(c, before):
            key = key_scr[c]
            eq = key == thr
            seen = before + _dot(tri_ref[...], jnp.where(eq, 1.0, 0.0).astype(MXU_DT))
            key_scr[c] = jnp.where(eq & (seen > room), key - 1, key)
            return seen[KC - 1:KC, :]
        lax.fori_loop(0, nch, body, jnp.zeros((1, QB), f32))

    acc_scr[...] = jnp.zeros(acc_scr.shape, f32)

    def att_body(c, carry):
        ms, ls = list(carry[0]), list(carry[1])
        ks = pl.multiple_of(c * KC, KC)
        key = key_scr[c]
        kpos = ks + _iota((KC, 1), 0)
        sel_t = (key >= thr) & (kpos <= tq_row)
        keep = _dot_nt(eye_ref[...], jnp.where(sel_t, 1.0, 0.0).astype(MXU_DT))
        bias = jnp.where(keep > 0.5, 0.0, NEG)
        for u in range(KC // SUB):
            sub = pl.ds(pl.multiple_of(ks + u * SUB, SUB), SUB)
            kvc = kv_ref[0, sub, :]
            b_u = bias[:, u * SUB:(u + 1) * SUB][None]
            for g in range(NG):
                s = jnp.stack([_dot_nt(qa_ref[0, hd], kh_ref[0, hd, sub, :])
                               for hd in range(g * HG, (g + 1) * HG)]) + b_u
                m_new = jnp.maximum(ms[g], jnp.max(s, axis=-1, keepdims=True))
                p = jnp.exp2(s - m_new)
                alpha = jnp.exp2(ms[g] - m_new)
                ls[g] = alpha * ls[g] + jnp.sum(p, axis=-1, keepdims=True)
                ms[g] = m_new
                rows = slice(g * RG, (g + 1) * RG)
                pv = _dot(p.reshape(RG, SUB).astype(MXU_DT), kvc)
                acc_scr[rows] = alpha.reshape(RG, 1) * acc_scr[rows] + pv
        return tuple(ms), tuple(ls)

    m0 = tuple(jnp.full((HG, QB, 1), SOFTMAX_M0, f32) for _ in range(NG))
    l0 = tuple(jnp.zeros((HG, QB, 1), f32) for _ in range(NG))
    carry = lax.fori_loop(0, nch // 2, lambda c2, cr: att_body(2 * c2 + 1, att_body(2 * c2, cr)), (m0, l0))
    _, l_fin = lax.cond(nch % 2 == 1, lambda cr: att_body(nch - 1, cr), lambda cr: cr, carry)
    for g in range(NG):
        o_lat = (acc_scr[g * RG:(g + 1) * RG] / l_fin[g].reshape(RG, 1)).astype(MXU_DT)
        for k in range(HG):
            hd = g * HG + k
            out_ref[0, :, hd * C_DV:(hd + 1) * C_DV] = _dot(o_lat[k * QB:(k + 1) * QB], wuv_ref[hd]).astype(out_ref.dtype)


def _dsa(qa, kh, kv, qi, ki, wi, w_uv):
    B, H, T, _ = qa.shape
    QB = min(DSA_Q_BLOCK, T)
    KC = min(KEY_CHUNK, T)
    topk = min(C_TOPK, T // 4)
    wuv = w_uv.transpose(1, 0, 2).astype(MXU_DT)
    assert T // PACKED_ROWS <= 256
    eye = jnp.eye(QB, dtype=MXU_DT)
    tri = jnp.tril(jnp.ones((KC, KC), MXU_DT))
    kern = functools.partial(_dsa_kernel, QB=QB, KC=KC, SUB=min(KEY_SUB, KC), TOPK=topk, HG=DSA_HEAD_GROUP)
    half_words = pltpu.VMEM((T // KC, KC // PACKED_ROWS, PACKED_ROWS, QB), jnp.int16)
    return pl.pallas_call(
        kern,
        grid=(B, T // QB),
        in_specs=[pl.BlockSpec((1, H, QB, C_KDIM), lambda b, i: (b, 0, i, 0)),
                  pl.BlockSpec((1, H, QB, LANES), lambda b, i: (b, 0, i, 0)),
                  pl.BlockSpec((1, QB, LANES), lambda b, i: (b, i, 0)),
                  pl.BlockSpec((1, H, T, C_KDIM), lambda b, i: (b, 0, 0, 0), pipeline_mode=pl.Buffered(1)),
                  pl.BlockSpec((1, T, C_KVL), lambda b, i: (b, 0, 0)),
                  pl.BlockSpec((1, T, LANES), lambda b, i: (b, 0, 0)),
                  _const_spec(wuv.shape), _const_spec(eye.shape), _const_spec(tri.shape)],
        out_specs=pl.BlockSpec((1, QB, H * C_DV), lambda b, i: (b, i, 0)),
        out_shape=jax.ShapeDtypeStruct((B, T, H * C_DV), MXU_DT),
        scratch_shapes=[pltpu.VMEM((T // KC, KC, QB), jnp.int32),
                        half_words, half_words,
                        pltpu.VMEM((H * QB, C_KVL), jnp.float32)],
        compiler_params=_cparams("parallel", "arbitrary"),
    )(qa, qi, wi, kh, kv, ki, wuv, eye, tri)


def _post_kernel(*refs, n_mix, n_ff):
    h_ref = refs[0]
    mix = refs[1:1 + 2 * n_mix]
    g1, b1, w1_ref, w2_ref, g2, b2, wg_ref, p_ref, wp_ref, out_ref = refs[1 + 2 * n_mix:]
    h = h_ref[...]
    y = _dot(mix[0][...].astype(MXU_DT), mix[1][...])
    for k in range(1, n_mix):
        y = y + _dot(mix[2 * k][...].astype(MXU_DT), mix[2 * k + 1][...])
    h1 = _layer_norm(DN_ALPHA * h + y, g1[...], b1[...])
    h1b = h1.astype(MXU_DT)
    ff = D_FF // n_ff
    u = None
    for k in range(n_ff):
        a = jnp.square(jnp.maximum(_dot(h1b, w1_ref[:, k * ff:(k + 1) * ff]), 0.0))
        t = _dot(a.astype(MXU_DT), w2_ref[k * ff:(k + 1) * ff, :])
        u = t if u is None else u + t
    h2 = _layer_norm(DN_ALPHA * h1 + u, g2[...], b2[...])
    gate = jax.nn.sigmoid(_dot(h2.astype(MXU_DT), wg_ref[...]))
    out_ref[...] = h2 + gate * _dot(p_ref[...].astype(MXU_DT), wp_ref[...])


def _post(h2d, mixes, ln1_g, ln1_b, w1, w2, ln2_g, ln2_b, wg, p2d, wp, tm):
    M, D = h2d.shape
    row = lambda w: pl.BlockSpec((tm, w), lambda i: (i, 0))
    vec = lambda v: v[None, :]
    in_specs = [row(D)]
    args = [h2d]
    for x, w in mixes:
        in_specs += [row(x.shape[1]), _const_spec(w.shape)]
        args += [x, w]
    in_specs += [_const_spec((1, D)), _const_spec((1, D)), _const_spec(w1.shape), _const_spec(w2.shape),
                 _const_spec((1, D)), _const_spec((1, D)), _const_spec(wg.shape), row(D_PLE), _const_spec(wp.shape)]
    args += [vec(ln1_g), vec(ln1_b), w1, w2, vec(ln2_g), vec(ln2_b), wg, p2d, wp]
    kern = functools.partial(_post_kernel, n_mix=len(mixes), n_ff=4)
    return pl.pallas_call(
        kern,
        grid=(M // tm,),
        in_specs=in_specs,
        out_specs=row(D),
        out_shape=jax.ShapeDtypeStruct((M, D), jnp.float32),
        compiler_params=_cparams("parallel"),
    )(*args)


def kernel(x, p, positions, e_w_in, e_a_conv, e_a_i_b, e_a_f_b, e_a_norm, e_b_cmp_pos, e_b_cmp_w1, e_b_cmp_w2, e_b_g_b, e_w_out, o_w_in, o_q_norm, o_kv_norm, o_w_qb, o_w_uk, o_w_uv, o_w_iq, o_ik_g, o_ik_b, o_w_out, ln1_g, ln1_b, ln2_g, ln2_b, mlp_w1, mlp_w2, ple_gate_w, ple_w):
    B, T, D = x.shape
    M = B * T
    tm = min(ROW_TILE, T)
    h = x
    pos3 = positions[..., None]
    bf = lambda w: w.astype(MXU_DT)
    for i in range(DEPTH):
        j = i // 2
        if i % 2 == 0:
            qk, av, ao, small, bq, bc, bs = _even_proj(h, _even_w_in_aug(e_w_in[j]), tm)
            ya = _mlstm(qk, av, ao, small, e_a_conv[j], e_a_i_b[j], e_a_f_b[j], e_a_norm[j])
            cmp = _nsa_compress(bc, e_b_cmp_pos[j], e_b_cmp_w1[j], e_b_cmp_w2[j])
            yb = _nsa(bq, bs, cmp, small, e_b_g_b[j])
            w_out = e_w_out[j]
            mixes = [(ya.reshape(M, A_W), bf(w_out[:A_W])),
                     (yb.reshape(M, B_HEADS * LANES), bf(_pad_heads(w_out[A_W:], B_HEADS, B_DH, axis=0)))]
        else:
            qa, kh, kv, qi, ki, wi = _odd_prep(h, pos3, o_w_in[j], o_q_norm[j], o_kv_norm[j], o_w_qb[j], o_w_uk[j],
                                               o_w_iq[j], o_ik_g[j], o_ik_b[j], tm)
            o = _dsa(qa, kh, kv, qi, ki, wi, o_w_uv[j])
            mixes = [(o.reshape(M, C_HEADS * C_DV), bf(o_w_out[j]))]
        h = _post(h.reshape(M, D), mixes, ln1_g[i], ln1_b[i], bf(mlp_w1[i]), bf(mlp_w2[i]), ln2_g[i], ln2_b[i],
                  bf(ple_gate_w[i]), p[i].reshape(M, D_PLE), bf(ple_w[i]), min(MLP_ROW_TILE, T)).reshape(B, T, D)
    return h
```

```python
import functools

import numpy as np
import jax
import jax.numpy as jnp
from jax import lax
from jax.experimental import pallas as pl
from jax.experimental.pallas import tpu as pltpu

D_MODEL = 1024
DEPTH = 4
D_PLE = 256
D_FF = 4 * D_MODEL
DN_ALPHA = (2.0 * DEPTH) ** 0.25
LN_EPS = 1e-5
NEG = -1e30

A_HEADS = 4
A_DH = D_MODEL // 8
A_W = A_HEADS * A_DH
A_CONV = 4

B_HEADS = 8
B_DH = 64
B_KV = 2
B_HPG = B_HEADS // B_KV
B_CMP_LEN = 32
B_CMP_STRIDE = 16
B_CMP_HID = 128
B_SEL_BLK = 64
B_SEL_N = 16
B_WIN = 512

C_HEADS = 8
C_DN = 128
C_DR = 64
C_DV = 128
C_QL = 512
C_KVL = 256
C_IDX_HEADS = 8
C_IDX_DH = 64
C_IDX_DR = 32
C_TOPK = 256
ROPE_BASE = 10000.0

LANES = 128
SUBLANES = 8
PACKED_ROWS = 16
VMEM_LIMIT_BYTES = 56 * 2**20
MXU_DT = jnp.bfloat16
INT_MIN = -2**31

MLSTM_CHUNK = 256
MLSTM_SEQS = 1
ROW_TILE = 512
MLP_ROW_TILE = 512
Q_BLOCK = 256
DSA_Q_BLOCK = 256
DSA_HEAD_GROUP = 4
KEY_CHUNK = 512
KEY_SUB = 512
C_KDIM = C_DN + LANES

LOG2E = 1.4426950408889634
SOFTMAX_M0 = 0.5 * NEG
SOFTMAX_TINY = 1e-30


def _cparams(*sem):
    return pltpu.CompilerParams(dimension_semantics=sem, vmem_limit_bytes=VMEM_LIMIT_BYTES)


def _const_spec(shape):
    nd = len(shape)
    return pl.BlockSpec(shape, lambda *_: (0,) * nd, pipeline_mode=pl.Buffered(1))


def _dot(a, b):
    return jnp.dot(a, b, preferred_element_type=jnp.float32)


def _dot_nt(a, b):
    return lax.dot_general(a, b, (((1,), (1,)), ((), ())), preferred_element_type=jnp.float32)


def _dot_tn(a, b):
    return lax.dot_general(a, b, (((0,), (0,)), ((), ())), preferred_element_type=jnp.float32)


def _dot_f32(a, b):
    return jnp.dot(a, b, preferred_element_type=jnp.float32, precision=lax.Precision.HIGHEST)


def _layer_norm(x, g, b):
    mu = jnp.mean(x, axis=-1, keepdims=True)
    var = jnp.mean(jnp.square(x - mu), axis=-1, keepdims=True)
    return (x - mu) * lax.rsqrt(var + LN_EPS) * g + b


def _masked_softmax2(s, bias):
    s = s + bias
    m = jnp.maximum(jnp.max(s, axis=-1, keepdims=True), SOFTMAX_M0)
    e = jnp.exp2(s - m)
    return e / jnp.maximum(jnp.sum(e, axis=-1, keepdims=True), SOFTMAX_TINY)


def _iota(shape, dim):
    return lax.broadcasted_iota(jnp.int32, shape, dim)


def _pad_heads(w, n_heads, dh, axis=-1):
    axis = axis % w.ndim
    shp = w.shape[:axis] + (n_heads, dh) + w.shape[axis + 1:]
    w = w.reshape(shp)
    pad = [(0, 0)] * w.ndim
    pad[axis + 1] = (0, LANES - dh)
    w = jnp.pad(w, pad)
    return w.reshape(shp[:axis] + (n_heads * LANES,) + shp[axis + 2:])


def _pad_cols(w, width):
    return jnp.pad(w, [(0, 0)] * (w.ndim - 1) + [(0, width - w.shape[-1])])


def _even_proj_kernel(h_ref, w_ref, blk_ref, qk_ref, av_ref, ao_ref, sm_ref, bq_ref, bc_ref, bs_ref):
    z = _dot(h_ref[0].astype(MXU_DT), w_ref[...])
    o = 0
    qk_ref[0] = z[:, o:o + 2 * A_W]; o += 2 * A_W
    av_ref[0] = z[:, o:o + A_W].astype(av_ref.dtype); o += A_W
    ao_ref[0] = z[:, o:o + A_W]; o += A_W
    sm_ref[0] = z[:, o:o + LANES]; o += LANES
    for hd in range(B_HEADS):
        bq_ref[0, hd] = z[:, o:o + LANES].astype(bq_ref.dtype); o += LANES
    for j in range(2 * B_KV):
        bc_ref[0, j] = z[:, o:o + LANES]; o += LANES
    for j in range(4 * B_KV):
        zj = z[:, o:o + LANES]; o += LANES
        if j < B_KV:
            zj = zj + blk_ref[...]
        bs_ref[0, j] = zj.astype(bs_ref.dtype)


def _even_proj(h3, w_aug, tm):
    B, T, D = h3.shape
    n = w_aug.shape[1]
    f32 = jnp.float32
    out_shape = (
        jax.ShapeDtypeStruct((B, T, 2 * A_W), f32),
        jax.ShapeDtypeStruct((B, T, A_W), MXU_DT),
        jax.ShapeDtypeStruct((B, T, A_W), f32),
        jax.ShapeDtypeStruct((B, T, LANES), f32),
        jax.ShapeDtypeStruct((B, B_HEADS, T, LANES), MXU_DT),
        jax.ShapeDtypeStruct((B, 2 * B_KV, T, LANES), f32),
        jax.ShapeDtypeStruct((B, 4 * B_KV, T, LANES), MXU_DT),
    )
    row = lambda w: pl.BlockSpec((1, tm, w), lambda b, i: (b, i, 0))
    hm = lambda nh: pl.BlockSpec((1, nh, tm, LANES), lambda b, i: (b, 0, i, 0))
    assert T // B_SEL_BLK <= LANES - B_DH
    blk = (np.arange(LANES)[None, :] == B_DH + np.arange(T)[:, None] // B_SEL_BLK).astype(np.float32)
    return pl.pallas_call(
        _even_proj_kernel,
        grid=(B, T // tm),
        in_specs=[row(D), _const_spec((D, n)), pl.BlockSpec((tm, LANES), lambda b, i: (i, 0))],
        out_specs=(row(2 * A_W), row(A_W), row(A_W), row(LANES), hm(B_HEADS), hm(2 * B_KV), hm(4 * B_KV)),
        out_shape=out_shape,
        compiler_params=_cparams("parallel", "parallel"),
    )(h3, w_aug, jnp.asarray(blk))


def _even_w_in_aug(w_in):
    sizes = (A_W, A_W, A_W, A_W, A_HEADS, A_HEADS, B_HEADS * B_DH) + (B_KV * B_DH,) * 6 + (3 * B_HEADS,)
    parts, o = [], 0
    for s in sizes:
        parts.append(w_in[:, o:o + s]); o += s
    aq, ak, av, ao, ai, af, bq, bkc, bvc, bks, bvs, bkw, bvw, bg = parts
    small = _pad_cols(jnp.concatenate([ai, af, bg], -1), LANES)
    ph = lambda w: _pad_heads(w, B_KV, B_DH)
    bq = bq * (B_DH ** -0.5 * LOG2E)
    cols = [aq, ak, av, ao, small, _pad_heads(bq, B_HEADS, B_DH),
            ph(bkc), ph(bvc), ph(bks), ph(bvs), ph(bkw), ph(bvw)]
    return jnp.concatenate(cols, -1).astype(MXU_DT)


def _mlstm_kernel(qk_ref, v_ref, o_ref, gc_ref, gr_ref, cw_ref, bc_ref, br_ref, ng_ref, tri_ref, out_ref,
                  xs_scr, ct_scr, n_scr, m_scr, *, L, NB):
    c = pl.program_id(1)
    f32 = jnp.float32
    W2 = 2 * A_W

    @pl.when(c == 0)
    def _():
        xs_scr[:, 0:SUBLANES, :] = jnp.zeros((NB, SUBLANES, W2), f32)
        ct_scr[...] = jnp.zeros(ct_scr.shape, f32)
        n_scr[...] = jnp.zeros(n_scr.shape, f32)
        m_scr[...] = jnp.zeros(m_scr.shape, f32)

    tri = tri_ref[...]
    causal = _iota((L, L), 1) <= _iota((L, L), 0)
    cw = cw_ref[...]
    for bb in range(NB):
        _mlstm_chunk(bb, qk_ref, v_ref, o_ref, gc_ref, gr_ref, cw, bc_ref, br_ref, ng_ref, tri, causal, out_ref,
                     xs_scr, ct_scr, n_scr, m_scr, L)


def _mlstm_chunk(bb, qk_ref, v_ref, o_ref, gc_ref, gr_ref, cw, bc_ref, br_ref, ng_ref, tri, causal, out_ref,
                 xs_scr, ct_scr, n_scr, m_scr, L):
    f32 = jnp.float32
    xs_scr[bb, SUBLANES:SUBLANES + L, :] = qk_ref[bb]
    base = SUBLANES - (A_CONV - 1)
    acc = xs_scr[bb, base:base + L, :] * cw[0:1, :]
    for j in range(1, A_CONV):
        acc = acc + xs_scr[bb, base + j:base + j + L, :] * cw[j:j + 1, :]
    qk = acc * jax.nn.sigmoid(acc)
    xs_scr[bb, 0:SUBLANES, :] = xs_scr[bb, L:L + SUBLANES, :]

    gc = gc_ref[bb] + bc_ref[...]
    gr = gr_ref[bb, 0] + br_ref[...]
    ig_c = gc[:, 0:A_HEADS]
    b_c = _dot_f32(tri, jax.nn.log_sigmoid(gc[:, A_HEADS:2 * A_HEADS]))
    ig_r = gr[0:A_HEADS, :]
    b_r = _dot_f32(jax.nn.log_sigmoid(gr[A_HEADS:2 * A_HEADS, :]), tri.T)

    for hd in range(A_HEADS):
        sl = slice(hd * A_DH, (hd + 1) * A_DH)
        q_h = (qk[:, sl] * A_DH ** -0.5).astype(MXU_DT)
        k_f = qk[:, A_W + hd * A_DH:A_W + (hd + 1) * A_DH]
        k_h = k_f.astype(MXU_DT)
        v_h = v_ref[bb][:, sl]
        bi = b_c[:, hd:hd + 1]
        ic = ig_c[:, hd:hd + 1]
        dmat = jnp.where(causal, bi - b_r[hd:hd + 1, :] + ig_r[hd:hd + 1, :], NEG)
        m_prev = m_scr[bb, hd][:, 0:1]
        m_inter = bi + m_prev
        m_t = jnp.maximum(m_inter, jnp.max(dmat, axis=1, keepdims=True))
        e_inter = jnp.exp(m_inter - m_t)
        s = _dot_nt(q_h, k_h) * jnp.exp(dmat - m_t)
        ct = ct_scr[bb, hd]
        nrow = n_scr[bb, hd]
        num = e_inter * _dot(q_h, ct.astype(MXU_DT)) + _dot(s.astype(MXU_DT), v_h)
        den = e_inter * jnp.sum(q_h.astype(f32) * nrow, axis=1, keepdims=True) + jnp.sum(s, axis=1, keepdims=True)
        hc = num / jnp.maximum(jnp.abs(den), jnp.exp(-m_t))
        b_last = bi[L - 1:L, :]
        dec = b_last - bi + ic
        m_new = jnp.maximum(b_last + m_prev, jnp.max(dec, axis=0, keepdims=True))
        wgt = jnp.exp(dec - m_new)
        e_st = jnp.exp(b_last + m_prev - m_new)
        kw = k_f * wgt
        ct_scr[bb, hd] = e_st * ct + _dot_tn(kw.astype(MXU_DT), v_h)
        n_scr[bb, hd] = e_st * nrow + jnp.sum(kw, axis=0, keepdims=True)
        m_scr[bb, hd] = jnp.broadcast_to(m_new, (1, LANES))
        mu = jnp.mean(hc, axis=1, keepdims=True)
        var = jnp.mean(jnp.square(hc - mu), axis=1, keepdims=True)
        hn = (hc - mu) * lax.rsqrt(var + LN_EPS) * ng_ref[:, sl]
        out_ref[bb, :, sl] = (hn * jax.nn.sigmoid(o_ref[bb][:, sl])).astype(out_ref.dtype)


def _mlstm(qk, av, ao, small, conv_w, i_b, f_b, norm_g):
    B, T, _ = qk.shape
    L = min(MLSTM_CHUNK, T)
    N = T // L
    f32 = jnp.float32
    gates = small[..., 0:2 * A_HEADS]
    gates_r = gates.reshape(B, N, L, 2 * A_HEADS).transpose(0, 1, 3, 2)
    bias = jnp.concatenate([i_b, f_b]).astype(f32)
    tri = jnp.tril(jnp.ones((L, L), f32))
    NB = MLSTM_SEQS if B % MLSTM_SEQS == 0 else 1
    row = lambda w: pl.BlockSpec((NB, L, w), lambda b, c: (b, c, 0))
    kern = functools.partial(_mlstm_kernel, L=L, NB=NB)
    return pl.pallas_call(
        kern,
        grid=(B // NB, N),
        in_specs=[row(2 * A_W), row(A_W), row(A_W), row(2 * A_HEADS),
                  pl.BlockSpec((NB, 1, 2 * A_HEADS, L), lambda b, c: (b, c, 0, 0)),
                  _const_spec((A_CONV, 2 * A_W)), _const_spec((1, 2 * A_HEADS)), _const_spec((2 * A_HEADS, 1)),
                  _const_spec((1, A_W)), _const_spec((L, L))],
        out_specs=row(A_W),
        out_shape=jax.ShapeDtypeStruct((B, T, A_W), MXU_DT),
        scratch_shapes=[pltpu.VMEM((NB, L + 2 * SUBLANES, 2 * A_W), f32),
                        pltpu.VMEM((NB, A_HEADS, A_DH, A_DH), f32),
                        pltpu.VMEM((NB, A_HEADS, 1, A_DH), f32),
                        pltpu.VMEM((NB, A_HEADS, 1, LANES), f32)],
        compiler_params=_cparams("parallel", "arbitrary"),
    )(qk, av, ao, gates, gates_r, conv_w, bias[None, :], bias[:, None], norm_g[None, :], tri)


def _nsa_cmp_kernel(x_ref, w1a_ref, w1b_ref, w2_ref, pos_ref, out_ref):
    n = x_ref.shape[2] // B_CMP_STRIDE
    half = B_CMP_STRIDE * LANES
    for j in range(2):
        bias = (_dot(pos_ref[j, :, 0:half].astype(MXU_DT), w1a_ref[j])
                + _dot(pos_ref[j, :, half:2 * half].astype(MXU_DT), w1b_ref[j]))[0:1, :]
        for g in range(B_KV):
            u = jnp.concatenate([x_ref[0, j * B_KV + g, pl.ds(r, n, stride=B_CMP_STRIDE), :]
                                 for r in range(B_CMP_STRIDE)], axis=1).astype(MXU_DT)
            a = _dot(u, w1a_ref[j])
            bm = _dot(u, w1b_ref[j])
            pre = a + pltpu.roll(bm, n - 1, 0) + bias
            hid = jax.nn.gelu(pre)
            out_ref[0, j * B_KV + g] = _dot(hid.astype(MXU_DT), w2_ref[j])


def _nsa_compress(bc, cmp_pos, cmp_w1, cmp_w2):
    B, _, T, _ = bc.shape
    nblk = T // B_CMP_STRIDE
    half = B_CMP_STRIDE * LANES
    w1 = jnp.pad(cmp_w1.reshape(2, B_CMP_LEN, B_DH, B_CMP_HID), ((0, 0), (0, 0), (0, LANES - B_DH), (0, 0)))
    w1 = w1.reshape(2, B_CMP_LEN * LANES, B_CMP_HID).astype(MXU_DT)
    w1a, w1b = w1[:, :half], w1[:, half:]
    w2 = _pad_cols(cmp_w2, LANES).astype(MXU_DT)
    pos = jnp.pad(cmp_pos, ((0, 0), (0, 0), (0, LANES - B_DH))).reshape(2, 1, B_CMP_LEN * LANES)
    pos = jnp.broadcast_to(pos, (2, SUBLANES, B_CMP_LEN * LANES))
    return pl.pallas_call(
        _nsa_cmp_kernel,
        grid=(B,),
        in_specs=[pl.BlockSpec((1, 2 * B_KV, T, LANES), lambda b: (b, 0, 0, 0)),
                  _const_spec(w1a.shape), _const_spec(w1b.shape), _const_spec(w2.shape), _const_spec(pos.shape)],
        out_specs=pl.BlockSpec((1, 2 * B_KV, nblk, LANES), lambda b: (b, 0, 0, 0)),
        out_shape=jax.ShapeDtypeStruct((B, 2 * B_KV, nblk, LANES), jnp.float32),
        compiler_params=_cparams("parallel"),
    )(bc, w1a, w1b, w2, pos)


def _nsa_kernel(q_ref, kv_ref, cmp_ref, sm_ref, gb_ref, ovt_ref, out_ref, acc_scr, ocmp_scr, qaug_scr,
                *, QB, KC, WL, NSB, NSEL):
    f32 = jnp.float32
    i = pl.program_id(1)
    s0 = i * QB
    R = B_HPG * QB
    tq_col = s0 + _iota((QB, 1), 0)
    tq_row = s0 + _iota((1, QB), 1)
    nch = (s0 + QB + KC - 1) // KC
    gates = jax.nn.sigmoid(sm_ref[0] + gb_ref[...])
    ncmp = cmp_ref.shape[2]
    cmp_end = _iota((1, ncmp), 1) * B_CMP_STRIDE + (B_CMP_LEN - 1)
    cmp_bias = jnp.where(cmp_end <= tq_col, 0.0, NEG)[None]
    jb = _iota((NSB, 1), 0)
    cur = lax.shift_right_logical(tq_row, int(np.log2(B_SEL_BLK)))
    forced = (jb == 0) | (jb == cur) | (jb == cur - 1)
    valid = jb * B_SEL_BLK <= tq_row
    wstart = pl.multiple_of(jnp.maximum(s0 + QB - WL, 0), QB)
    wpos = wstart + _iota((1, WL), 1)
    win_bias = jnp.where((wpos <= tq_col) & (wpos > tq_col - B_WIN), 0.0, NEG)[None]

    for g in range(B_KV):
        qs = q_ref[0, g * B_HPG:(g + 1) * B_HPG].reshape(R, LANES)
        kcm = cmp_ref[0, g].astype(MXU_DT)
        vcm = cmp_ref[0, B_KV + g].astype(MXU_DT)
        p_cmp = _masked_softmax2(_dot_nt(qs, kcm).reshape(B_HPG, QB, ncmp), cmp_bias)
        ocmp_scr[g * R:(g + 1) * R] = _dot(p_cmp.reshape(R, ncmp).astype(MXU_DT), vcm)
        psum = jnp.sum(p_cmp, axis=0)
        imp_t = lax.dot_general(ovt_ref[...], psum, (((1,), (1,)), ((), ())),
                                preferred_element_type=f32, precision=lax.Precision.HIGHEST)
        sc = jnp.where(forced, 1e6, imp_t)
        sc = jnp.where(valid, sc, NEG)
        rank = jnp.zeros((NSB, QB), f32)
        for j in range(NSB):
            cj = jnp.broadcast_to(sc[j:j + 1, :], (NSB, QB))
            ahead = (cj > sc) | ((cj == sc) & (jb > j))
            rank = rank + jnp.where(ahead, 1.0, 0.0)
        drop = jnp.where(rank < NSEL, 0.0, NEG)
        pad = [jnp.zeros((B_DH, QB), f32), drop]
        if B_DH + NSB < LANES:
            pad.append(jnp.zeros((LANES - B_DH - NSB, QB), f32))
        drop_t = jnp.concatenate(pad, axis=0).T.astype(MXU_DT)
        for hd in range(B_HPG):
            rows = slice((g * B_HPG + hd) * QB, (g * B_HPG + hd + 1) * QB)
            qaug_scr[rows] = q_ref[0, g * B_HPG + hd] + drop_t

    acc_scr[...] = jnp.zeros(acc_scr.shape, f32)

    def sel_body(c, carry, diagonal=False):
        ms, ls = list(carry[0]), list(carry[1])
        ks = pl.multiple_of(c * KC, KC)
        for g in range(B_KV):
            kc_ = kv_ref[0, 0 * B_KV + g, pl.ds(ks, KC), :]
            vc_ = kv_ref[0, 1 * B_KV + g, pl.ds(ks, KC), :]
            s3 = _dot_nt(qaug_scr[g * R:(g + 1) * R], kc_).reshape(B_HPG, QB, KC)
            if diagonal:
                s3 = s3 + jnp.where(ks + _iota((1, KC), 1) <= tq_col, 0.0, NEG)[None]
            m_new = jnp.maximum(ms[g], jnp.max(s3, axis=-1, keepdims=True))
            p = jnp.exp2(s3 - m_new)
            alpha = jnp.exp2(ms[g] - m_new)
            ls[g] = alpha * ls[g] + jnp.sum(p, axis=-1, keepdims=True)
            ms[g] = m_new
            rows = slice(g * R, (g + 1) * R)
            acc_scr[rows] = alpha.reshape(R, 1) * acc_scr[rows] + _dot(p.reshape(R, KC).astype(MXU_DT), vc_)
        return tuple(ms), tuple(ls)

    m0 = tuple(jnp.full((B_HPG, QB, 1), SOFTMAX_M0, f32) for _ in range(B_KV))
    l0 = tuple(jnp.zeros((B_HPG, QB, 1), f32) for _ in range(B_KV))
    nfull = nch - 1
    carry = lax.fori_loop(0, nfull // 2, lambda c2, cr: sel_body(2 * c2 + 1, sel_body(2 * c2, cr)), (m0, l0))
    carry = lax.cond(nfull % 2 == 1, lambda cr: sel_body(nfull - 1, cr), lambda cr: cr, carry)
    _, l_fin = sel_body(nfull, carry, diagonal=True)

    for g in range(B_KV):
        rows = slice(g * R, (g + 1) * R)
        qs = q_ref[0, g * B_HPG:(g + 1) * B_HPG].reshape(R, LANES)
        o_sel = acc_scr[rows] / l_fin[g].reshape(R, 1)
        o_cmp = ocmp_scr[rows]
        kw_ = kv_ref[0, 2 * B_KV + g, pl.ds(wstart, WL), :]
        vw_ = kv_ref[0, 3 * B_KV + g, pl.ds(wstart, WL), :]
        p_win = _masked_softmax2(_dot_nt(qs, kw_).reshape(B_HPG, QB, WL), win_bias)
        o_win = _dot(p_win.reshape(R, WL).astype(MXU_DT), vw_)
        for hd in range(B_HPG):
            c0 = 2 * A_HEADS + (g * B_HPG + hd) * 3
            rs = slice(hd * QB, (hd + 1) * QB)
            o = (gates[:, c0:c0 + 1] * o_cmp[rs] + gates[:, c0 + 1:c0 + 2] * o_sel[rs]
                 + gates[:, c0 + 2:c0 + 3] * o_win[rs])
            col = (g * B_HPG + hd) * LANES
            out_ref[0, :, col:col + LANES] = o.astype(out_ref.dtype)


def _nsa(bq, bs, cmp, small, g_b):
    B, _, T, _ = bq.shape
    QB = min(Q_BLOCK, T)
    KC = min(KEY_CHUNK, T)
    assert KC % QB == 0
    WL = min(B_WIN + QB, T)
    NSB = T // B_SEL_BLK
    NSEL = min(B_SEL_N, NSB)
    ncmp = cmp.shape[2]
    f32 = jnp.float32
    M = (T - B_CMP_LEN) // B_CMP_STRIDE + 1
    assert NSB % SUBLANES == 0 and NSB <= LANES
    mi, jj = np.arange(ncmp)[None, :], np.arange(NSB)[:, None]
    ovt = ((mi * B_CMP_STRIDE < (jj + 1) * B_SEL_BLK) & (mi * B_CMP_STRIDE + B_CMP_LEN > jj * B_SEL_BLK)
           & (mi < M)).astype(np.float32)
    gb = jnp.zeros((1, LANES), f32).at[0, 2 * A_HEADS:2 * A_HEADS + 3 * B_HEADS].set(g_b)
    kern = functools.partial(_nsa_kernel, QB=QB, KC=KC, WL=WL, NSB=NSB, NSEL=NSEL)
    return pl.pallas_call(
        kern,
        grid=(B, T // QB),
        in_specs=[pl.BlockSpec((1, B_HEADS, QB, LANES), lambda b, i: (b, 0, i, 0)),
                  pl.BlockSpec((1, 4 * B_KV, T, LANES), lambda b, i: (b, 0, 0, 0)),
                  pl.BlockSpec((1, 2 * B_KV, ncmp, LANES), lambda b, i: (b, 0, 0, 0)),
                  pl.BlockSpec((1, QB, LANES), lambda b, i: (b, i, 0)),
                  _const_spec((1, LANES)), _const_spec(ovt.shape)],
        out_specs=pl.BlockSpec((1, QB, B_HEADS * LANES), lambda b, i: (b, i, 0)),
        out_shape=jax.ShapeDtypeStruct((B, T, B_HEADS * LANES), MXU_DT),
        scratch_shapes=[pltpu.VMEM((B_HEADS * QB, LANES), f32), pltpu.VMEM((B_HEADS * QB, LANES), f32),
                        pltpu.VMEM((B_HEADS * QB, LANES), MXU_DT)],
        compiler_params=_cparams("parallel", "arbitrary"),
    )(bq, bs, cmp, small, gb, jnp.asarray(ovt))


def _rope_rows(d):
    inv = ROPE_BASE ** (-jnp.arange(0, d, 2, dtype=jnp.float32) / d)
    z = jnp.zeros((C_DR - d,), jnp.float32)
    inv64 = jnp.concatenate([inv, inv, z])
    sgn64 = jnp.concatenate([-jnp.ones(d // 2), jnp.ones(d // 2), z]).astype(jnp.float32)
    return inv64, sgn64


def _odd_prep_kernel(h_ref, pos_ref, win_ref, wqb_ref, wiq_ref, wuk_ref, qn_ref, kvn_ref, ikg_ref, ikb_ref,
                     rope_ref, perm_ref, qa_ref, kh_ref, kv_ref, qi_ref, ki_ref, wi_ref):
    f32 = jnp.float32
    z = _dot(h_ref[0].astype(MXU_DT), win_ref[...])
    pos = pos_ref[0].astype(f32)
    rr = rope_ref[...]
    ang_q = pos * rr[0:1, :]
    cos_q, sin_q = jnp.cos(ang_q), jnp.sin(ang_q) * rr[1:2, :]
    ang_i = pos * rr[2:3, :]
    cos_i, sin_i = jnp.cos(ang_i), jnp.sin(ang_i) * rr[3:4, :]

    def rms(x, g):
        return x * lax.rsqrt(jnp.mean(jnp.square(x), axis=-1, keepdims=True) + LN_EPS) * g

    cq = rms(z[:, 0:C_QL], qn_ref[...])
    ckv = rms(z[:, C_QL:C_QL + C_KVL], kvn_ref[...])
    o = C_QL + C_KVL
    k_rope = (z[:, o:o + LANES] * cos_q + z[:, o + LANES:o + 2 * LANES] * sin_q).astype(kh_ref.dtype)
    ckvb = ckv.astype(kv_ref.dtype)
    kv_ref[0] = ckvb
    for hd in range(C_HEADS):
        kh_ref[0, hd, :, 0:C_DN] = _dot(ckvb, wuk_ref[hd]).astype(kh_ref.dtype)
        kh_ref[0, hd, :, C_DN:C_KDIM] = k_rope
    ik = z[:, o + 2 * LANES:o + 3 * LANES]
    real = _iota((1, LANES), 1) < C_IDX_DH
    mu = jnp.sum(ik, axis=-1, keepdims=True) / C_IDX_DH
    dlt = jnp.where(real, ik - mu, 0.0)
    var = jnp.sum(jnp.square(dlt), axis=-1, keepdims=True) / C_IDX_DH
    ki = dlt * lax.rsqrt(var + LN_EPS) * ikg_ref[...] + ikb_ref[...]
    ki = ki * cos_i + _dot_f32(ki, perm_ref[...]) * sin_i
    ki_ref[0] = ki.astype(ki_ref.dtype)
    wi_ref[0] = z[:, o + 3 * LANES:o + 4 * LANES] * (C_IDX_HEADS ** -0.5 * C_IDX_DH ** -0.5)
    cqb = cq.astype(MXU_DT)
    qf = _dot(cqb, wqb_ref[...])
    qi = _dot(cqb, wiq_ref[...])
    scale = (C_DN + C_DR) ** -0.5 * LOG2E
    nh = C_HEADS * LANES
    for hd in range(C_HEADS):
        cs = slice(hd * LANES, (hd + 1) * LANES)
        q_rope = qf[:, nh:2 * nh][:, cs] * cos_q + qf[:, 2 * nh:3 * nh][:, cs] * sin_q
        qa_ref[0, hd, :, 0:C_DN] = (qf[:, hd * C_DN:(hd + 1) * C_DN] * scale).astype(qa_ref.dtype)
        qa_ref[0, hd, :, C_DN:C_KDIM] = (q_rope * scale).astype(qa_ref.dtype)
        qi_h = qi[:, 0:nh][:, cs] * cos_i + qi[:, nh:2 * nh][:, cs] * sin_i
        qi_ref[0, hd] = qi_h.astype(qi_ref.dtype)


def _rot_cols(w, n_heads, dh, d):
    w = w.reshape(w.shape[0], n_heads, dh)
    h = d // 2
    return jnp.concatenate([w[..., h:d], w[..., 0:h], jnp.zeros_like(w[..., d:])], -1).reshape(w.shape[0], n_heads * dh)


def _odd_prep(h3, pos3, w_in, q_norm, kv_norm, w_qb, w_uk, w_iq, ik_g, ik_b, tm):
    B, T, D = h3.shape
    f32 = jnp.float32
    o = 0
    parts = []
    for s in (C_QL, C_KVL, C_DR, C_IDX_DH, C_IDX_HEADS):
        parts.append(w_in[:, o:o + s]); o += s
    w_cq, w_ckv, w_kr, w_ik, w_iw = parts
    pc = lambda w: _pad_cols(w, LANES)
    win = jnp.concatenate([w_cq, w_ckv, pc(w_kr), pc(_rot_cols(w_kr, 1, C_DR, C_DR)), pc(w_ik), pc(w_iw)],
                          -1).astype(MXU_DT)
    wq = w_qb.reshape(C_QL, C_HEADS, C_DN + C_DR)
    w_nope = wq[..., :C_DN].reshape(C_QL, C_HEADS * C_DN)
    w_rope = wq[..., C_DN:].reshape(C_QL, C_HEADS * C_DR)
    wqb = jnp.concatenate([w_nope, _pad_heads(w_rope, C_HEADS, C_DR),
                           _pad_heads(_rot_cols(w_rope, C_HEADS, C_DR, C_DR), C_HEADS, C_DR)], -1).astype(MXU_DT)
    wiq = jnp.concatenate([_pad_heads(w_iq, C_IDX_HEADS, C_IDX_DH),
                           _pad_heads(_rot_cols(w_iq, C_IDX_HEADS, C_IDX_DH, C_IDX_DR), C_IDX_HEADS, C_IDX_DH)],
                          -1).astype(MXU_DT)
    wuk = w_uk.transpose(1, 0, 2).astype(MXU_DT)
    inv_q, sgn_q = _rope_rows(C_DR)
    inv_i, sgn_i = _rope_rows(C_IDX_DR)
    rope = jnp.stack([jnp.tile(v, LANES // C_DR) for v in (inv_q, sgn_q, inv_i, sgn_i)])
    rope = jnp.concatenate([rope, jnp.zeros((SUBLANES - 4, LANES), f32)])
    hh = C_IDX_DR // 2
    src = np.arange(LANES)
    src[:hh] += hh
    src[hh:C_IDX_DR] -= hh
    perm = np.zeros((LANES, LANES), np.float32)
    perm[src, np.arange(LANES)] = 1.0
    ikg = _pad_cols(ik_g[None, :], LANES)
    ikb = _pad_cols(ik_b[None, :], LANES)
    out_shape = (
        jax.ShapeDtypeStruct((B, C_HEADS, T, C_KDIM), MXU_DT),
        jax.ShapeDtypeStruct((B, C_HEADS, T, C_KDIM), MXU_DT),
        jax.ShapeDtypeStruct((B, T, C_KVL), MXU_DT),
        jax.ShapeDtypeStruct((B, C_IDX_HEADS, T, LANES), MXU_DT),
        jax.ShapeDtypeStruct((B, T, LANES), MXU_DT),
        jax.ShapeDtypeStruct((B, T, LANES), f32),
    )
    row = lambda w: pl.BlockSpec((1, tm, w), lambda b, i: (b, i, 0))
    hm = lambda w: pl.BlockSpec((1, C_HEADS, tm, w), lambda b, i: (b, 0, i, 0))
    return pl.pallas_call(
        _odd_prep_kernel,
        grid=(B, T // tm),
        in_specs=[row(D), row(1), _const_spec(win.shape), _const_spec(wqb.shape), _const_spec(wiq.shape),
                  _const_spec(wuk.shape), _const_spec((1, C_QL)), _const_spec((1, C_KVL)),
                  _const_spec((1, LANES)), _const_spec((1, LANES)), _const_spec(rope.shape),
                  _const_spec(perm.shape)],
        out_specs=(hm(C_KDIM), hm(C_KDIM), row(C_KVL), hm(LANES), row(LANES), row(LANES)),
        out_shape=out_shape,
        compiler_params=_cparams("parallel", "parallel"),
    )(h3, pos3, win, wqb, wiq, wuk, q_norm[None, :], kv_norm[None, :], ikg, ikb, rope, jnp.asarray(perm))


def _dsa_kernel(qa_ref, qi_ref, wi_ref, kh_ref, kv_ref, ki_ref, wuv_ref, eye_ref, tri_ref, out_ref,
                key_scr, hi_scr, lo_scr, acc_scr, *, QB, KC, SUB, TOPK, HG):
    f32, i32, i16 = jnp.float32, jnp.int32, jnp.int16
    i = pl.program_id(1)
    s0 = i * QB
    H = C_HEADS
    NG = H // HG
    RG = HG * QB
    nch = (s0 + QB + KC - 1) // KC
    PK = PACKED_ROWS
    tq_row = s0 + _iota((1, QB), 1)
    w_t = wi_ref[0].T
    one, zero = jnp.ones((), MXU_DT), jnp.zeros((), MXU_DT)

    def idx_body(c, _):
        ks = pl.multiple_of(c * KC, KC)
        kic = ki_ref[0, pl.ds(ks, KC), :]
        isc = None
        for hd in range(C_IDX_HEADS):
            s = jnp.maximum(_dot_nt(kic, qi_ref[0, hd]), 0.0) * w_t[hd:hd + 1, :]
            isc = s if isc is None else isc + s
        isc = jnp.where(isc == 0.0, 0.0, isc)
        kpos = ks + _iota((KC, 1), 0)
        isc = jnp.where(kpos <= tq_row, isc, NEG)
        bits = lax.bitcast_convert_type(isc, i32)
        key = jnp.where(bits < 0, bits ^ jnp.int32(0x7FFFFFFF), bits)
        key_scr[c] = key
        k3 = key.reshape(KC // PK, PK, QB)
        hi_scr[c] = lax.shift_right_arithmetic(k3, 16).astype(i16)
        lo_scr[c] = ((k3 & 0xFFFF) - 32768).astype(i16)
        return 0

    lax.fori_loop(0, nch // 2, lambda c2, z: idx_body(2 * c2 + 1, idx_body(2 * c2, z)), 0)
    lax.cond(nch % 2 == 1, lambda z: idx_body(nch - 1, z), lambda z: z, 0)

    def rep16(v):
        return jnp.broadcast_to(v, (PK, QB)).astype(i16)[None]

    def count16(pred):
        def body(c, accs):
            x = jnp.where(pred(hi_scr[c], lo_scr[c]), one, zero)
            accs = list(accs)
            for r in range(KC // PK):
                accs[r % len(accs)] = accs[r % len(accs)] + x[r]
            return tuple(accs)
        accs = lax.fori_loop(0, nch, body, tuple(jnp.zeros((PK, QB), MXU_DT) for _ in range(4)))
        acc = (accs[0] + accs[1]) + (accs[2] + accs[3])
        return jnp.sum(acc.astype(f32), axis=0, keepdims=True)

    def bisect(pick, base):
        def body(b, t):
            cand = t + lax.shift_left(jnp.int32(1), 15 - b)
            c16 = rep16(cand)
            return jnp.where(base + count16(lambda h, l: pick(h, l) >= c16) >= TOPK, cand, t)
        return lax.fori_loop(0, 16, body, jnp.full((1, QB), -32768, i32))

    thi = bisect(lambda h, l: h, 0.0)
    thi16 = rep16(thi)

    def bucket_body(c, _):
        lo_scr[c] = jnp.where(hi_scr[c] == thi16, lo_scr[c], jnp.full((), -32768, i16))
        return 0

    lax.fori_loop(0, nch, bucket_body, 0)
    n_hi = count16(lambda h, l: h > thi16)
    tlo = bisect(lambda h, l: l, n_hi)
    tlo16 = rep16(tlo)
    thr = thi * 65536 + (tlo + 32768)
    room = TOPK - (n_hi + count16(lambda h, l: l > tlo16))
    n_eq = count16(lambda h, l: (h == thi16) & (l == tlo16))

    @pl.when(jnp.max(jnp.where(n_eq > room, 1, 0)) > 0)
    def _():
        def body(c, before):
            key = key_scr[c]
            eq = key == thr
            seen = before + _dot(tri_ref[...], jnp.where(eq, 1.0, 0.0).astype(MXU_DT))
            key_scr[c] = jnp.where(eq & (seen > room), key - 1, key)
            return seen[KC - 1:KC, :]
        lax.fori_loop(0, nch, body, jnp.zeros((1, QB), f32))

    acc_scr[...] = jnp.zeros(acc_scr.shape, f32)

    def att_body(c, carry):
        ms, ls = list(carry[0]), list(carry[1])
        ks = pl.multiple_of(c * KC, KC)
        key = key_scr[c]
        kpos = ks + _iota((KC, 1), 0)
        sel_t = (key >= thr) & (kpos <= tq_row)
        keep = _dot_nt(eye_ref[...], jnp.where(sel_t, 1.0, 0.0).astype(MXU_DT))
        bias = jnp.where(keep > 0.5, 0.0, NEG)
        for u in range(KC // SUB):
            sub = pl.ds(pl.multiple_of(ks + u * SUB, SUB), SUB)
            kvc = kv_ref[0, sub, :]
            b_u = bias[:, u * SUB:(u + 1) * SUB][None]
            for g in range(NG):
                s = jnp.stack([_dot_nt(qa_ref[0, hd], kh_ref[0, hd, sub, :])
                               for hd in range(g * HG, (g + 1) * HG)]) + b_u
                m_new = jnp.maximum(ms[g], jnp.max(s, axis=-1, keepdims=True))
                p = jnp.exp2(s - m_new)
                alpha = jnp.exp2(ms[g] - m_new)
                ls[g] = alpha * ls[g] + jnp.sum(p, axis=-1, keepdims=True)
                ms[g] = m_new
                rows = slice(g * RG, (g + 1) * RG)
                pv = _dot(p.reshape(RG, SUB).astype(MXU_DT), kvc)
                acc_scr[rows] = alpha.reshape(RG, 1) * acc_scr[rows] + pv
        return tuple(ms), tuple(ls)

    m0 = tuple(jnp.full((HG, QB, 1), SOFTMAX_M0, f32) for _ in range(NG))
    l0 = tuple(jnp.zeros((HG, QB, 1), f32) for _ in range(NG))
    carry = lax.fori_loop(0, nch // 2, lambda c2, cr: att_body(2 * c2 + 1, att_body(2 * c2, cr)), (m0, l0))
    _, l_fin = lax.cond(nch % 2 == 1, lambda cr: att_body(nch - 1, cr), lambda cr: cr, carry)
    for g in range(NG):
        o_lat = (acc_scr[g * RG:(g + 1) * RG] / l_fin[g].reshape(RG, 1)).astype(MXU_DT)
        for k in range(HG):
            hd = g * HG + k
            out_ref[0, :, hd * C_DV:(hd + 1) * C_DV] = _dot(o_lat[k * QB:(k + 1) * QB], wuv_ref[hd]).astype(out_ref.dtype)


def _dsa(qa, kh, kv, qi, ki, wi, w_uv):
    B, H, T, _ = qa.shape
    QB = min(DSA_Q_BLOCK, T)
    KC = min(KEY_CHUNK, T)
    topk = min(C_TOPK, T // 4)
    wuv = w_uv.transpose(1, 0, 2).astype(MXU_DT)
    assert T // PACKED_ROWS <= 256
    eye = jnp.eye(QB, dtype=MXU_DT)
    tri = jnp.tril(jnp.ones((KC, KC), MXU_DT))
    kern = functools.partial(_dsa_kernel, QB=QB, KC=KC, SUB=min(KEY_SUB, KC), TOPK=topk, HG=DSA_HEAD_GROUP)
    half_words = pltpu.VMEM((T // KC, KC // PACKED_ROWS, PACKED_ROWS, QB), jnp.int16)
    return pl.pallas_call(
        kern,
        grid=(B, T // QB),
        in_specs=[pl.BlockSpec((1, H, QB, C_KDIM), lambda b, i: (b, 0, i, 0)),
                  pl.BlockSpec((1, H, QB, LANES), lambda b, i: (b, 0, i, 0)),
                  pl.BlockSpec((1, QB, LANES), lambda b, i: (b, i, 0)),
                  pl.BlockSpec((1, H, T, C_KDIM), lambda b, i: (b, 0, 0, 0), pipeline_mode=pl.Buffered(1)),
                  pl.BlockSpec((1, T, C_KVL), lambda b, i: (b, 0, 0)),
                  pl.BlockSpec((1, T, LANES), lambda b, i: (b, 0, 0)),
                  _const_spec(wuv.shape), _const_spec(eye.shape), _const_spec(tri.shape)],
        out_specs=pl.BlockSpec((1, QB, H * C_DV), lambda b, i: (b, i, 0)),
        out_shape=jax.ShapeDtypeStruct((B, T, H * C_DV), MXU_DT),
        scratch_shapes=[pltpu.VMEM((T // KC, KC, QB), jnp.int32),
                        half_words, half_words,
                        pltpu.VMEM((H * QB, C_KVL), jnp.float32)],
        compiler_params=_cparams("parallel", "arbitrary"),
    )(qa, qi, wi, kh, kv, ki, wuv, eye, tri)


def _post_kernel(*refs, n_mix, n_ff):
    h_ref = refs[0]
    mix = refs[1:1 + 2 * n_mix]
    g1, b1, w1_ref, w2_ref, g2, b2, wg_ref, p_ref, wp_ref, out_ref = refs[1 + 2 * n_mix:]
    h = h_ref[...]
    y = _dot(mix[0][...].astype(MXU_DT), mix[1][...])
    for k in range(1, n_mix):
        y = y + _dot(mix[2 * k][...].astype(MXU_DT), mix[2 * k + 1][...])
    h1 = _layer_norm(DN_ALPHA * h + y, g1[...], b1[...])
    h1b = h1.astype(MXU_DT)
    ff = D_FF // n_ff
    u = None
    for k in range(n_ff):
        a = jnp.square(jnp.maximum(_dot(h1b, w1_ref[:, k * ff:(k + 1) * ff]), 0.0))
        t = _dot(a.astype(MXU_DT), w2_ref[k * ff:(k + 1) * ff, :])
        u = t if u is None else u + t
    h2 = _layer_norm(DN_ALPHA * h1 + u, g2[...], b2[...])
    gate = jax.nn.sigmoid(_dot(h2.astype(MXU_DT), wg_ref[...]))
    out_ref[...] = h2 + gate * _dot(p_ref[...].astype(MXU_DT), wp_ref[...])


def _post(h2d, mixes, ln1_g, ln1_b, w1, w2, ln2_g, ln2_b, wg, p2d, wp, tm):
    M, D = h2d.shape
    row = lambda w: pl.BlockSpec((tm, w), lambda i: (i, 0))
    vec = lambda v: v[None, :]
    in_specs = [row(D)]
    args = [h2d]
    for x, w in mixes:
        in_specs += [row(x.shape[1]), _const_spec(w.shape)]
        args += [x, w]
    in_specs += [_const_spec((1, D)), _const_spec((1, D)), _const_spec(w1.shape), _const_spec(w2.shape),
                 _const_spec((1, D)), _const_spec((1, D)), _const_spec(wg.shape), row(D_PLE), _const_spec(wp.shape)]
    args += [vec(ln1_g), vec(ln1_b), w1, w2, vec(ln2_g), vec(ln2_b), wg, p2d, wp]
    kern = functools.partial(_post_kernel, n_mix=len(mixes), n_ff=4)
    return pl.pallas_call(
        kern,
        grid=(M // tm,),
        in_specs=in_specs,
        out_specs=row(D),
        out_shape=jax.ShapeDtypeStruct((M, D), jnp.float32),
        compiler_params=_cparams("parallel"),
    )(*args)


def kernel(x, p, positions, e_w_in, e_a_conv, e_a_i_b, e_a_f_b, e_a_norm, e_b_cmp_pos, e_b_cmp_w1, e_b_cmp_w2, e_b_g_b, e_w_out, o_w_in, o_q_norm, o_kv_norm, o_w_qb, o_w_uk, o_w_uv, o_w_iq, o_ik_g, o_ik_b, o_w_out, ln1_g, ln1_b, ln2_g, ln2_b, mlp_w1, mlp_w2, ple_gate_w, ple_w):
    B, T, D = x.shape
    M = B * T
    tm = min(ROW_TILE, T)
    h = x
    pos3 = positions[..., None]
    bf = lambda w: w.astype(MXU_DT)
    for i in range(DEPTH):
        j = i // 2
        if i % 2 == 0:
            qk, av, ao, small, bq, bc, bs = _even_proj(h, _even_w_in_aug(e_w_in[j]), tm)
            ya = _mlstm(qk, av, ao, small, e_a_conv[j], e_a_i_b[j], e_a_f_b[j], e_a_norm[j])
            cmp = _nsa_compress(bc, e_b_cmp_pos[j], e_b_cmp_w1[j], e_b_cmp_w2[j])
            yb = _nsa(bq, bs, cmp, small, e_b_g_b[j])
            w_out = e_w_out[j]
            mixes = [(ya.reshape(M, A_W), bf(w_out[:A_W])),
                     (yb.reshape(M, B_HEADS * LANES), bf(_pad_heads(w_out[A_W:], B_HEADS, B_DH, axis=0)))]
        else:
            qa, kh, kv, qi, ki, wi = _odd_prep(h, pos3, o_w_in[j], o_q_norm[j], o_kv_norm[j], o_w_qb[j], o_w_uk[j],
                                               o_w_iq[j], o_ik_g[j], o_ik_b[j], tm)
            o = _dsa(qa, kh, kv, qi, ki, wi, o_w_uv[j])
            mixes = [(o.reshape(M, C_HEADS * C_DV), bf(o_w_out[j]))]
        h = _post(h.reshape(M, D), mixes, ln1_g[i], ln1_b[i], bf(mlp_w1[i]), bf(mlp_w2[i]), ln2_g[i], ln2_b[i],
                  bf(ple_gate_w[i]), p[i].reshape(M, D_PLE), bf(ple_w[i]), min(MLP_ROW_TILE, T)).reshape(B, T, D)
    return h
```

```python
import functools

import numpy as np
import jax
import jax.numpy as jnp
from jax import lax
from jax.experimental import pallas as pl
from jax.experimental.pallas import tpu as pltpu

D_MODEL = 1024
DEPTH = 4
D_PLE = 256
D_FF = 4 * D_MODEL
DN_ALPHA = (2.0 * DEPTH) ** 0.25
LN_EPS = 1e-5
NEG = -1e30

A_HEADS = 4
A_DH = D_MODEL // 8
A_W = A_HEADS * A_DH
A_CONV = 4

B_HEADS = 8
B_DH = 64
B_KV = 2
B_HPG = B_HEADS // B_KV
B_CMP_LEN = 32
B_CMP_STRIDE = 16
B_CMP_HID = 128
B_SEL_BLK = 64
B_SEL_N = 16
B_WIN = 512

C_HEADS = 8
C_DN = 128
C_DR = 64
C_DV = 128
C_QL = 512
C_KVL = 256
C_IDX_HEADS = 8
C_IDX_DH = 64
C_IDX_DR = 32
C_TOPK = 256
ROPE_BASE = 10000.0

LANES = 128
SUBLANES = 8
PACKED_ROWS = 16
VMEM_LIMIT_BYTES = 56 * 2**20
MXU_DT = jnp.bfloat16
INT_MIN = -2**31

MLSTM_CHUNK = 256
MLSTM_SEQS = 1
ROW_TILE = 512
MLP_ROW_TILE = 512
Q_BLOCK = 256
DSA_Q_BLOCK = 256
DSA_HEAD_GROUP = 4
KEY_CHUNK = 512
KEY_SUB = 512
C_KDIM = C_DN + LANES

LOG2E = 1.4426950408889634
SOFTMAX_M0 = 0.5 * NEG
SOFTMAX_TINY = 1e-30


def _cparams(*sem):
    return pltpu.CompilerParams(dimension_semantics=sem, vmem_limit_bytes=VMEM_LIMIT_BYTES)


def _const_spec(shape):
    nd = len(shape)
    return pl.BlockSpec(shape, lambda *_: (0,) * nd, pipeline_mode=pl.Buffered(1))


def _dot(a, b):
    return jnp.dot(a, b, preferred_element_type=jnp.float32)


def _dot_nt(a, b):
    return lax.dot_general(a, b, (((1,), (1,)), ((), ())), preferred_element_type=jnp.float32)


def _dot_tn(a, b):
    return lax.dot_general(a, b, (((0,), (0,)), ((), ())), preferred_element_type=jnp.float32)


def _dot_f32(a, b):
    return jnp.dot(a, b, preferred_element_type=jnp.float32, precision=lax.Precision.HIGHEST)


def _layer_norm(x, g, b):
    mu = jnp.mean(x, axis=-1, keepdims=True)
    var = jnp.mean(jnp.square(x - mu), axis=-1, keepdims=True)
    return (x - mu) * lax.rsqrt(var + LN_EPS) * g + b


def _masked_softmax2(s, bias):
    s = s + bias
    m = jnp.maximum(jnp.max(s, axis=-1, keepdims=True), SOFTMAX_M0)
    e = jnp.exp2(s - m)
    return e / jnp.maximum(jnp.sum(e, axis=-1, keepdims=True), SOFTMAX_TINY)


def _iota(shape, dim):
    return lax.broadcasted_iota(jnp.int32, shape, dim)


def _pad_heads(w, n_heads, dh, axis=-1):
    axis = axis % w.ndim
    shp = w.shape[:axis] + (n_heads, dh) + w.shape[axis + 1:]
    w = w.reshape(shp)
    pad = [(0, 0)] * w.ndim
    pad[axis + 1] = (0, LANES - dh)
    w = jnp.pad(w, pad)
    return w.reshape(shp[:axis] + (n_heads * LANES,) + shp[axis + 2:])


def _pad_cols(w, width):
    return jnp.pad(w, [(0, 0)] * (w.ndim - 1) + [(0, width - w.shape[-1])])


def _even_proj_kernel(h_ref, w_ref, blk_ref, qk_ref, av_ref, ao_ref, sm_ref, bq_ref, bc_ref, bs_ref):
    z = _dot(h_ref[0].astype(MXU_DT), w_ref[...])
    o = 0
    qk_ref[0] = z[:, o:o + 2 * A_W]; o += 2 * A_W
    av_ref[0] = z[:, o:o + A_W].astype(av_ref.dtype); o += A_W
    ao_ref[0] = z[:, o:o + A_W]; o += A_W
    sm_ref[0] = z[:, o:o + LANES]; o += LANES
    for hd in range(B_HEADS):
        bq_ref[0, hd] = z[:, o:o + LANES].astype(bq_ref.dtype); o += LANES
    for j in range(2 * B_KV):
        bc_ref[0, j] = z[:, o:o + LANES]; o += LANES
    for j in range(4 * B_KV):
        zj = z[:, o:o + LANES]; o += LANES
        if j < B_KV:
            zj = zj + blk_ref[...]
        bs_ref[0, j] = zj.astype(bs_ref.dtype)


def _even_proj(h3, w_aug, tm):
    B, T, D = h3.shape
    n = w_aug.shape[1]
    f32 = jnp.float32
    out_shape = (
        jax.ShapeDtypeStruct((B, T, 2 * A_W), f32),
        jax.ShapeDtypeStruct((B, T, A_W), MXU_DT),
        jax.ShapeDtypeStruct((B, T, A_W), f32),
        jax.ShapeDtypeStruct((B, T, LANES), f32),
        jax.ShapeDtypeStruct((B, B_HEADS, T, LANES), MXU_DT),
        jax.ShapeDtypeStruct((B, 2 * B_KV, T, LANES), f32),
        jax.ShapeDtypeStruct((B, 4 * B_KV, T, LANES), MXU_DT),
    )
    row = lambda w: pl.BlockSpec((1, tm, w), lambda b, i: (b, i, 0))
    hm = lambda nh: pl.BlockSpec((1, nh, tm, LANES), lambda b, i: (b, 0, i, 0))
    assert T // B_SEL_BLK <= LANES - B_DH
    blk = (np.arange(LANES)[None, :] == B_DH + np.arange(T)[:, None] // B_SEL_BLK).astype(np.float32)
    return pl.pallas_call(
        _even_proj_kernel,
        grid=(B, T // tm),
        in_specs=[row(D), _const_spec((D, n)), pl.BlockSpec((tm, LANES), lambda b, i: (i, 0))],
        out_specs=(row(2 * A_W), row(A_W), row(A_W), row(LANES), hm(B_HEADS), hm(2 * B_KV), hm(4 * B_KV)),
        out_shape=out_shape,
        compiler_params=_cparams("parallel", "parallel"),
    )(h3, w_aug, jnp.asarray(blk))


def _even_w_in_aug(w_in):
    sizes = (A_W, A_W, A_W, A_W, A_HEADS, A_HEADS, B_HEADS * B_DH) + (B_KV * B_DH,) * 6 + (3 * B_HEADS,)
    parts, o = [], 0
    for s in sizes:
        parts.append(w_in[:, o:o + s]); o += s
    aq, ak, av, ao, ai, af, bq, bkc, bvc, bks, bvs, bkw, bvw, bg = parts
    small = _pad_cols(jnp.concatenate([ai, af, bg], -1), LANES)
    ph = lambda w: _pad_heads(w, B_KV, B_DH)
    bq = bq * (B_DH ** -0.5 * LOG2E)
    cols = [aq, ak, av, ao, small, _pad_heads(bq, B_HEADS, B_DH),
            ph(bkc), ph(bvc), ph(bks), ph(bvs), ph(bkw), ph(bvw)]
    return jnp.concatenate(cols, -1).astype(MXU_DT)


def _mlstm_kernel(qk_ref, v_ref, o_ref, gc_ref, gr_ref, cw_ref, bc_ref, br_ref, ng_ref, tri_ref, out_ref,
                  xs_scr, ct_scr, n_scr, m_scr, *, L, NB):
    c = pl.program_id(1)
    f32 = jnp.float32
    W2 = 2 * A_W

    @pl.when(c == 0)
    def _():
        xs_scr[:, 0:SUBLANES, :] = jnp.zeros((NB, SUBLANES, W2), f32)
        ct_scr[...] = jnp.zeros(ct_scr.shape, f32)
        n_scr[...] = jnp.zeros(n_scr.shape, f32)
        m_scr[...] = jnp.zeros(m_scr.shape, f32)

    tri = tri_ref[...]
    causal = _iota((L, L), 1) <= _iota((L, L), 0)
    cw = cw_ref[...]
    for bb in range(NB):
        _mlstm_chunk(bb, qk_ref, v_ref, o_ref, gc_ref, gr_ref, cw, bc_ref, br_ref, ng_ref, tri, causal, out_ref,
                     xs_scr, ct_scr, n_scr, m_scr, L)


def _mlstm_chunk(bb, qk_ref, v_ref, o_ref, gc_ref, gr_ref, cw, bc_ref, br_ref, ng_ref, tri, causal, out_ref,
                 xs_scr, ct_scr, n_scr, m_scr, L):
    f32 = jnp.float32
    xs_scr[bb, SUBLANES:SUBLANES + L, :] = qk_ref[bb]
    base = SUBLANES - (A_CONV - 1)
    acc = xs_scr[bb, base:base + L, :] * cw[0:1, :]
    for j in range(1, A_CONV):
        acc = acc + xs_scr[bb, base + j:base + j + L, :] * cw[j:j + 1, :]
    qk = acc * jax.nn.sigmoid(acc)
    xs_scr[bb, 0:SUBLANES, :] = xs_scr[bb, L:L + SUBLANES, :]

    gc = gc_ref[bb] + bc_ref[...]
    gr = gr_ref[bb, 0] + br_ref[...]
    ig_c = gc[:, 0:A_HEADS]
    b_c = _dot_f32(tri, jax.nn.log_sigmoid(gc[:, A_HEADS:2 * A_HEADS]))
    ig_r = gr[0:A_HEADS, :]
    b_r = _dot_f32(jax.nn.log_sigmoid(gr[A_HEADS:2 * A_HEADS, :]), tri.T)

    for hd in range(A_HEADS):
        sl = slice(hd * A_DH, (hd + 1) * A_DH)
        q_h = (qk[:, sl] * A_DH ** -0.5).astype(MXU_DT)
        k_f = qk[:, A_W + hd * A_DH:A_W + (hd + 1) * A_DH]
        k_h = k_f.astype(MXU_DT)
        v_h = v_ref[bb][:, sl]
        bi = b_c[:, hd:hd + 1]
        ic = ig_c[:, hd:hd + 1]
        dmat = jnp.where(causal, bi - b_r[hd:hd + 1, :] + ig_r[hd:hd + 1, :], NEG)
        m_prev = m_scr[bb, hd][:, 0:1]
        m_inter = bi + m_prev
        m_t = jnp.maximum(m_inter, jnp.max(dmat, axis=1, keepdims=True))
        e_inter = jnp.exp(m_inter - m_t)
        s = _dot_nt(q_h, k_h) * jnp.exp(dmat - m_t)
        ct = ct_scr[bb, hd]
        nrow = n_scr[bb, hd]
        num = e_inter * _dot(q_h, ct.astype(MXU_DT)) + _dot(s.astype(MXU_DT), v_h)
        den = e_inter * jnp.sum(q_h.astype(f32) * nrow, axis=1, keepdims=True) + jnp.sum(s, axis=1, keepdims=True)
        hc = num / jnp.maximum(jnp.abs(den), jnp.exp(-m_t))
        b_last = bi[L - 1:L, :]
        dec = b_last - bi + ic
        m_new = jnp.maximum(b_last + m_prev, jnp.max(dec, axis=0, keepdims=True))
        wgt = jnp.exp(dec - m_new)
        e_st = jnp.exp(b_last + m_prev - m_new)
        kw = k_f * wgt
        ct_scr[bb, hd] = e_st * ct + _dot_tn(kw.astype(MXU_DT), v_h)
        n_scr[bb, hd] = e_st * nrow + jnp.sum(kw, axis=0, keepdims=True)
        m_scr[bb, hd] = jnp.broadcast_to(m_new, (1, LANES))
        mu = jnp.mean(hc, axis=1, keepdims=True)
        var = jnp.mean(jnp.square(hc - mu), axis=1, keepdims=True)
        hn = (hc - mu) * lax.rsqrt(var + LN_EPS) * ng_ref[:, sl]
        out_ref[bb, :, sl] = (hn * jax.nn.sigmoid(o_ref[bb][:, sl])).astype(out_ref.dtype)


def _mlstm(qk, av, ao, small, conv_w, i_b, f_b, norm_g):
    B, T, _ = qk.shape
    L = min(MLSTM_CHUNK, T)
    N = T // L
    f32 = jnp.float32
    gates = small[..., 0:2 * A_HEADS]
    gates_r = gates.reshape(B, N, L, 2 * A_HEADS).transpose(0, 1, 3, 2)
    bias = jnp.concatenate([i_b, f_b]).astype(f32)
    tri = jnp.tril(jnp.ones((L, L), f32))
    NB = MLSTM_SEQS if B % MLSTM_SEQS == 0 else 1
    row = lambda w: pl.BlockSpec((NB, L, w), lambda b, c: (b, c, 0))
    kern = functools.partial(_mlstm_kernel, L=L, NB=NB)
    return pl.pallas_call(
        kern,
        grid=(B // NB, N),
        in_specs=[row(2 * A_W), row(A_W), row(A_W), row(2 * A_HEADS),
                  pl.BlockSpec((NB, 1, 2 * A_HEADS, L), lambda b, c: (b, c, 0, 0)),
                  _const_spec((A_CONV, 2 * A_W)), _const_spec((1, 2 * A_HEADS)), _const_spec((2 * A_HEADS, 1)),
                  _const_spec((1, A_W)), _const_spec((L, L))],
        out_specs=row(A_W),
        out_shape=jax.ShapeDtypeStruct((B, T, A_W), MXU_DT),
        scratch_shapes=[pltpu.VMEM((NB, L + 2 * SUBLANES, 2 * A_W), f32),
                        pltpu.VMEM((NB, A_HEADS, A_DH, A_DH), f32),
                        pltpu.VMEM((NB, A_HEADS, 1, A_DH), f32),
                        pltpu.VMEM((NB, A_HEADS, 1, LANES), f32)],
        compiler_params=_cparams("parallel", "arbitrary"),
    )(qk, av, ao, gates, gates_r, conv_w, bias[None, :], bias[:, None], norm_g[None, :], tri)


def _nsa_cmp_kernel(x_ref, w1a_ref, w1b_ref, w2_ref, pos_ref, out_ref):
    n = x_ref.shape[2] // B_CMP_STRIDE
    half = B_CMP_STRIDE * LANES
    for j in range(2):
        bias = (_dot(pos_ref[j, :, 0:half].astype(MXU_DT), w1a_ref[j])
                + _dot(pos_ref[j, :, half:2 * half].astype(MXU_DT), w1b_ref[j]))[0:1, :]
        for g in range(B_KV):
            u = jnp.concatenate([x_ref[0, j * B_KV + g, pl.ds(r, n, stride=B_CMP_STRIDE), :]
                                 for r in range(B_CMP_STRIDE)], axis=1).astype(MXU_DT)
            a = _dot(u, w1a_ref[j])
            bm = _dot(u, w1b_ref[j])
            pre = a + pltpu.roll(bm, n - 1, 0) + bias
            hid = jax.nn.gelu(pre)
            out_ref[0, j * B_KV + g] = _dot(hid.astype(MXU_DT), w2_ref[j])


def _nsa_compress(bc, cmp_pos, cmp_w1, cmp_w2):
    B, _, T, _ = bc.shape
    nblk = T // B_CMP_STRIDE
    half = B_CMP_STRIDE * LANES
    w1 = jnp.pad(cmp_w1.reshape(2, B_CMP_LEN, B_DH, B_CMP_HID), ((0, 0), (0, 0), (0, LANES - B_DH), (0, 0)))
    w1 = w1.reshape(2, B_CMP_LEN * LANES, B_CMP_HID).astype(MXU_DT)
    w1a, w1b = w1[:, :half], w1[:, half:]
    w2 = _pad_cols(cmp_w2, LANES).astype(MXU_DT)
    pos = jnp.pad(cmp_pos, ((0, 0), (0, 0), (0, LANES - B_DH))).reshape(2, 1, B_CMP_LEN * LANES)
    pos = jnp.broadcast_to(pos, (2, SUBLANES, B_CMP_LEN * LANES))
    return pl.pallas_call(
        _nsa_cmp_kernel,
        grid=(B,),
        in_specs=[pl.BlockSpec((1, 2 * B_KV, T, LANES), lambda b: (b, 0, 0, 0)),
                  _const_spec(w1a.shape), _const_spec(w1b.shape), _const_spec(w2.shape), _const_spec(pos.shape)],
        out_specs=pl.BlockSpec((1, 2 * B_KV, nblk, LANES), lambda b: (b, 0, 0, 0)),
        out_shape=jax.ShapeDtypeStruct((B, 2 * B_KV, nblk, LANES), jnp.float32),
        compiler_params=_cparams("parallel"),
    )(bc, w1a, w1b, w2, pos)


def _nsa_kernel(q_ref, kv_ref, cmp_ref, sm_ref, gb_ref, ovt_ref, out_ref, acc_scr, ocmp_scr, qaug_scr,
                *, QB, KC, WL, NSB, NSEL):
    f32 = jnp.float32
    i = pl.program_id(1)
    s0 = i * QB
    R = B_HPG * QB
    tq_col = s0 + _iota((QB, 1), 0)
    tq_row = s0 + _iota((1, QB), 1)
    nch = (s0 + QB + KC - 1) // KC
    gates = jax.nn.sigmoid(sm_ref[0] + gb_ref[...])
    ncmp = cmp_ref.shape[2]
    cmp_end = _iota((1, ncmp), 1) * B_CMP_STRIDE + (B_CMP_LEN - 1)
    cmp_bias = jnp.where(cmp_end <= tq_col, 0.0, NEG)[None]
    jb = _iota((NSB, 1), 0)
    cur = lax.shift_right_logical(tq_row, int(np.log2(B_SEL_BLK)))
    forced = (jb == 0) | (jb == cur) | (jb == cur - 1)
    valid = jb * B_SEL_BLK <= tq_row
    wstart = pl.multiple_of(jnp.maximum(s0 + QB - WL, 0), QB)
    wpos = wstart + _iota((1, WL), 1)
    win_bias = jnp.where((wpos <= tq_col) & (wpos > tq_col - B_WIN), 0.0, NEG)[None]

    for g in range(B_KV):
        qs = q_ref[0, g * B_HPG:(g + 1) * B_HPG].reshape(R, LANES)
        kcm = cmp_ref[0, g].astype(MXU_DT)
        vcm = cmp_ref[0, B_KV + g].astype(MXU_DT)
        p_cmp = _masked_softmax2(_dot_nt(qs, kcm).reshape(B_HPG, QB, ncmp), cmp_bias)
        ocmp_scr[g * R:(g + 1) * R] = _dot(p_cmp.reshape(R, ncmp).astype(MXU_DT), vcm)
        psum = jnp.sum(p_cmp, axis=0)
        imp_t = lax.dot_general(ovt_ref[...], psum, (((1,), (1,)), ((), ())),
                                preferred_element_type=f32, precision=lax.Precision.HIGHEST)
        sc = jnp.where(forced, 1e6, imp_t)
        sc = jnp.where(valid, sc, NEG)
        rank = jnp.zeros((NSB, QB), f32)
        for j in range(NSB):
            cj = jnp.broadcast_to(sc[j:j + 1, :], (NSB, QB))
            ahead = (cj > sc) | ((cj == sc) & (jb > j))
            rank = rank + jnp.where(ahead, 1.0, 0.0)
        drop = jnp.where(rank < NSEL, 0.0, NEG)
        pad = [jnp.zeros((B_DH, QB), f32), drop]
        if B_DH + NSB < LANES:
            pad.append(jnp.zeros((LANES - B_DH - NSB, QB), f32))
        drop_t = jnp.concatenate(pad, axis=0).T.astype(MXU_DT)
        for hd in range(B_HPG):
            rows = slice((g * B_HPG + hd) * QB, (g * B_HPG + hd + 1) * QB)
            qaug_scr[rows] = q_ref[0, g * B_HPG + hd] + drop_t

    acc_scr[...] = jnp.zeros(acc_scr.shape, f32)

    def sel_body(c, carry, diagonal=False):
        ms, ls = list(carry[0]), list(carry[1])
        ks = pl.multiple_of(c * KC, KC)
        for g in range(B_KV):
            kc_ = kv_ref[0, 0 * B_KV + g, pl.ds(ks, KC), :]
            vc_ = kv_ref[0, 1 * B_KV + g, pl.ds(ks, KC), :]
            s3 = _dot_nt(qaug_scr[g * R:(g + 1) * R], kc_).reshape(B_HPG, QB, KC)
            if diagonal:
                s3 = s3 + jnp.where(ks + _iota((1, KC), 1) <= tq_col, 0.0, NEG)[None]
            m_new = jnp.maximum(ms[g], jnp.max(s3, axis=-1, keepdims=True))
            p = jnp.exp2(s3 - m_new)
            alpha = jnp.exp2(ms[g] - m_new)
            ls[g] = alpha * ls[g] + jnp.sum(p, axis=-1, keepdims=True)
            ms[g] = m_new
            rows = slice(g * R, (g + 1) * R)
            acc_scr[rows] = alpha.reshape(R, 1) * acc_scr[rows] + _dot(p.reshape(R, KC).astype(MXU_DT), vc_)
        return tuple(ms), tuple(ls)

    m0 = tuple(jnp.full((B_HPG, QB, 1), SOFTMAX_M0, f32) for _ in range(B_KV))
    l0 = tuple(jnp.zeros((B_HPG, QB, 1), f32) for _ in range(B_KV))
    nfull = nch - 1
    carry = lax.fori_loop(0, nfull // 2, lambda c2, cr: sel_body(2 * c2 + 1, sel_body(2 * c2, cr)), (m0, l0))
    carry = lax.cond(nfull % 2 == 1, lambda cr: sel_body(nfull - 1, cr), lambda cr: cr, carry)
    _, l_fin = sel_body(nfull, carry, diagonal=True)

    for g in range(B_KV):
        rows = slice(g * R, (g + 1) * R)
        qs = q_ref[0, g * B_HPG:(g + 1) * B_HPG].reshape(R, LANES)
        o_sel = acc_scr[rows] / l_fin[g].reshape(R, 1)
        o_cmp = ocmp_scr[rows]
        kw_ = kv_ref[0, 2 * B_KV + g, pl.ds(wstart, WL), :]
        vw_ = kv_ref[0, 3 * B_KV + g, pl.ds(wstart, WL), :]
        p_win = _masked_softmax2(_dot_nt(qs, kw_).reshape(B_HPG, QB, WL), win_bias)
        o_win = _dot(p_win.reshape(R, WL).astype(MXU_DT), vw_)
        for hd in range(B_HPG):
            c0 = 2 * A_HEADS + (g * B_HPG + hd) * 3
            rs = slice(hd * QB, (hd + 1) * QB)
            o = (gates[:, c0:c0 + 1] * o_cmp[rs] + gates[:, c0 + 1:c0 + 2] * o_sel[rs]
                 + gates[:, c0 + 2:c0 + 3] * o_win[rs])
            col = (g * B_HPG + hd) * LANES
            out_ref[0, :, col:col + LANES] = o.astype(out_ref.dtype)


def _nsa(bq, bs, cmp, small, g_b):
    B, _, T, _ = bq.shape
    QB = min(Q_BLOCK, T)
    KC = min(KEY_CHUNK, T)
    assert KC % QB == 0
    WL = min(B_WIN + QB, T)
    NSB = T // B_SEL_BLK
    NSEL = min(B_SEL_N, NSB)
    ncmp = cmp.shape[2]
    f32 = jnp.float32
    M = (T - B_CMP_LEN) // B_CMP_STRIDE + 1
    assert NSB % SUBLANES == 0 and NSB <= LANES
    mi, jj = np.arange(ncmp)[None, :], np.arange(NSB)[:, None]
    ovt = ((mi * B_CMP_STRIDE < (jj + 1) * B_SEL_BLK) & (mi * B_CMP_STRIDE + B_CMP_LEN > jj * B_SEL_BLK)
           & (mi < M)).astype(np.float32)
    gb = jnp.zeros((1, LANES), f32).at[0, 2 * A_HEADS:2 * A_HEADS + 3 * B_HEADS].set(g_b)
    kern = functools.partial(_nsa_kernel, QB=QB, KC=KC, WL=WL, NSB=NSB, NSEL=NSEL)
    return pl.pallas_call(
        kern,
        grid=(B, T // QB),
        in_specs=[pl.BlockSpec((1, B_HEADS, QB, LANES), lambda b, i: (b, 0, i, 0)),
                  pl.BlockSpec((1, 4 * B_KV, T, LANES), lambda b, i: (b, 0, 0, 0)),
                  pl.BlockSpec((1, 2 * B_KV, ncmp, LANES), lambda b, i: (b, 0, 0, 0)),
                  pl.BlockSpec((1, QB, LANES), lambda b, i: (b, i, 0)),
                  _const_spec((1, LANES)), _const_spec(ovt.shape)],
        out_specs=pl.BlockSpec((1, QB, B_HEADS * LANES), lambda b, i: (b, i, 0)),
        out_shape=jax.ShapeDtypeStruct((B, T, B_HEADS * LANES), MXU_DT),
        scratch_shapes=[pltpu.VMEM((B_HEADS * QB, LANES), f32), pltpu.VMEM((B_HEADS * QB, LANES), f32),
                        pltpu.VMEM((B_HEADS * QB, LANES), MXU_DT)],
        compiler_params=_cparams("parallel", "arbitrary"),
    )(bq, bs, cmp, small, gb, jnp.asarray(ovt))


def _rope_rows(d):
    inv = ROPE_BASE ** (-jnp.arange(0, d, 2, dtype=jnp.float32) / d)
    z = jnp.zeros((C_DR - d,), jnp.float32)
    inv64 = jnp.concatenate([inv, inv, z])
    sgn64 = jnp.concatenate([-jnp.ones(d // 2), jnp.ones(d // 2), z]).astype(jnp.float32)
    return inv64, sgn64


def _odd_prep_kernel(h_ref, pos_ref, win_ref, wqb_ref, wiq_ref, wuk_ref, qn_ref, kvn_ref, ikg_ref, ikb_ref,
                     rope_ref, perm_ref, qa_ref, kh_ref, kv_ref, qi_ref, ki_ref, wi_ref):
    f32 = jnp.float32
    z = _dot(h_ref[0].astype(MXU_DT), win_ref[...])
    pos = pos_ref[0].astype(f32)
    rr = rope_ref[...]
    ang_q = pos * rr[0:1, :]
    cos_q, sin_q = jnp.cos(ang_q), jnp.sin(ang_q) * rr[1:2, :]
    ang_i = pos * rr[2:3, :]
    cos_i, sin_i = jnp.cos(ang_i), jnp.sin(ang_i) * rr[3:4, :]

    def rms(x, g):
        return x * lax.rsqrt(jnp.mean(jnp.square(x), axis=-1, keepdims=True) + LN_EPS) * g

    cq = rms(z[:, 0:C_QL], qn_ref[...])
    ckv = rms(z[:, C_QL:C_QL + C_KVL], kvn_ref[...])
    o = C_QL + C_KVL
    k_rope = (z[:, o:o + LANES] * cos_q + z[:, o + LANES:o + 2 * LANES] * sin_q).astype(kh_ref.dtype)
    ckvb = ckv.astype(kv_ref.dtype)
    kv_ref[0] = ckvb
    for hd in range(C_HEADS):
        kh_ref[0, hd, :, 0:C_DN] = _dot(ckvb, wuk_ref[hd]).astype(kh_ref.dtype)
        kh_ref[0, hd, :, C_DN:C_KDIM] = k_rope
    ik = z[:, o + 2 * LANES:o + 3 * LANES]
    real = _iota((1, LANES), 1) < C_IDX_DH
    mu = jnp.sum(ik, axis=-1, keepdims=True) / C_IDX_DH
    dlt = jnp.where(real, ik - mu, 0.0)
    var = jnp.sum(jnp.square(dlt), axis=-1, keepdims=True) / C_IDX_DH
    ki = dlt * lax.rsqrt(var + LN_EPS) * ikg_ref[...] + ikb_ref[...]
    ki = ki * cos_i + _dot_f32(ki, perm_ref[...]) * sin_i
    ki_ref[0] = ki.astype(ki_ref.dtype)
    wi_ref[0] = z[:, o + 3 * LANES:o + 4 * LANES] * (C_IDX_HEADS ** -0.5 * C_IDX_DH ** -0.5)
    cqb = cq.astype(MXU_DT)
    qf = _dot(cqb, wqb_ref[...])
    qi = _dot(cqb, wiq_ref[...])
    scale = (C_DN + C_DR) ** -0.5 * LOG2E
    nh = C_HEADS * LANES
    for hd in range(C_HEADS):
        cs = slice(hd * LANES, (hd + 1) * LANES)
        q_rope = qf[:, nh:2 * nh][:, cs] * cos_q + qf[:, 2 * nh:3 * nh][:, cs] * sin_q
        qa_ref[0, hd, :, 0:C_DN] = (qf[:, hd * C_DN:(hd + 1) * C_DN] * scale).astype(qa_ref.dtype)
        qa_ref[0, hd, :, C_DN:C_KDIM] = (q_rope * scale).astype(qa_ref.dtype)
        qi_h = qi[:, 0:nh][:, cs] * cos_i + qi[:, nh:2 * nh][:, cs] * sin_i
        qi_ref[0, hd] = qi_h.astype(qi_ref.dtype)


def _rot_cols(w, n_heads, dh, d):
    w = w.reshape(w.shape[0], n_heads, dh)
    h = d // 2
    return jnp.concatenate([w[..., h:d], w[..., 0:h], jnp.zeros_like(w[..., d:])], -1).reshape(w.shape[0], n_heads * dh)


def _odd_prep(h3, pos3, w_in, q_norm, kv_norm, w_qb, w_uk, w_iq, ik_g, ik_b, tm):
    B, T, D = h3.shape
    f32 = jnp.float32
    o = 0
    parts = []
    for s in (C_QL, C_KVL, C_DR, C_IDX_DH, C_IDX_HEADS):
        parts.append(w_in[:, o:o + s]); o += s
    w_cq, w_ckv, w_kr, w_ik, w_iw = parts
    pc = lambda w: _pad_cols(w, LANES)
    win = jnp.concatenate([w_cq, w_ckv, pc(w_kr), pc(_rot_cols(w_kr, 1, C_DR, C_DR)), pc(w_ik), pc(w_iw)],
                          -1).astype(MXU_DT)
    wq = w_qb.reshape(C_QL, C_HEADS, C_DN + C_DR)
    w_nope = wq[..., :C_DN].reshape(C_QL, C_HEADS * C_DN)
    w_rope = wq[..., C_DN:].reshape(C_QL, C_HEADS * C_DR)
    wqb = jnp.concatenate([w_nope, _pad_heads(w_rope, C_HEADS, C_DR),
                           _pad_heads(_rot_cols(w_rope, C_HEADS, C_DR, C_DR), C_HEADS, C_DR)], -1).astype(MXU_DT)
    wiq = jnp.concatenate([_pad_heads(w_iq, C_IDX_HEADS, C_IDX_DH),
                           _pad_heads(_rot_cols(w_iq, C_IDX_HEADS, C_IDX_DH, C_IDX_DR), C_IDX_HEADS, C_IDX_DH)],
                          -1).astype(MXU_DT)
    wuk = w_uk.transpose(1, 0, 2).astype(MXU_DT)
    inv_q, sgn_q = _rope_rows(C_DR)
    inv_i, sgn_i = _rope_rows(C_IDX_DR)
    rope = jnp.stack([jnp.tile(v, LANES // C_DR) for v in (inv_q, sgn_q, inv_i, sgn_i)])
    rope = jnp.concatenate([rope, jnp.zeros((SUBLANES - 4, LANES), f32)])
    hh = C_IDX_DR // 2
    src = np.arange(LANES)
    src[:hh] += hh
    src[hh:C_IDX_DR] -= hh
    perm = np.zeros((LANES, LANES), np.float32)
    perm[src, np.arange(LANES)] = 1.0
    ikg = _pad_cols(ik_g[None, :], LANES)
    ikb = _pad_cols(ik_b[None, :], LANES)
    out_shape = (
        jax.ShapeDtypeStruct((B, C_HEADS, T, C_KDIM), MXU_DT),
        jax.ShapeDtypeStruct((B, C_HEADS, T, C_KDIM), MXU_DT),
        jax.ShapeDtypeStruct((B, T, C_KVL), MXU_DT),
        jax.ShapeDtypeStruct((B, C_IDX_HEADS, T, LANES), MXU_DT),
        jax.ShapeDtypeStruct((B, T, LANES), MXU_DT),
        jax.ShapeDtypeStruct((B, T, LANES), f32),
    )
    row = lambda w: pl.BlockSpec((1, tm, w), lambda b, i: (b, i, 0))
    hm = lambda w: pl.BlockSpec((1, C_HEADS, tm, w), lambda b, i: (b, 0, i, 0))
    return pl.pallas_call(
        _odd_prep_kernel,
        grid=(B, T // tm),
        in_specs=[row(D), row(1), _const_spec(win.shape), _const_spec(wqb.shape), _const_spec(wiq.shape),
                  _const_spec(wuk.shape), _const_spec((1, C_QL)), _const_spec((1, C_KVL)),
                  _const_spec((1, LANES)), _const_spec((1, LANES)), _const_spec(rope.shape),
                  _const_spec(perm.shape)],
        out_specs=(hm(C_KDIM), hm(C_KDIM), row(C_KVL), hm(LANES), row(LANES), row(LANES)),
        out_shape=out_shape,
        compiler_params=_cparams("parallel", "parallel"),
    )(h3, pos3, win, wqb, wiq, wuk, q_norm[None, :], kv_norm[None, :], ikg, ikb, rope, jnp.asarray(perm))


def _dsa_kernel(qa_ref, qi_ref, wi_ref, kh_ref, kv_ref, ki_ref, wuv_ref, eye_ref, tri_ref, out_ref,
                key_scr, hi_scr, lo_scr, acc_scr, *, QB, KC, SUB, TOPK, HG):
    f32, i32, i16 = jnp.float32, jnp.int32, jnp.int16
    i = pl.program_id(1)
    s0 = i * QB
    H = C_HEADS
    NG = H // HG
    RG = HG * QB
    nch = (s0 + QB + KC - 1) // KC
    PK = PACKED_ROWS
    tq_row = s0 + _iota((1, QB), 1)
    w_t = wi_ref[0].T
    one, zero = jnp.ones((), MXU_DT), jnp.zeros((), MXU_DT)

    def idx_body(c, _):
        ks = pl.multiple_of(c * KC, KC)
        kic = ki_ref[0, pl.ds(ks, KC), :]
        isc = None
        for hd in range(C_IDX_HEADS):
            s = jnp.maximum(_dot_nt(kic, qi_ref[0, hd]), 0.0) * w_t[hd:hd + 1, :]
            isc = s if isc is None else isc + s
        isc = jnp.where(isc == 0.0, 0.0, isc)
        kpos = ks + _iota((KC, 1), 0)
        isc = jnp.where(kpos <= tq_row, isc, NEG)
        bits = lax.bitcast_convert_type(isc, i32)
        key = jnp.where(bits < 0, bits ^ jnp.int32(0x7FFFFFFF), bits)
        key_scr[c] = key
        k3 = key.reshape(KC // PK, PK, QB)
        hi_scr[c] = lax.shift_right_arithmetic(k3, 16).astype(i16)
        lo_scr[c] = ((k3 & 0xFFFF) - 32768).astype(i16)
        return 0

    lax.fori_loop(0, nch // 2, lambda c2, z: idx_body(2 * c2 + 1, idx_body(2 * c2, z)), 0)
    lax.cond(nch % 2 == 1, lambda z: idx_body(nch - 1, z), lambda z: z, 0)

    def rep16(v):
        return jnp.broadcast_to(v, (PK, QB)).astype(i16)[None]

    def select_threshold(nk):
        def count16(pred):
            accs = [jnp.zeros((PK, QB), MXU_DT) for _ in range(4)]
            for c in range(nk):
                x = jnp.where(pred(hi_scr[c], lo_scr[c]), one, zero)
                for r in range(KC // PK):
                    accs[r % len(accs)] = accs[r % len(accs)] + x[r]
            acc = (accs[0] + accs[1]) + (accs[2] + accs[3])
            return jnp.sum(acc.astype(f32), axis=0, keepdims=True)

        def bisect(pick, base):
            def body(b, t):
                cand = t + lax.shift_left(jnp.int32(1), 15 - b)
                c16 = rep16(cand)
                return jnp.where(base + count16(lambda h, l: pick(h, l) >= c16) >= TOPK, cand, t)
            return lax.fori_loop(0, 16, body, jnp.full((1, QB), -32768, i32))

        thi = bisect(lambda h, l: h, 0.0)
        thi16 = rep16(thi)
        for c in range(nk):
            lo_scr[c] = jnp.where(hi_scr[c] == thi16, lo_scr[c], jnp.full((), -32768, i16))
        n_hi = count16(lambda h, l: h > thi16)
        tlo = bisect(lambda h, l: l, n_hi)
        tlo16 = rep16(tlo)
        n_gt = n_hi + count16(lambda h, l: l > tlo16)
        n_eq = count16(lambda h, l: (h == thi16) & (l == tlo16))
        return thi, tlo, n_gt, n_eq

    n_variants = kv_ref.shape[1] // KC
    thi, tlo, n_gt, n_eq = lax.switch(nch - 1, [functools.partial(select_threshold, k + 1) for k in range(n_variants)])
    thr = thi * 65536 + (tlo + 32768)
    room = TOPK - n_gt

    @pl.when(jnp.max(jnp.where(n_eq > room, 1, 0)) > 0)
    def _():
        def body(c, before):
            key = key_scr[c]
            eq = key == thr
            seen = before + _dot(tri_ref[...], jnp.where(eq, 1.0, 0.0).astype(MXU_DT))
            key_scr[c] = jnp.where(eq & (seen > room), key - 1, key)
            return seen[KC - 1:KC, :]
        lax.fori_loop(0, nch, body, jnp.zeros((1, QB), f32))

    acc_scr[...] = jnp.zeros(acc_scr.shape, f32)

    def att_body(c, carry):
        ms, ls = list(carry[0]), list(carry[1])
        ks = pl.multiple_of(c * KC, KC)
        key = key_scr[c]
        kpos = ks + _iota((KC, 1), 0)
        sel_t = (key >= thr) & (kpos <= tq_row)
        keep = _dot_nt(eye_ref[...], jnp.where(sel_t, 1.0, 0.0).astype(MXU_DT))
        bias = jnp.where(keep > 0.5, 0.0, NEG)
        for u in range(KC // SUB):
            sub = pl.ds(pl.multiple_of(ks + u * SUB, SUB), SUB)
            kvc = kv_ref[0, sub, :]
            b_u = bias[:, u * SUB:(u + 1) * SUB][None]
            for g in range(NG):
                s = jnp.stack([_dot_nt(qa_ref[0, hd], kh_ref[0, hd, sub, :])
                               for hd in range(g * HG, (g + 1) * HG)]) + b_u
                m_new = jnp.maximum(ms[g], jnp.max(s, axis=-1, keepdims=True))
                p = jnp.exp2(s - m_new)
                alpha = jnp.exp2(ms[g] - m_new)
                ls[g] = alpha * ls[g] + jnp.sum(p, axis=-1, keepdims=True)
                ms[g] = m_new
                rows = slice(g * RG, (g + 1) * RG)
                pv = _dot(p.reshape(RG, SUB).astype(MXU_DT), kvc)
                acc_scr[rows] = alpha.reshape(RG, 1) * acc_scr[rows] + pv
        return tuple(ms), tuple(ls)

    m0 = tuple(jnp.full((HG, QB, 1), SOFTMAX_M0, f32) for _ in range(NG))
    l0 = tuple(jnp.zeros((HG, QB, 1), f32) for _ in range(NG))
    carry = lax.fori_loop(0, nch // 2, lambda c2, cr: att_body(2 * c2 + 1, att_body(2 * c2, cr)), (m0, l0))
    _, l_fin = lax.cond(nch % 2 == 1, lambda cr: att_body(nch - 1, cr), lambda cr: cr, carry)
    for g in range(NG):
        o_lat = (acc_scr[g * RG:(g + 1) * RG] / l_fin[g].reshape(RG, 1)).astype(MXU_DT)
        for k in range(HG):
            hd = g * HG + k
            out_ref[0, :, hd * C_DV:(hd + 1) * C_DV] = _dot(o_lat[k * QB:(k + 1) * QB], wuv_ref[hd]).astype(out_ref.dtype)


def _dsa(qa, kh, kv, qi, ki, wi, w_uv):
    B, H, T, _ = qa.shape
    QB = min(DSA_Q_BLOCK, T)
    KC = min(KEY_CHUNK, T)
    topk = min(C_TOPK, T // 4)
    wuv = w_uv.transpose(1, 0, 2).astype(MXU_DT)
    assert T // PACKED_ROWS <= 256
    eye = jnp.eye(QB, dtype=MXU_DT)
    tri = jnp.tril(jnp.ones((KC, KC), MXU_DT))
    kern = functools.partial(_dsa_kernel, QB=QB, KC=KC, SUB=min(KEY_SUB, KC), TOPK=topk, HG=DSA_HEAD_GROUP)
    half_words = pltpu.VMEM((T // KC, KC // PACKED_ROWS, PACKED_ROWS, QB), jnp.int16)
    return pl.pallas_call(
        kern,
        grid=(B, T // QB),
        in_specs=[pl.BlockSpec((1, H, QB, C_KDIM), lambda b, i: (b, 0, i, 0)),
                  pl.BlockSpec((1, H, QB, LANES), lambda b, i: (b, 0, i, 0)),
                  pl.BlockSpec((1, QB, LANES), lambda b, i: (b, i, 0)),
                  pl.BlockSpec((1, H, T, C_KDIM), lambda b, i: (b, 0, 0, 0), pipeline_mode=pl.Buffered(1)),
                  pl.BlockSpec((1, T, C_KVL), lambda b, i: (b, 0, 0)),
                  pl.BlockSpec((1, T, LANES), lambda b, i: (b, 0, 0)),
                  _const_spec(wuv.shape), _const_spec(eye.shape), _const_spec(tri.shape)],
        out_specs=pl.BlockSpec((1, QB, H * C_DV), lambda b, i: (b, i, 0)),
        out_shape=jax.ShapeDtypeStruct((B, T, H * C_DV), MXU_DT),
        scratch_shapes=[pltpu.VMEM((T // KC, KC, QB), jnp.int32),
                        half_words, half_words,
                        pltpu.VMEM((H * QB, C_KVL), jnp.float32)],
        compiler_params=_cparams("parallel", "arbitrary"),
    )(qa, qi, wi, kh, kv, ki, wuv, eye, tri)


def _post_kernel(*refs, n_mix, n_ff):
    h_ref = refs[0]
    mix = refs[1:1 + 2 * n_mix]
    g1, b1, w1_ref, w2_ref, g2, b2, wg_ref, p_ref, wp_ref, out_ref = refs[1 + 2 * n_mix:]
    h = h_ref[...]
    y = _dot(mix[0][...].astype(MXU_DT), mix[1][...])
    for k in range(1, n_mix):
        y = y + _dot(mix[2 * k][...].astype(MXU_DT), mix[2 * k + 1][...])
    h1 = _layer_norm(DN_ALPHA * h + y, g1[...], b1[...])
    h1b = h1.astype(MXU_DT)
    ff = D_FF // n_ff
    u = None
    for k in range(n_ff):
        a = jnp.square(jnp.maximum(_dot(h1b, w1_ref[:, k * ff:(k + 1) * ff]), 0.0))
        t = _dot(a.astype(MXU_DT), w2_ref[k * ff:(k + 1) * ff, :])
        u = t if u is None else u + t
    h2 = _layer_norm(DN_ALPHA * h1 + u, g2[...], b2[...])
    gate = jax.nn.sigmoid(_dot(h2.astype(MXU_DT), wg_ref[...]))
    out_ref[...] = h2 + gate * _dot(p_ref[...].astype(MXU_DT), wp_ref[...])


def _post(h2d, mixes, ln1_g, ln1_b, w1, w2, ln2_g, ln2_b, wg, p2d, wp, tm):
    M, D = h2d.shape
    row = lambda w: pl.BlockSpec((tm, w), lambda i: (i, 0))
    vec = lambda v: v[None, :]
    in_specs = [row(D)]
    args = [h2d]
    for x, w in mixes:
        in_specs += [row(x.shape[1]), _const_spec(w.shape)]
        args += [x, w]
    in_specs += [_const_spec((1, D)), _const_spec((1, D)), _const_spec(w1.shape), _const_spec(w2.shape),
                 _const_spec((1, D)), _const_spec((1, D)), _const_spec(wg.shape), row(D_PLE), _const_spec(wp.shape)]
    args += [vec(ln1_g), vec(ln1_b), w1, w2, vec(ln2_g), vec(ln2_b), wg, p2d, wp]
    kern = functools.partial(_post_kernel, n_mix=len(mixes), n_ff=4)
    return pl.pallas_call(
        kern,
        grid=(M // tm,),
        in_specs=in_specs,
        out_specs=row(D),
        out_shape=jax.ShapeDtypeStruct((M, D), jnp.float32),
        compiler_params=_cparams("parallel"),
    )(*args)


def kernel(x, p, positions, e_w_in, e_a_conv, e_a_i_b, e_a_f_b, e_a_norm, e_b_cmp_pos, e_b_cmp_w1, e_b_cmp_w2, e_b_g_b, e_w_out, o_w_in, o_q_norm, o_kv_norm, o_w_qb, o_w_uk, o_w_uv, o_w_iq, o_ik_g, o_ik_b, o_w_out, ln1_g, ln1_b, ln2_g, ln2_b, mlp_w1, mlp_w2, ple_gate_w, ple_w):
    B, T, D = x.shape
    M = B * T
    tm = min(ROW_TILE, T)
    h = x
    pos3 = positions[..., None]
    bf = lambda w: w.astype(MXU_DT)
    for i in range(DEPTH):
        j = i // 2
        if i % 2 == 0:
            qk, av, ao, small, bq, bc, bs = _even_proj(h, _even_w_in_aug(e_w_in[j]), tm)
            ya = _mlstm(qk, av, ao, small, e_a_conv[j], e_a_i_b[j], e_a_f_b[j], e_a_norm[j])
            cmp = _nsa_compress(bc, e_b_cmp_pos[j], e_b_cmp_w1[j], e_b_cmp_w2[j])
            yb = _nsa(bq, bs, cmp, small, e_b_g_b[j])
            w_out = e_w_out[j]
            mixes = [(ya.reshape(M, A_W), bf(w_out[:A_W])),
                     (yb.reshape(M, B_HEADS * LANES), bf(_pad_heads(w_out[A_W:], B_HEADS, B_DH, axis=0)))]
        else:
            qa, kh, kv, qi, ki, wi = _odd_prep(h, pos3, o_w_in[j], o_q_norm[j], o_kv_norm[j], o_w_qb[j], o_w_uk[j],
                                               o_w_iq[j], o_ik_g[j], o_ik_b[j], tm)
            o = _dsa(qa, kh, kv, qi, ki, wi, o_w_uv[j])
            mixes = [(o.reshape(M, C_HEADS * C_DV), bf(o_w_out[j]))]
        h = _post(h.reshape(M, D), mixes, ln1_g[i], ln1_b[i], bf(mlp_w1[i]), bf(mlp_w2[i]), ln2_g[i], ln2_b[i],
                  bf(ple_gate_w[i]), p[i].reshape(M, D_PLE), bf(ple_w[i]), min(MLP_ROW_TILE, T)).reshape(B, T, D)
    return h
```

```python
import functools

import numpy as np
import jax
import jax.numpy as jnp
from jax import lax
from jax.experimental import pallas as pl
from jax.experimental.pallas import tpu as pltpu

D_MODEL = 1024
DEPTH = 4
D_PLE = 256
D_FF = 4 * D_MODEL
DN_ALPHA = (2.0 * DEPTH) ** 0.25
LN_EPS = 1e-5
NEG = -1e30

A_HEADS = 4
A_DH = D_MODEL // 8
A_W = A_HEADS * A_DH
A_CONV = 4

B_HEADS = 8
B_DH = 64
B_KV = 2
B_HPG = B_HEADS // B_KV
B_CMP_LEN = 32
B_CMP_STRIDE = 16
B_CMP_HID = 128
B_SEL_BLK = 64
B_SEL_N = 16
B_WIN = 512

C_HEADS = 8
C_DN = 128
C_DR = 64
C_DV = 128
C_QL = 512
C_KVL = 256
C_IDX_HEADS = 8
C_IDX_DH = 64
C_IDX_DR = 32
C_TOPK = 256
ROPE_BASE = 10000.0

LANES = 128
SUBLANES = 8
PACKED_ROWS = 16
VMEM_LIMIT_BYTES = 56 * 2**20
MXU_DT = jnp.bfloat16
INT_MIN = -2**31

MLSTM_CHUNK = 256
MLSTM_SEQS = 1
ROW_TILE = 512
MLP_ROW_TILE = 512
Q_BLOCK = 256
DSA_Q_BLOCK = 256
DSA_HEAD_GROUP = 4
KEY_CHUNK = 512
KEY_SUB = 512
C_KDIM = C_DN + LANES

LOG2E = 1.4426950408889634
SOFTMAX_M0 = 0.5 * NEG
SOFTMAX_TINY = 1e-30


def _cparams(*sem):
    return pltpu.CompilerParams(dimension_semantics=sem, vmem_limit_bytes=VMEM_LIMIT_BYTES)


def _const_spec(shape):
    nd = len(shape)
    return pl.BlockSpec(shape, lambda *_: (0,) * nd, pipeline_mode=pl.Buffered(1))


def _dot(a, b):
    return jnp.dot(a, b, preferred_element_type=jnp.float32)


def _dot_nt(a, b):
    return lax.dot_general(a, b, (((1,), (1,)), ((), ())), preferred_element_type=jnp.float32)


def _dot_tn(a, b):
    return lax.dot_general(a, b, (((0,), (0,)), ((), ())), preferred_element_type=jnp.float32)


def _dot_f32(a, b):
    return jnp.dot(a, b, preferred_element_type=jnp.float32, precision=lax.Precision.HIGHEST)


def _layer_norm(x, g, b):
    mu = jnp.mean(x, axis=-1, keepdims=True)
    var = jnp.mean(jnp.square(x - mu), axis=-1, keepdims=True)
    return (x - mu) * lax.rsqrt(var + LN_EPS) * g + b


def _masked_softmax2(s, bias):
    s = s + bias
    m = jnp.maximum(jnp.max(s, axis=-1, keepdims=True), SOFTMAX_M0)
    e = jnp.exp2(s - m)
    return e / jnp.maximum(jnp.sum(e, axis=-1, keepdims=True), SOFTMAX_TINY)


def _iota(shape, dim):
    return lax.broadcasted_iota(jnp.int32, shape, dim)


def _pad_heads(w, n_heads, dh, axis=-1):
    axis = axis % w.ndim
    shp = w.shape[:axis] + (n_heads, dh) + w.shape[axis + 1:]
    w = w.reshape(shp)
    pad = [(0, 0)] * w.ndim
    pad[axis + 1] = (0, LANES - dh)
    w = jnp.pad(w, pad)
    return w.reshape(shp[:axis] + (n_heads * LANES,) + shp[axis + 2:])


def _pad_cols(w, width):
    return jnp.pad(w, [(0, 0)] * (w.ndim - 1) + [(0, width - w.shape[-1])])


def _even_proj_kernel(h_ref, w_ref, blk_ref, qk_ref, av_ref, ao_ref, sm_ref, bq_ref, bc_ref, bs_ref):
    z = _dot(h_ref[0].astype(MXU_DT), w_ref[...])
    o = 0
    qk_ref[0] = z[:, o:o + 2 * A_W]; o += 2 * A_W
    av_ref[0] = z[:, o:o + A_W].astype(av_ref.dtype); o += A_W
    ao_ref[0] = z[:, o:o + A_W]; o += A_W
    sm_ref[0] = z[:, o:o + LANES]; o += LANES
    for hd in range(B_HEADS):
        bq_ref[0, hd] = z[:, o:o + LANES].astype(bq_ref.dtype); o += LANES
    for j in range(2 * B_KV):
        bc_ref[0, j] = z[:, o:o + LANES]; o += LANES
    for j in range(4 * B_KV):
        zj = z[:, o:o + LANES]; o += LANES
        if j < B_KV:
            zj = zj + blk_ref[...]
        bs_ref[0, j] = zj.astype(bs_ref.dtype)


def _even_proj(h3, w_aug, tm):
    B, T, D = h3.shape
    n = w_aug.shape[1]
    f32 = jnp.float32
    out_shape = (
        jax.ShapeDtypeStruct((B, T, 2 * A_W), f32),
        jax.ShapeDtypeStruct((B, T, A_W), MXU_DT),
        jax.ShapeDtypeStruct((B, T, A_W), f32),
        jax.ShapeDtypeStruct((B, T, LANES), f32),
        jax.ShapeDtypeStruct((B, B_HEADS, T, LANES), MXU_DT),
        jax.ShapeDtypeStruct((B, 2 * B_KV, T, LANES), f32),
        jax.ShapeDtypeStruct((B, 4 * B_KV, T, LANES), MXU_DT),
    )
    row = lambda w: pl.BlockSpec((1, tm, w), lambda b, i: (b, i, 0))
    hm = lambda nh: pl.BlockSpec((1, nh, tm, LANES), lambda b, i: (b, 0, i, 0))
    assert T // B_SEL_BLK <= LANES - B_DH
    blk = (np.arange(LANES)[None, :] == B_DH + np.arange(T)[:, None] // B_SEL_BLK).astype(np.float32)
    return pl.pallas_call(
        _even_proj_kernel,
        grid=(B, T // tm),
        in_specs=[row(D), _const_spec((D, n)), pl.BlockSpec((tm, LANES), lambda b, i: (i, 0))],
        out_specs=(row(2 * A_W), row(A_W), row(A_W), row(LANES), hm(B_HEADS), hm(2 * B_KV), hm(4 * B_KV)),
        out_shape=out_shape,
        compiler_params=_cparams("parallel", "parallel"),
    )(h3, w_aug, jnp.asarray(blk))


def _even_w_in_aug(w_in):
    sizes = (A_W, A_W, A_W, A_W, A_HEADS, A_HEADS, B_HEADS * B_DH) + (B_KV * B_DH,) * 6 + (3 * B_HEADS,)
    parts, o = [], 0
    for s in sizes:
        parts.append(w_in[:, o:o + s]); o += s
    aq, ak, av, ao, ai, af, bq, bkc, bvc, bks, bvs, bkw, bvw, bg = parts
    small = _pad_cols(jnp.concatenate([ai, af, bg], -1), LANES)
    ph = lambda w: _pad_heads(w, B_KV, B_DH)
    bq = bq * (B_DH ** -0.5 * LOG2E)
    cols = [aq, ak, av, ao, small, _pad_heads(bq, B_HEADS, B_DH),
            ph(bkc), ph(bvc), ph(bks), ph(bvs), ph(bkw), ph(bvw)]
    return jnp.concatenate(cols, -1).astype(MXU_DT)


def _mlstm_kernel(qk_ref, v_ref, o_ref, gc_ref, gr_ref, cw_ref, bc_ref, br_ref, ng_ref, tri_ref, out_ref,
                  xs_scr, ct_scr, n_scr, m_scr, *, L, NB):
    c = pl.program_id(1)
    f32 = jnp.float32
    W2 = 2 * A_W

    @pl.when(c == 0)
    def _():
        xs_scr[:, 0:SUBLANES, :] = jnp.zeros((NB, SUBLANES, W2), f32)
        ct_scr[...] = jnp.zeros(ct_scr.shape, f32)
        n_scr[...] = jnp.zeros(n_scr.shape, f32)
        m_scr[...] = jnp.zeros(m_scr.shape, f32)

    tri = tri_ref[...]
    causal = _iota((L, L), 1) <= _iota((L, L), 0)
    cw = cw_ref[...]
    for bb in range(NB):
        _mlstm_chunk(bb, qk_ref, v_ref, o_ref, gc_ref, gr_ref, cw, bc_ref, br_ref, ng_ref, tri, causal, out_ref,
                     xs_scr, ct_scr, n_scr, m_scr, L)


def _mlstm_chunk(bb, qk_ref, v_ref, o_ref, gc_ref, gr_ref, cw, bc_ref, br_ref, ng_ref, tri, causal, out_ref,
                 xs_scr, ct_scr, n_scr, m_scr, L):
    f32 = jnp.float32
    xs_scr[bb, SUBLANES:SUBLANES + L, :] = qk_ref[bb]
    base = SUBLANES - (A_CONV - 1)
    acc = xs_scr[bb, base:base + L, :] * cw[0:1, :]
    for j in range(1, A_CONV):
        acc = acc + xs_scr[bb, base + j:base + j + L, :] * cw[j:j + 1, :]
    qk = acc * jax.nn.sigmoid(acc)
    xs_scr[bb, 0:SUBLANES, :] = xs_scr[bb, L:L + SUBLANES, :]

    gc = gc_ref[bb] + bc_ref[...]
    gr = gr_ref[bb, 0] + br_ref[...]
    ig_c = gc[:, 0:A_HEADS]
    b_c = _dot_f32(tri, jax.nn.log_sigmoid(gc[:, A_HEADS:2 * A_HEADS]))
    ig_r = gr[0:A_HEADS, :]
    b_r = _dot_f32(jax.nn.log_sigmoid(gr[A_HEADS:2 * A_HEADS, :]), tri.T)

    for hd in range(A_HEADS):
        sl = slice(hd * A_DH, (hd + 1) * A_DH)
        q_h = (qk[:, sl] * A_DH ** -0.5).astype(MXU_DT)
        k_f = qk[:, A_W + hd * A_DH:A_W + (hd + 1) * A_DH]
        k_h = k_f.astype(MXU_DT)
        v_h = v_ref[bb][:, sl]
        bi = b_c[:, hd:hd + 1]
        ic = ig_c[:, hd:hd + 1]
        dmat = jnp.where(causal, bi - b_r[hd:hd + 1, :] + ig_r[hd:hd + 1, :], NEG)
        m_prev = m_scr[bb, hd][:, 0:1]
        m_inter = bi + m_prev
        m_t = jnp.maximum(m_inter, jnp.max(dmat, axis=1, keepdims=True))
        e_inter = jnp.exp(m_inter - m_t)
        s = _dot_nt(q_h, k_h) * jnp.exp(dmat - m_t)
        ct = ct_scr[bb, hd]
        nrow = n_scr[bb, hd]
        num = e_inter * _dot(q_h, ct.astype(MXU_DT)) + _dot(s.astype(MXU_DT), v_h)
        den = e_inter * jnp.sum(q_h.astype(f32) * nrow, axis=1, keepdims=True) + jnp.sum(s, axis=1, keepdims=True)
        hc = num / jnp.maximum(jnp.abs(den), jnp.exp(-m_t))
        b_last = bi[L - 1:L, :]
        dec = b_last - bi + ic
        m_new = jnp.maximum(b_last + m_prev, jnp.max(dec, axis=0, keepdims=True))
        wgt = jnp.exp(dec - m_new)
        e_st = jnp.exp(b_last + m_prev - m_new)
        kw = k_f * wgt
        ct_scr[bb, hd] = e_st * ct + _dot_tn(kw.astype(MXU_DT), v_h)
        n_scr[bb, hd] = e_st * nrow + jnp.sum(kw, axis=0, keepdims=True)
        m_scr[bb, hd] = jnp.broadcast_to(m_new, (1, LANES))
        mu = jnp.mean(hc, axis=1, keepdims=True)
        var = jnp.mean(jnp.square(hc - mu), axis=1, keepdims=True)
        hn = (hc - mu) * lax.rsqrt(var + LN_EPS) * ng_ref[:, sl]
        out_ref[bb, :, sl] = (hn * jax.nn.sigmoid(o_ref[bb][:, sl])).astype(out_ref.dtype)


def _mlstm(qk, av, ao, small, conv_w, i_b, f_b, norm_g):
    B, T, _ = qk.shape
    L = min(MLSTM_CHUNK, T)
    N = T // L
    f32 = jnp.float32
    gates = small[..., 0:2 * A_HEADS]
    gates_r = gates.reshape(B, N, L, 2 * A_HEADS).transpose(0, 1, 3, 2)
    bias = jnp.concatenate([i_b, f_b]).astype(f32)
    tri = jnp.tril(jnp.ones((L, L), f32))
    NB = MLSTM_SEQS if B % MLSTM_SEQS == 0 else 1
    row = lambda w: pl.BlockSpec((NB, L, w), lambda b, c: (b, c, 0))
    kern = functools.partial(_mlstm_kernel, L=L, NB=NB)
    return pl.pallas_call(
        kern,
        grid=(B // NB, N),
        in_specs=[row(2 * A_W), row(A_W), row(A_W), row(2 * A_HEADS),
                  pl.BlockSpec((NB, 1, 2 * A_HEADS, L), lambda b, c: (b, c, 0, 0)),
                  _const_spec((A_CONV, 2 * A_W)), _const_spec((1, 2 * A_HEADS)), _const_spec((2 * A_HEADS, 1)),
                  _const_spec((1, A_W)), _const_spec((L, L))],
        out_specs=row(A_W),
        out_shape=jax.ShapeDtypeStruct((B, T, A_W), MXU_DT),
        scratch_shapes=[pltpu.VMEM((NB, L + 2 * SUBLANES, 2 * A_W), f32),
                        pltpu.VMEM((NB, A_HEADS, A_DH, A_DH), f32),
                        pltpu.VMEM((NB, A_HEADS, 1, A_DH), f32),
                        pltpu.VMEM((NB, A_HEADS, 1, LANES), f32)],
        compiler_params=_cparams("parallel", "arbitrary"),
    )(qk, av, ao, gates, gates_r, conv_w, bias[None, :], bias[:, None], norm_g[None, :], tri)


def _nsa_cmp_kernel(x_ref, w1a_ref, w1b_ref, w2_ref, pos_ref, out_ref):
    n = x_ref.shape[2] // B_CMP_STRIDE
    half = B_CMP_STRIDE * LANES
    for j in range(2):
        bias = (_dot(pos_ref[j, :, 0:half].astype(MXU_DT), w1a_ref[j])
                + _dot(pos_ref[j, :, half:2 * half].astype(MXU_DT), w1b_ref[j]))[0:1, :]
        for g in range(B_KV):
            u = jnp.concatenate([x_ref[0, j * B_KV + g, pl.ds(r, n, stride=B_CMP_STRIDE), :]
                                 for r in range(B_CMP_STRIDE)], axis=1).astype(MXU_DT)
            a = _dot(u, w1a_ref[j])
            bm = _dot(u, w1b_ref[j])
            pre = a + pltpu.roll(bm, n - 1, 0) + bias
            hid = jax.nn.gelu(pre)
            out_ref[0, j * B_KV + g] = _dot(hid.astype(MXU_DT), w2_ref[j])


def _nsa_compress(bc, cmp_pos, cmp_w1, cmp_w2):
    B, _, T, _ = bc.shape
    nblk = T // B_CMP_STRIDE
    half = B_CMP_STRIDE * LANES
    w1 = jnp.pad(cmp_w1.reshape(2, B_CMP_LEN, B_DH, B_CMP_HID), ((0, 0), (0, 0), (0, LANES - B_DH), (0, 0)))
    w1 = w1.reshape(2, B_CMP_LEN * LANES, B_CMP_HID).astype(MXU_DT)
    w1a, w1b = w1[:, :half], w1[:, half:]
    w2 = _pad_cols(cmp_w2, LANES).astype(MXU_DT)
    pos = jnp.pad(cmp_pos, ((0, 0), (0, 0), (0, LANES - B_DH))).reshape(2, 1, B_CMP_LEN * LANES)
    pos = jnp.broadcast_to(pos, (2, SUBLANES, B_CMP_LEN * LANES))
    return pl.pallas_call(
        _nsa_cmp_kernel,
        grid=(B,),
        in_specs=[pl.BlockSpec((1, 2 * B_KV, T, LANES), lambda b: (b, 0, 0, 0)),
                  _const_spec(w1a.shape), _const_spec(w1b.shape), _const_spec(w2.shape), _const_spec(pos.shape)],
        out_specs=pl.BlockSpec((1, 2 * B_KV, nblk, LANES), lambda b: (b, 0, 0, 0)),
        out_shape=jax.ShapeDtypeStruct((B, 2 * B_KV, nblk, LANES), jnp.float32),
        compiler_params=_cparams("parallel"),
    )(bc, w1a, w1b, w2, pos)


def _nsa_kernel(q_ref, kv_ref, cmp_ref, sm_ref, gb_ref, ovt_ref, out_ref, acc_scr, ocmp_scr, qaug_scr,
                *, QB, KC, WL, NSB, NSEL):
    f32 = jnp.float32
    i = pl.program_id(1)
    s0 = i * QB
    R = B_HPG * QB
    tq_col = s0 + _iota((QB, 1), 0)
    tq_row = s0 + _iota((1, QB), 1)
    nch = (s0 + QB + KC - 1) // KC
    gates = jax.nn.sigmoid(sm_ref[0] + gb_ref[...])
    ncmp = cmp_ref.shape[2]
    cmp_end = _iota((1, ncmp), 1) * B_CMP_STRIDE + (B_CMP_LEN - 1)
    cmp_bias = jnp.where(cmp_end <= tq_col, 0.0, NEG)[None]
    jb = _iota((NSB, 1), 0)
    cur = lax.shift_right_logical(tq_row, int(np.log2(B_SEL_BLK)))
    forced = (jb == 0) | (jb == cur) | (jb == cur - 1)
    valid = jb * B_SEL_BLK <= tq_row
    wstart = pl.multiple_of(jnp.maximum(s0 + QB - WL, 0), QB)
    wpos = wstart + _iota((1, WL), 1)
    win_bias = jnp.where((wpos <= tq_col) & (wpos > tq_col - B_WIN), 0.0, NEG)[None]

    for g in range(B_KV):
        qs = q_ref[0, g * B_HPG:(g + 1) * B_HPG].reshape(R, LANES)
        kcm = cmp_ref[0, g].astype(MXU_DT)
        vcm = cmp_ref[0, B_KV + g].astype(MXU_DT)
        p_cmp = _masked_softmax2(_dot_nt(qs, kcm).reshape(B_HPG, QB, ncmp), cmp_bias)
        ocmp_scr[g * R:(g + 1) * R] = _dot(p_cmp.reshape(R, ncmp).astype(MXU_DT), vcm)
        psum = jnp.sum(p_cmp, axis=0)
        imp_t = lax.dot_general(ovt_ref[...], psum, (((1,), (1,)), ((), ())),
                                preferred_element_type=f32, precision=lax.Precision.HIGHEST)
        sc = jnp.where(forced, 1e6, imp_t)
        sc = jnp.where(valid, sc, NEG)
        rank = jnp.zeros((NSB, QB), f32)
        for j in range(NSB):
            cj = jnp.broadcast_to(sc[j:j + 1, :], (NSB, QB))
            ahead = (cj > sc) | ((cj == sc) & (jb > j))
            rank = rank + jnp.where(ahead, 1.0, 0.0)
        drop = jnp.where(rank < NSEL, 0.0, NEG)
        pad = [jnp.zeros((B_DH, QB), f32), drop]
        if B_DH + NSB < LANES:
            pad.append(jnp.zeros((LANES - B_DH - NSB, QB), f32))
        drop_t = jnp.concatenate(pad, axis=0).T.astype(MXU_DT)
        for hd in range(B_HPG):
            rows = slice((g * B_HPG + hd) * QB, (g * B_HPG + hd + 1) * QB)
            qaug_scr[rows] = q_ref[0, g * B_HPG + hd] + drop_t

    acc_scr[...] = jnp.zeros(acc_scr.shape, f32)

    def sel_body(c, carry, diagonal=False):
        ms, ls = list(carry[0]), list(carry[1])
        ks = pl.multiple_of(c * KC, KC)
        for g in range(B_KV):
            kc_ = kv_ref[0, 0 * B_KV + g, pl.ds(ks, KC), :]
            vc_ = kv_ref[0, 1 * B_KV + g, pl.ds(ks, KC), :]
            s3 = _dot_nt(qaug_scr[g * R:(g + 1) * R], kc_).reshape(B_HPG, QB, KC)
            if diagonal:
                s3 = s3 + jnp.where(ks + _iota((1, KC), 1) <= tq_col, 0.0, NEG)[None]
            m_new = jnp.maximum(ms[g], jnp.max(s3, axis=-1, keepdims=True))
            p = jnp.exp2(s3 - m_new)
            alpha = jnp.exp2(ms[g] - m_new)
            ls[g] = alpha * ls[g] + jnp.sum(p, axis=-1, keepdims=True)
            ms[g] = m_new
            rows = slice(g * R, (g + 1) * R)
            acc_scr[rows] = alpha.reshape(R, 1) * acc_scr[rows] + _dot(p.reshape(R, KC).astype(MXU_DT), vc_)
        return tuple(ms), tuple(ls)

    m0 = tuple(jnp.full((B_HPG, QB, 1), SOFTMAX_M0, f32) for _ in range(B_KV))
    l0 = tuple(jnp.zeros((B_HPG, QB, 1), f32) for _ in range(B_KV))
    nfull = nch - 1
    carry = lax.fori_loop(0, nfull // 2, lambda c2, cr: sel_body(2 * c2 + 1, sel_body(2 * c2, cr)), (m0, l0))
    carry = lax.cond(nfull % 2 == 1, lambda cr: sel_body(nfull - 1, cr), lambda cr: cr, carry)
    _, l_fin = sel_body(nfull, carry, diagonal=True)

    for g in range(B_KV):
        rows = slice(g * R, (g + 1) * R)
        qs = q_ref[0, g * B_HPG:(g + 1) * B_HPG].reshape(R, LANES)
        o_sel = acc_scr[rows] / l_fin[g].reshape(R, 1)
        o_cmp = ocmp_scr[rows]
        kw_ = kv_ref[0, 2 * B_KV + g, pl.ds(wstart, WL), :]
        vw_ = kv_ref[0, 3 * B_KV + g, pl.ds(wstart, WL), :]
        p_win = _masked_softmax2(_dot_nt(qs, kw_).reshape(B_HPG, QB, WL), win_bias)
        o_win = _dot(p_win.reshape(R, WL).astype(MXU_DT), vw_)
        for hd in range(B_HPG):
            c0 = 2 * A_HEADS + (g * B_HPG + hd) * 3
            rs = slice(hd * QB, (hd + 1) * QB)
            o = (gates[:, c0:c0 + 1] * o_cmp[rs] + gates[:, c0 + 1:c0 + 2] * o_sel[rs]
                 + gates[:, c0 + 2:c0 + 3] * o_win[rs])
            col = (g * B_HPG + hd) * LANES
            out_ref[0, :, col:col + LANES] = o.astype(out_ref.dtype)


def _nsa(bq, bs, cmp, small, g_b):
    B, _, T, _ = bq.shape
    QB = min(Q_BLOCK, T)
    KC = min(KEY_CHUNK, T)
    assert KC % QB == 0
    WL = min(B_WIN + QB, T)
    NSB = T // B_SEL_BLK
    NSEL = min(B_SEL_N, NSB)
    ncmp = cmp.shape[2]
    f32 = jnp.float32
    M = (T - B_CMP_LEN) // B_CMP_STRIDE + 1
    assert NSB % SUBLANES == 0 and NSB <= LANES
    mi, jj = np.arange(ncmp)[None, :], np.arange(NSB)[:, None]
    ovt = ((mi * B_CMP_STRIDE < (jj + 1) * B_SEL_BLK) & (mi * B_CMP_STRIDE + B_CMP_LEN > jj * B_SEL_BLK)
           & (mi < M)).astype(np.float32)
    gb = jnp.zeros((1, LANES), f32).at[0, 2 * A_HEADS:2 * A_HEADS + 3 * B_HEADS].set(g_b)
    kern = functools.partial(_nsa_kernel, QB=QB, KC=KC, WL=WL, NSB=NSB, NSEL=NSEL)
    return pl.pallas_call(
        kern,
        grid=(B, T // QB),
        in_specs=[pl.BlockSpec((1, B_HEADS, QB, LANES), lambda b, i: (b, 0, i, 0)),
                  pl.BlockSpec((1, 4 * B_KV, T, LANES), lambda b, i: (b, 0, 0, 0)),
                  pl.BlockSpec((1, 2 * B_KV, ncmp, LANES), lambda b, i: (b, 0, 0, 0)),
                  pl.BlockSpec((1, QB, LANES), lambda b, i: (b, i, 0)),
                  _const_spec((1, LANES)), _const_spec(ovt.shape)],
        out_specs=pl.BlockSpec((1, QB, B_HEADS * LANES), lambda b, i: (b, i, 0)),
        out_shape=jax.ShapeDtypeStruct((B, T, B_HEADS * LANES), MXU_DT),
        scratch_shapes=[pltpu.VMEM((B_HEADS * QB, LANES), f32), pltpu.VMEM((B_HEADS * QB, LANES), f32),
                        pltpu.VMEM((B_HEADS * QB, LANES), MXU_DT)],
        compiler_params=_cparams("parallel", "arbitrary"),
    )(bq, bs, cmp, small, gb, jnp.asarray(ovt))


def _rope_rows(d):
    inv = ROPE_BASE ** (-jnp.arange(0, d, 2, dtype=jnp.float32) / d)
    z = jnp.zeros((C_DR - d,), jnp.float32)
    inv64 = jnp.concatenate([inv, inv, z])
    sgn64 = jnp.concatenate([-jnp.ones(d // 2), jnp.ones(d // 2), z]).astype(jnp.float32)
    return inv64, sgn64


def _odd_prep_kernel(h_ref, pos_ref, win_ref, wqb_ref, wiq_ref, wuk_ref, qn_ref, kvn_ref, ikg_ref, ikb_ref,
                     rope_ref, perm_ref, qa_ref, kh_ref, kv_ref, qi_ref, ki_ref, wi_ref):
    f32 = jnp.float32
    z = _dot(h_ref[0].astype(MXU_DT), win_ref[...])
    pos = pos_ref[0].astype(f32)
    rr = rope_ref[...]
    ang_q = pos * rr[0:1, :]
    cos_q, sin_q = jnp.cos(ang_q), jnp.sin(ang_q) * rr[1:2, :]
    ang_i = pos * rr[2:3, :]
    cos_i, sin_i = jnp.cos(ang_i), jnp.sin(ang_i) * rr[3:4, :]

    def rms(x, g):
        return x * lax.rsqrt(jnp.mean(jnp.square(x), axis=-1, keepdims=True) + LN_EPS) * g

    cq = rms(z[:, 0:C_QL], qn_ref[...])
    ckv = rms(z[:, C_QL:C_QL + C_KVL], kvn_ref[...])
    o = C_QL + C_KVL
    k_rope = (z[:, o:o + LANES] * cos_q + z[:, o + LANES:o + 2 * LANES] * sin_q).astype(kh_ref.dtype)
    ckvb = ckv.astype(kv_ref.dtype)
    kv_ref[0] = ckvb
    for hd in range(C_HEADS):
        kh_ref[0, hd, :, 0:C_DN] = _dot(ckvb, wuk_ref[hd]).astype(kh_ref.dtype)
        kh_ref[0, hd, :, C_DN:C_KDIM] = k_rope
    ik = z[:, o + 2 * LANES:o + 3 * LANES]
    real = _iota((1, LANES), 1) < C_IDX_DH
    mu = jnp.sum(ik, axis=-1, keepdims=True) / C_IDX_DH
    dlt = jnp.where(real, ik - mu, 0.0)
    var = jnp.sum(jnp.square(dlt), axis=-1, keepdims=True) / C_IDX_DH
    ki = dlt * lax.rsqrt(var + LN_EPS) * ikg_ref[...] + ikb_ref[...]
    ki = ki * cos_i + _dot_f32(ki, perm_ref[...]) * sin_i
    ki_ref[0] = ki.astype(ki_ref.dtype)
    wi_ref[0] = z[:, o + 3 * LANES:o + 4 * LANES] * (C_IDX_HEADS ** -0.5 * C_IDX_DH ** -0.5)
    cqb = cq.astype(MXU_DT)
    qf = _dot(cqb, wqb_ref[...])
    qi = _dot(cqb, wiq_ref[...])
    scale = (C_DN + C_DR) ** -0.5 * LOG2E
    nh = C_HEADS * LANES
    for hd in range(C_HEADS):
        cs = slice(hd * LANES, (hd + 1) * LANES)
        q_rope = qf[:, nh:2 * nh][:, cs] * cos_q + qf[:, 2 * nh:3 * nh][:, cs] * sin_q
        qa_ref[0, hd, :, 0:C_DN] = (qf[:, hd * C_DN:(hd + 1) * C_DN] * scale).astype(qa_ref.dtype)
        qa_ref[0, hd, :, C_DN:C_KDIM] = (q_rope * scale).astype(qa_ref.dtype)
        qi_h = qi[:, 0:nh][:, cs] * cos_i + qi[:, nh:2 * nh][:, cs] * sin_i
        qi_ref[0, hd] = qi_h.astype(qi_ref.dtype)


def _rot_cols(w, n_heads, dh, d):
    w = w.reshape(w.shape[0], n_heads, dh)
    h = d // 2
    return jnp.concatenate([w[..., h:d], w[..., 0:h], jnp.zeros_like(w[..., d:])], -1).reshape(w.shape[0], n_heads * dh)


def _odd_prep(h3, pos3, w_in, q_norm, kv_norm, w_qb, w_uk, w_iq, ik_g, ik_b, tm):
    B, T, D = h3.shape
    f32 = jnp.float32
    o = 0
    parts = []
    for s in (C_QL, C_KVL, C_DR, C_IDX_DH, C_IDX_HEADS):
        parts.append(w_in[:, o:o + s]); o += s
    w_cq, w_ckv, w_kr, w_ik, w_iw = parts
    pc = lambda w: _pad_cols(w, LANES)
    win = jnp.concatenate([w_cq, w_ckv, pc(w_kr), pc(_rot_cols(w_kr, 1, C_DR, C_DR)), pc(w_ik), pc(w_iw)],
                          -1).astype(MXU_DT)
    wq = w_qb.reshape(C_QL, C_HEADS, C_DN + C_DR)
    w_nope = wq[..., :C_DN].reshape(C_QL, C_HEADS * C_DN)
    w_rope = wq[..., C_DN:].reshape(C_QL, C_HEADS * C_DR)
    wqb = jnp.concatenate([w_nope, _pad_heads(w_rope, C_HEADS, C_DR),
                           _pad_heads(_rot_cols(w_rope, C_HEADS, C_DR, C_DR), C_HEADS, C_DR)], -1).astype(MXU_DT)
    wiq = jnp.concatenate([_pad_heads(w_iq, C_IDX_HEADS, C_IDX_DH),
                           _pad_heads(_rot_cols(w_iq, C_IDX_HEADS, C_IDX_DH, C_IDX_DR), C_IDX_HEADS, C_IDX_DH)],
                          -1).astype(MXU_DT)
    wuk = w_uk.transpose(1, 0, 2).astype(MXU_DT)
    inv_q, sgn_q = _rope_rows(C_DR)
    inv_i, sgn_i = _rope_rows(C_IDX_DR)
    rope = jnp.stack([jnp.tile(v, LANES // C_DR) for v in (inv_q, sgn_q, inv_i, sgn_i)])
    rope = jnp.concatenate([rope, jnp.zeros((SUBLANES - 4, LANES), f32)])
    hh = C_IDX_DR // 2
    src = np.arange(LANES)
    src[:hh] += hh
    src[hh:C_IDX_DR] -= hh
    perm = np.zeros((LANES, LANES), np.float32)
    perm[src, np.arange(LANES)] = 1.0
    ikg = _pad_cols(ik_g[None, :], LANES)
    ikb = _pad_cols(ik_b[None, :], LANES)
    out_shape = (
        jax.ShapeDtypeStruct((B, C_HEADS, T, C_KDIM), MXU_DT),
        jax.ShapeDtypeStruct((B, C_HEADS, T, C_KDIM), MXU_DT),
        jax.ShapeDtypeStruct((B, T, C_KVL), MXU_DT),
        jax.ShapeDtypeStruct((B, C_IDX_HEADS, T, LANES), MXU_DT),
        jax.ShapeDtypeStruct((B, T, LANES), MXU_DT),
        jax.ShapeDtypeStruct((B, T, LANES), f32),
    )
    row = lambda w: pl.BlockSpec((1, tm, w), lambda b, i: (b, i, 0))
    hm = lambda w: pl.BlockSpec((1, C_HEADS, tm, w), lambda b, i: (b, 0, i, 0))
    return pl.pallas_call(
        _odd_prep_kernel,
        grid=(B, T // tm),
        in_specs=[row(D), row(1), _const_spec(win.shape), _const_spec(wqb.shape), _const_spec(wiq.shape),
                  _const_spec(wuk.shape), _const_spec((1, C_QL)), _const_spec((1, C_KVL)),
                  _const_spec((1, LANES)), _const_spec((1, LANES)), _const_spec(rope.shape),
                  _const_spec(perm.shape)],
        out_specs=(hm(C_KDIM), hm(C_KDIM), row(C_KVL), hm(LANES), row(LANES), row(LANES)),
        out_shape=out_shape,
        compiler_params=_cparams("parallel", "parallel"),
    )(h3, pos3, win, wqb, wiq, wuk, q_norm[None, :], kv_norm[None, :], ikg, ikb, rope, jnp.asarray(perm))


def _dsa_kernel(qa_ref, qi_ref, wi_ref, kh_ref, kv_ref, ki_ref, wuv_ref, eye_ref, tri_ref, out_ref,
                key_scr, hi_scr, lo_scr, acc_scr, *, QB, KC, SUB, TOPK, HG):
    f32, i32, i16 = jnp.float32, jnp.int32, jnp.int16
    i = pl.program_id(1)
    s0 = i * QB
    H = C_HEADS
    NG = H // HG
    RG = HG * QB
    nch = (s0 + QB + KC - 1) // KC
    PK = PACKED_ROWS
    tq_row = s0 + _iota((1, QB), 1)
    w_t = wi_ref[0].T
    one, zero = jnp.ones((), MXU_DT), jnp.zeros((), MXU_DT)

    def idx_body(c, _):
        ks = pl.multiple_of(c * KC, KC)
        kic = ki_ref[0, pl.ds(ks, KC), :]
        isc = None
        for hd in range(C_IDX_HEADS):
            s = jnp.maximum(_dot_nt(kic, qi_ref[0, hd]), 0.0) * w_t[hd:hd + 1, :]
            isc = s if isc is None else isc + s
        isc = jnp.where(isc == 0.0, 0.0, isc)
        kpos = ks + _iota((KC, 1), 0)
        isc = jnp.where(kpos <= tq_row, isc, NEG)
        bits = lax.bitcast_convert_type(isc, i32)
        key = jnp.where(bits < 0, bits ^ jnp.int32(0x7FFFFFFF), bits)
        key_scr[c] = key
        k3 = key.reshape(KC // PK, PK, QB)
        hi_scr[c] = lax.shift_right_arithmetic(k3, 16).astype(i16)
        lo_scr[c] = ((k3 & 0xFFFF) - 32768).astype(i16)
        return 0

    lax.fori_loop(0, nch // 2, lambda c2, z: idx_body(2 * c2 + 1, idx_body(2 * c2, z)), 0)
    lax.cond(nch % 2 == 1, lambda z: idx_body(nch - 1, z), lambda z: z, 0)

    def rep16(v):
        return jnp.broadcast_to(v, (PK, QB)).astype(i16)[None]

    def select_threshold(nk):
        def count16(pred):
            accs = [jnp.zeros((PK, QB), MXU_DT) for _ in range(4)]
            for c in range(nk):
                x = jnp.where(pred(hi_scr[c], lo_scr[c]), one, zero)
                for r in range(KC // PK):
                    accs[r % len(accs)] = accs[r % len(accs)] + x[r]
            acc = (accs[0] + accs[1]) + (accs[2] + accs[3])
            return jnp.sum(acc.astype(f32), axis=0, keepdims=True)

        def bisect(pick, base):
            def body(b, t):
                cand = t + lax.shift_left(jnp.int32(1), 15 - b)
                c16 = rep16(cand)
                return jnp.where(base + count16(lambda h, l: pick(h, l) >= c16) >= TOPK, cand, t)
            return lax.fori_loop(0, 16, body, jnp.full((1, QB), -32768, i32))

        thi = bisect(lambda h, l: h, 0.0)
        thi16 = rep16(thi)
        for c in range(nk):
            lo_scr[c] = jnp.where(hi_scr[c] == thi16, lo_scr[c], jnp.full((), -32768, i16))
        n_hi = count16(lambda h, l: h > thi16)
        tlo = bisect(lambda h, l: l, n_hi)
        tlo16 = rep16(tlo)
        n_gt = n_hi + count16(lambda h, l: l > tlo16)
        n_eq = count16(lambda h, l: (h == thi16) & (l == tlo16))
        return thi, tlo, n_gt, n_eq

    n_variants = kv_ref.shape[1] // KC
    thi, tlo, n_gt, n_eq = lax.switch(nch - 1, [functools.partial(select_threshold, k + 1) for k in range(n_variants)])
    thr = thi * 65536 + (tlo + 32768)
    room = TOPK - n_gt

    @pl.when(jnp.max(jnp.where(n_eq > room, 1, 0)) > 0)
    def _():
        def body(c, before):
            key = key_scr[c]
            eq = key == thr
            seen = before + _dot(tri_ref[...], jnp.where(eq, 1.0, 0.0).astype(MXU_DT))
            key_scr[c] = jnp.where(eq & (seen > room), key - 1, key)
            return seen[KC - 1:KC, :]
        lax.fori_loop(0, nch, body, jnp.zeros((1, QB), f32))

    acc_scr[...] = jnp.zeros(acc_scr.shape, f32)

    def att_body(c, carry):
        ms, ls = list(carry[0]), list(carry[1])
        ks = pl.multiple_of(c * KC, KC)
        key = key_scr[c]
        kpos = ks + _iota((KC, 1), 0)
        sel_t = (key >= thr) & (kpos <= tq_row)
        keep = _dot_nt(eye_ref[...], jnp.where(sel_t, 1.0, 0.0).astype(MXU_DT))
        bias = jnp.where(keep > 0.5, 0.0, NEG)
        for u in range(KC // SUB):
            sub = pl.ds(pl.multiple_of(ks + u * SUB, SUB), SUB)
            kvc = kv_ref[0, sub, :]
            b_u = bias[:, u * SUB:(u + 1) * SUB][None]
            for g in range(NG):
                s = jnp.stack([_dot_nt(qa_ref[0, hd], kh_ref[0, hd, sub, :])
                               for hd in range(g * HG, (g + 1) * HG)]) + b_u
                m_new = jnp.maximum(ms[g], jnp.max(s, axis=-1, keepdims=True))
                p = jnp.exp2(s - m_new)
                alpha = jnp.exp2(ms[g] - m_new)
                ls[g] = alpha * ls[g] + jnp.sum(p, axis=-1, keepdims=True)
                ms[g] = m_new
                rows = slice(g * RG, (g + 1) * RG)
                pv = _dot(p.reshape(RG, SUB).astype(MXU_DT), kvc)
                acc_scr[rows] = alpha.reshape(RG, 1) * acc_scr[rows] + pv
        return tuple(ms), tuple(ls)

    m0 = tuple(jnp.full((HG, QB, 1), SOFTMAX_M0, f32) for _ in range(NG))
    l0 = tuple(jnp.zeros((HG, QB, 1), f32) for _ in range(NG))
    carry = lax.fori_loop(0, nch // 2, lambda c2, cr: att_body(2 * c2 + 1, att_body(2 * c2, cr)), (m0, l0))
    _, l_fin = lax.cond(nch % 2 == 1, lambda cr: att_body(nch - 1, cr), lambda cr: cr, carry)
    for g in range(NG):
        o_lat = (acc_scr[g * RG:(g + 1) * RG] / l_fin[g].reshape(RG, 1)).astype(MXU_DT)
        for k in range(HG):
            hd = g * HG + k
            out_ref[0, :, hd * C_DV:(hd + 1) * C_DV] = _dot(o_lat[k * QB:(k + 1) * QB], wuv_ref[hd]).astype(out_ref.dtype)


def _dsa(qa, kh, kv, qi, ki, wi, w_uv):
    B, H, T, _ = qa.shape
    QB = min(DSA_Q_BLOCK, T)
    KC = min(KEY_CHUNK, T)
    topk = min(C_TOPK, T // 4)
    wuv = w_uv.transpose(1, 0, 2).astype(MXU_DT)
    assert T // PACKED_ROWS <= 256
    eye = jnp.eye(QB, dtype=MXU_DT)
    tri = jnp.tril(jnp.ones((KC, KC), MXU_DT))
    kern = functools.partial(_dsa_kernel, QB=QB, KC=KC, SUB=min(KEY_SUB, KC), TOPK=topk, HG=DSA_HEAD_GROUP)
    half_words = pltpu.VMEM((T // KC, KC // PACKED_ROWS, PACKED_ROWS, QB), jnp.int16)
    return pl.pallas_call(
        kern,
        grid=(B, T // QB),
        in_specs=[pl.BlockSpec((1, H, QB, C_KDIM), lambda b, i: (b, 0, i, 0)),
                  pl.BlockSpec((1, H, QB, LANES), lambda b, i: (b, 0, i, 0)),
                  pl.BlockSpec((1, QB, LANES), lambda b, i: (b, i, 0)),
                  pl.BlockSpec((1, H, T, C_KDIM), lambda b, i: (b, 0, 0, 0), pipeline_mode=pl.Buffered(1)),
                  pl.BlockSpec((1, T, C_KVL), lambda b, i: (b, 0, 0)),
                  pl.BlockSpec((1, T, LANES), lambda b, i: (b, 0, 0)),
                  _const_spec(wuv.shape), _const_spec(eye.shape), _const_spec(tri.shape)],
        out_specs=pl.BlockSpec((1, QB, H * C_DV), lambda b, i: (b, i, 0)),
        out_shape=jax.ShapeDtypeStruct((B, T, H * C_DV), MXU_DT),
        scratch_shapes=[pltpu.VMEM((T // KC, KC, QB), jnp.int32),
                        half_words, half_words,
                        pltpu.VMEM((H * QB, C_KVL), jnp.float32)],
        compiler_params=_cparams("parallel", "arbitrary"),
    )(qa, qi, wi, kh, kv, ki, wuv, eye, tri)


def _post_kernel(*refs, n_mix, n_ff):
    h_ref = refs[0]
    mix = refs[1:1 + 2 * n_mix]
    g1, b1, w1_ref, w2_ref, g2, b2, wg_ref, p_ref, wp_ref, out_ref = refs[1 + 2 * n_mix:]
    h = h_ref[...]
    y = _dot(mix[0][...].astype(MXU_DT), mix[1][...])
    for k in range(1, n_mix):
        y = y + _dot(mix[2 * k][...].astype(MXU_DT), mix[2 * k + 1][...])
    h1 = _layer_norm(DN_ALPHA * h + y, g1[...], b1[...])
    h1b = h1.astype(MXU_DT)
    ff = D_FF // n_ff
    u = None
    for k in range(n_ff):
        a = jnp.square(jnp.maximum(_dot(h1b, w1_ref[:, k * ff:(k + 1) * ff]), 0.0))
        t = _dot(a.astype(MXU_DT), w2_ref[k * ff:(k + 1) * ff, :])
        u = t if u is None else u + t
    h2 = _layer_norm(DN_ALPHA * h1 + u, g2[...], b2[...])
    gate = jax.nn.sigmoid(_dot(h2.astype(MXU_DT), wg_ref[...]))
    out_ref[...] = h2 + gate * _dot(p_ref[0].astype(MXU_DT), wp_ref[...])


def _post(h2d, mixes, ln1_g, ln1_b, w1, w2, ln2_g, ln2_b, wg, p_all, layer, wp, tm):
    M, D = h2d.shape
    row = lambda w: pl.BlockSpec((tm, w), lambda i: (i, 0))
    vec = lambda v: v[None, :]
    in_specs = [row(D)]
    args = [h2d]
    for x, w in mixes:
        in_specs += [row(x.shape[1]), _const_spec(w.shape)]
        args += [x, w]
    in_specs += [_const_spec((1, D)), _const_spec((1, D)), _const_spec(w1.shape), _const_spec(w2.shape),
                 _const_spec((1, D)), _const_spec((1, D)), _const_spec(wg.shape),
                 pl.BlockSpec((1, tm, D_PLE), lambda i: (layer, i, 0)), _const_spec(wp.shape)]
    args += [vec(ln1_g), vec(ln1_b), w1, w2, vec(ln2_g), vec(ln2_b), wg, p_all, wp]
    kern = functools.partial(_post_kernel, n_mix=len(mixes), n_ff=4)
    return pl.pallas_call(
        kern,
        grid=(M // tm,),
        in_specs=in_specs,
        out_specs=row(D),
        out_shape=jax.ShapeDtypeStruct((M, D), jnp.float32),
        compiler_params=_cparams("parallel"),
    )(*args)


def kernel(x, p, positions, e_w_in, e_a_conv, e_a_i_b, e_a_f_b, e_a_norm, e_b_cmp_pos, e_b_cmp_w1, e_b_cmp_w2, e_b_g_b, e_w_out, o_w_in, o_q_norm, o_kv_norm, o_w_qb, o_w_uk, o_w_uv, o_w_iq, o_ik_g, o_ik_b, o_w_out, ln1_g, ln1_b, ln2_g, ln2_b, mlp_w1, mlp_w2, ple_gate_w, ple_w):
    B, T, D = x.shape
    M = B * T
    tm = min(ROW_TILE, T)
    h = x
    pos3 = positions[..., None]
    p_all = p.reshape(DEPTH, M, D_PLE)
    bf = lambda w: w.astype(MXU_DT)
    for i in range(DEPTH):
        j = i // 2
        if i % 2 == 0:
            qk, av, ao, small, bq, bc, bs = _even_proj(h, _even_w_in_aug(e_w_in[j]), tm)
            ya = _mlstm(qk, av, ao, small, e_a_conv[j], e_a_i_b[j], e_a_f_b[j], e_a_norm[j])
            cmp = _nsa_compress(bc, e_b_cmp_pos[j], e_b_cmp_w1[j], e_b_cmp_w2[j])
            yb = _nsa(bq, bs, cmp, small, e_b_g_b[j])
            w_out = e_w_out[j]
            mixes = [(ya.reshape(M, A_W), bf(w_out[:A_W])),
                     (yb.reshape(M, B_HEADS * LANES), bf(_pad_heads(w_out[A_W:], B_HEADS, B_DH, axis=0)))]
        else:
            qa, kh, kv, qi, ki, wi = _odd_prep(h, pos3, o_w_in[j], o_q_norm[j], o_kv_norm[j], o_w_qb[j], o_w_uk[j],
                                               o_w_iq[j], o_ik_g[j], o_ik_b[j], tm)
            o = _dsa(qa, kh, kv, qi, ki, wi, o_w_uv[j])
            mixes = [(o.reshape(M, C_HEADS * C_DV), bf(o_w_out[j]))]
        h = _post(h.reshape(M, D), mixes, ln1_g[i], ln1_b[i], bf(mlp_w1[i]), bf(mlp_w2[i]), ln2_g[i], ln2_b[i],
                  bf(ple_gate_w[i]), p_all, i, bf(ple_w[i]), min(MLP_ROW_TILE, T)).reshape(B, T, D)
    return h
```

```python
import functools

import numpy as np
import jax
import jax.numpy as jnp
from jax import lax
from jax.experimental import pallas as pl
from jax.experimental.pallas import tpu as pltpu

D_MODEL = 1024
DEPTH = 4
D_PLE = 256
D_FF = 4 * D_MODEL
DN_ALPHA = (2.0 * DEPTH) ** 0.25
LN_EPS = 1e-5
NEG = -1e30

A_HEADS = 4
A_DH = D_MODEL // 8
A_W = A_HEADS * A_DH
A_CONV = 4

B_HEADS = 8
B_DH = 64
B_KV = 2
B_HPG = B_HEADS // B_KV
B_CMP_LEN = 32
B_CMP_STRIDE = 16
B_CMP_HID = 128
B_SEL_BLK = 64
B_SEL_N = 16
B_WIN = 512

C_HEADS = 8
C_DN = 128
C_DR = 64
C_DV = 128
C_QL = 512
C_KVL = 256
C_IDX_HEADS = 8
C_IDX_DH = 64
C_IDX_DR = 32
C_TOPK = 256
ROPE_BASE = 10000.0

LANES = 128
SUBLANES = 8
PACKED_ROWS = 16
VMEM_LIMIT_BYTES = 56 * 2**20
MXU_DT = jnp.bfloat16
INT_MIN = -2**31

MLSTM_CHUNK = 256
MLSTM_SEQS = 1
ROW_TILE = 512
MLP_ROW_TILE = 512
Q_BLOCK = 256
DSA_Q_BLOCK = 256
DSA_HEAD_GROUP = 4
KEY_CHUNK = 512
KEY_SUB = 512
C_KDIM = C_DN + LANES

LOG2E = 1.4426950408889634
SOFTMAX_M0 = 0.5 * NEG
SOFTMAX_TINY = 1e-30


def _cparams(*sem):
    return pltpu.CompilerParams(dimension_semantics=sem, vmem_limit_bytes=VMEM_LIMIT_BYTES)


def _const_spec(shape):
    nd = len(shape)
    return pl.BlockSpec(shape, lambda *_: (0,) * nd, pipeline_mode=pl.Buffered(1))


def _dot(a, b):
    return jnp.dot(a, b, preferred_element_type=jnp.float32)


def _dot_nt(a, b):
    return lax.dot_general(a, b, (((1,), (1,)), ((), ())), preferred_element_type=jnp.float32)


def _dot_tn(a, b):
    return lax.dot_general(a, b, (((0,), (0,)), ((), ())), preferred_element_type=jnp.float32)


def _dot_f32(a, b):
    return jnp.dot(a, b, preferred_element_type=jnp.float32, precision=lax.Precision.HIGHEST)


def _layer_norm(x, g, b):
    mu = jnp.mean(x, axis=-1, keepdims=True)
    var = jnp.mean(jnp.square(x - mu), axis=-1, keepdims=True)
    return (x - mu) * lax.rsqrt(var + LN_EPS) * g + b


def _masked_softmax2(s, bias):
    s = s + bias
    m = jnp.maximum(jnp.max(s, axis=-1, keepdims=True), SOFTMAX_M0)
    e = jnp.exp2(s - m)
    return e / jnp.maximum(jnp.sum(e, axis=-1, keepdims=True), SOFTMAX_TINY)


def _iota(shape, dim):
    return lax.broadcasted_iota(jnp.int32, shape, dim)


def _pad_heads(w, n_heads, dh, axis=-1):
    axis = axis % w.ndim
    shp = w.shape[:axis] + (n_heads, dh) + w.shape[axis + 1:]
    w = w.reshape(shp)
    pad = [(0, 0)] * w.ndim
    pad[axis + 1] = (0, LANES - dh)
    w = jnp.pad(w, pad)
    return w.reshape(shp[:axis] + (n_heads * LANES,) + shp[axis + 2:])


def _pad_cols(w, width):
    return jnp.pad(w, [(0, 0)] * (w.ndim - 1) + [(0, width - w.shape[-1])])


def _even_proj_kernel(h_ref, w_ref, blk_ref, qk_ref, av_ref, ao_ref, sm_ref, bq_ref, bc_ref, bs_ref):
    z = _dot(h_ref[0].astype(MXU_DT), w_ref[...])
    o = 0
    qk_ref[0] = z[:, o:o + 2 * A_W]; o += 2 * A_W
    av_ref[0] = z[:, o:o + A_W].astype(av_ref.dtype); o += A_W
    ao_ref[0] = z[:, o:o + A_W]; o += A_W
    sm_ref[0] = z[:, o:o + LANES]; o += LANES
    for hd in range(B_HEADS):
        bq_ref[0, hd] = z[:, o:o + LANES].astype(bq_ref.dtype); o += LANES
    for j in range(2 * B_KV):
        bc_ref[0, j] = z[:, o:o + LANES]; o += LANES
    for j in range(4 * B_KV):
        zj = z[:, o:o + LANES]; o += LANES
        if j < B_KV:
            zj = zj + blk_ref[...]
        bs_ref[0, j] = zj.astype(bs_ref.dtype)


def _even_proj(h3, w_aug, tm):
    B, T, D = h3.shape
    n = w_aug.shape[1]
    f32 = jnp.float32
    out_shape = (
        jax.ShapeDtypeStruct((B, T, 2 * A_W), f32),
        jax.ShapeDtypeStruct((B, T, A_W), MXU_DT),
        jax.ShapeDtypeStruct((B, T, A_W), f32),
        jax.ShapeDtypeStruct((B, T, LANES), f32),
        jax.ShapeDtypeStruct((B, B_HEADS, T, LANES), MXU_DT),
        jax.ShapeDtypeStruct((B, 2 * B_KV, T, LANES), f32),
        jax.ShapeDtypeStruct((B, 4 * B_KV, T, LANES), MXU_DT),
    )
    row = lambda w: pl.BlockSpec((1, tm, w), lambda b, i: (b, i, 0))
    hm = lambda nh: pl.BlockSpec((1, nh, tm, LANES), lambda b, i: (b, 0, i, 0))
    assert T // B_SEL_BLK <= LANES - B_DH
    blk = (np.arange(LANES)[None, :] == B_DH + np.arange(T)[:, None] // B_SEL_BLK).astype(np.float32)
    return pl.pallas_call(
        _even_proj_kernel,
        grid=(B, T // tm),
        in_specs=[row(D), _const_spec((D, n)), pl.BlockSpec((tm, LANES), lambda b, i: (i, 0))],
        out_specs=(row(2 * A_W), row(A_W), row(A_W), row(LANES), hm(B_HEADS), hm(2 * B_KV), hm(4 * B_KV)),
        out_shape=out_shape,
        compiler_params=_cparams("parallel", "parallel"),
    )(h3, w_aug, jnp.asarray(blk))


def _even_w_in_aug(w_in):
    sizes = (A_W, A_W, A_W, A_W, A_HEADS, A_HEADS, B_HEADS * B_DH) + (B_KV * B_DH,) * 6 + (3 * B_HEADS,)
    parts, o = [], 0
    for s in sizes:
        parts.append(w_in[:, o:o + s]); o += s
    aq, ak, av, ao, ai, af, bq, bkc, bvc, bks, bvs, bkw, bvw, bg = parts
    small = _pad_cols(jnp.concatenate([ai, af, bg], -1), LANES)
    ph = lambda w: _pad_heads(w, B_KV, B_DH)
    bq = bq * (B_DH ** -0.5 * LOG2E)
    cols = [aq, ak, av, ao, small, _pad_heads(bq, B_HEADS, B_DH),
            ph(bkc), ph(bvc), ph(bks), ph(bvs), ph(bkw), ph(bvw)]
    return jnp.concatenate(cols, -1).astype(MXU_DT)


def _mlstm_kernel(qk_ref, v_ref, o_ref, gc_ref, gr_ref, cw_ref, bc_ref, br_ref, ng_ref, tri_ref, out_ref,
                  xs_scr, ct_scr, n_scr, m_scr, *, L, NB):
    c = pl.program_id(1)
    f32 = jnp.float32
    W2 = 2 * A_W

    @pl.when(c == 0)
    def _():
        xs_scr[:, 0:SUBLANES, :] = jnp.zeros((NB, SUBLANES, W2), f32)
        ct_scr[...] = jnp.zeros(ct_scr.shape, f32)
        n_scr[...] = jnp.zeros(n_scr.shape, f32)
        m_scr[...] = jnp.zeros(m_scr.shape, f32)

    tri = tri_ref[...]
    causal = _iota((L, L), 1) <= _iota((L, L), 0)
    cw = cw_ref[...]
    for bb in range(NB):
        _mlstm_chunk(bb, qk_ref, v_ref, o_ref, gc_ref, gr_ref, cw, bc_ref, br_ref, ng_ref, tri, causal, out_ref,
                     xs_scr, ct_scr, n_scr, m_scr, L)


def _mlstm_chunk(bb, qk_ref, v_ref, o_ref, gc_ref, gr_ref, cw, bc_ref, br_ref, ng_ref, tri, causal, out_ref,
                 xs_scr, ct_scr, n_scr, m_scr, L):
    f32 = jnp.float32
    xs_scr[bb, SUBLANES:SUBLANES + L, :] = qk_ref[bb]
    base = SUBLANES - (A_CONV - 1)
    acc = xs_scr[bb, base:base + L, :] * cw[0:1, :]
    for j in range(1, A_CONV):
        acc = acc + xs_scr[bb, base + j:base + j + L, :] * cw[j:j + 1, :]
    qk = acc * jax.nn.sigmoid(acc)
    xs_scr[bb, 0:SUBLANES, :] = xs_scr[bb, L:L + SUBLANES, :]

    gc = gc_ref[bb] + bc_ref[...]
    gr = gr_ref[bb, 0] + br_ref[...]
    ig_c = gc[:, 0:A_HEADS]
    b_c = _dot_f32(tri, jax.nn.log_sigmoid(gc[:, A_HEADS:2 * A_HEADS]))
    ig_r = gr[0:A_HEADS, :]
    b_r = _dot_f32(jax.nn.log_sigmoid(gr[A_HEADS:2 * A_HEADS, :]), tri.T)

    for hd in range(A_HEADS):
        sl = slice(hd * A_DH, (hd + 1) * A_DH)
        q_h = (qk[:, sl] * A_DH ** -0.5).astype(MXU_DT)
        k_f = qk[:, A_W + hd * A_DH:A_W + (hd + 1) * A_DH]
        k_h = k_f.astype(MXU_DT)
        v_h = v_ref[bb][:, sl]
        bi = b_c[:, hd:hd + 1]
        ic = ig_c[:, hd:hd + 1]
        dmat = jnp.where(causal, bi - b_r[hd:hd + 1, :] + ig_r[hd:hd + 1, :], NEG)
        m_prev = m_scr[bb, hd][:, 0:1]
        m_inter = bi + m_prev
        m_t = jnp.maximum(m_inter, jnp.max(dmat, axis=1, keepdims=True))
        e_inter = jnp.exp(m_inter - m_t)
        s = _dot_nt(q_h, k_h) * jnp.exp(dmat - m_t)
        ct = ct_scr[bb, hd]
        nrow = n_scr[bb, hd]
        num = e_inter * _dot(q_h, ct.astype(MXU_DT)) + _dot(s.astype(MXU_DT), v_h)
        den = e_inter * jnp.sum(q_h.astype(f32) * nrow, axis=1, keepdims=True) + jnp.sum(s, axis=1, keepdims=True)
        hc = num / jnp.maximum(jnp.abs(den), jnp.exp(-m_t))
        b_last = bi[L - 1:L, :]
        dec = b_last - bi + ic
        m_new = jnp.maximum(b_last + m_prev, jnp.max(dec, axis=0, keepdims=True))
        wgt = jnp.exp(dec - m_new)
        e_st = jnp.exp(b_last + m_prev - m_new)
        kw = k_f * wgt
        ct_scr[bb, hd] = e_st * ct + _dot_tn(kw.astype(MXU_DT), v_h)
        n_scr[bb, hd] = e_st * nrow + jnp.sum(kw, axis=0, keepdims=True)
        m_scr[bb, hd] = jnp.broadcast_to(m_new, (1, LANES))
        mu = jnp.mean(hc, axis=1, keepdims=True)
        var = jnp.mean(jnp.square(hc - mu), axis=1, keepdims=True)
        hn = (hc - mu) * lax.rsqrt(var + LN_EPS) * ng_ref[:, sl]
        out_ref[bb, :, sl] = (hn * jax.nn.sigmoid(o_ref[bb][:, sl])).astype(out_ref.dtype)


def _mlstm(qk, av, ao, small, conv_w, i_b, f_b, norm_g):
    B, T, _ = qk.shape
    L = min(MLSTM_CHUNK, T)
    N = T // L
    f32 = jnp.float32
    gates = small[..., 0:2 * A_HEADS]
    gates_r = gates.reshape(B, N, L, 2 * A_HEADS).transpose(0, 1, 3, 2)
    bias = jnp.concatenate([i_b, f_b]).astype(f32)
    tri = jnp.tril(jnp.ones((L, L), f32))
    NB = MLSTM_SEQS if B % MLSTM_SEQS == 0 else 1
    row = lambda w: pl.BlockSpec((NB, L, w), lambda b, c: (b, c, 0))
    kern = functools.partial(_mlstm_kernel, L=L, NB=NB)
    return pl.pallas_call(
        kern,
        grid=(B // NB, N),
        in_specs=[row(2 * A_W), row(A_W), row(A_W), row(2 * A_HEADS),
                  pl.BlockSpec((NB, 1, 2 * A_HEADS, L), lambda b, c: (b, c, 0, 0)),
                  _const_spec((A_CONV, 2 * A_W)), _const_spec((1, 2 * A_HEADS)), _const_spec((2 * A_HEADS, 1)),
                  _const_spec((1, A_W)), _const_spec((L, L))],
        out_specs=row(A_W),
        out_shape=jax.ShapeDtypeStruct((B, T, A_W), MXU_DT),
        scratch_shapes=[pltpu.VMEM((NB, L + 2 * SUBLANES, 2 * A_W), f32),
                        pltpu.VMEM((NB, A_HEADS, A_DH, A_DH), f32),
                        pltpu.VMEM((NB, A_HEADS, 1, A_DH), f32),
                        pltpu.VMEM((NB, A_HEADS, 1, LANES), f32)],
        compiler_params=_cparams("parallel", "arbitrary"),
    )(qk, av, ao, gates, gates_r, conv_w, bias[None, :], bias[:, None], norm_g[None, :], tri)


def _nsa_cmp_kernel(x_ref, w1a_ref, w1b_ref, w2_ref, pos_ref, out_ref):
    n = x_ref.shape[2] // B_CMP_STRIDE
    half = B_CMP_STRIDE * LANES
    for j in range(2):
        bias = (_dot(pos_ref[j, :, 0:half].astype(MXU_DT), w1a_ref[j])
                + _dot(pos_ref[j, :, half:2 * half].astype(MXU_DT), w1b_ref[j]))[0:1, :]
        for g in range(B_KV):
            u = jnp.concatenate([x_ref[0, j * B_KV + g, pl.ds(r, n, stride=B_CMP_STRIDE), :]
                                 for r in range(B_CMP_STRIDE)], axis=1).astype(MXU_DT)
            a = _dot(u, w1a_ref[j])
            bm = _dot(u, w1b_ref[j])
            pre = a + pltpu.roll(bm, n - 1, 0) + bias
            hid = jax.nn.gelu(pre)
            out_ref[0, j * B_KV + g] = _dot(hid.astype(MXU_DT), w2_ref[j])


def _nsa_compress(bc, cmp_pos, cmp_w1, cmp_w2):
    B, _, T, _ = bc.shape
    nblk = T // B_CMP_STRIDE
    half = B_CMP_STRIDE * LANES
    w1 = jnp.pad(cmp_w1.reshape(2, B_CMP_LEN, B_DH, B_CMP_HID), ((0, 0), (0, 0), (0, LANES - B_DH), (0, 0)))
    w1 = w1.reshape(2, B_CMP_LEN * LANES, B_CMP_HID).astype(MXU_DT)
    w1a, w1b = w1[:, :half], w1[:, half:]
    w2 = _pad_cols(cmp_w2, LANES).astype(MXU_DT)
    pos = jnp.pad(cmp_pos, ((0, 0), (0, 0), (0, LANES - B_DH))).reshape(2, 1, B_CMP_LEN * LANES)
    pos = jnp.broadcast_to(pos, (2, SUBLANES, B_CMP_LEN * LANES))
    return pl.pallas_call(
        _nsa_cmp_kernel,
        grid=(B,),
        in_specs=[pl.BlockSpec((1, 2 * B_KV, T, LANES), lambda b: (b, 0, 0, 0)),
                  _const_spec(w1a.shape), _const_spec(w1b.shape), _const_spec(w2.shape), _const_spec(pos.shape)],
        out_specs=pl.BlockSpec((1, 2 * B_KV, nblk, LANES), lambda b: (b, 0, 0, 0)),
        out_shape=jax.ShapeDtypeStruct((B, 2 * B_KV, nblk, LANES), jnp.float32),
        compiler_params=_cparams("parallel"),
    )(bc, w1a, w1b, w2, pos)


def _nsa_kernel(q_ref, kv_ref, cmp_ref, sm_ref, gb_ref, ovt_ref, out_ref, acc_scr, ocmp_scr, qaug_scr,
                *, QB, KC, WL, NSB, NSEL):
    f32 = jnp.float32
    i = pl.program_id(1)
    s0 = i * QB
    R = B_HPG * QB
    tq_col = s0 + _iota((QB, 1), 0)
    tq_row = s0 + _iota((1, QB), 1)
    nch = (s0 + QB + KC - 1) // KC
    gates = jax.nn.sigmoid(sm_ref[0] + gb_ref[...])
    ncmp = cmp_ref.shape[2]
    cmp_end = _iota((1, ncmp), 1) * B_CMP_STRIDE + (B_CMP_LEN - 1)
    cmp_bias = jnp.where(cmp_end <= tq_col, 0.0, NEG)[None]
    jb = _iota((NSB, 1), 0)
    cur = lax.shift_right_logical(tq_row, int(np.log2(B_SEL_BLK)))
    forced = (jb == 0) | (jb == cur) | (jb == cur - 1)
    valid = jb * B_SEL_BLK <= tq_row
    wstart = pl.multiple_of(jnp.maximum(s0 + QB - WL, 0), QB)
    wpos = wstart + _iota((1, WL), 1)
    win_bias = jnp.where((wpos <= tq_col) & (wpos > tq_col - B_WIN), 0.0, NEG)[None]

    for g in range(B_KV):
        qs = q_ref[0, g * B_HPG:(g + 1) * B_HPG].reshape(R, LANES)
        kcm = cmp_ref[0, g].astype(MXU_DT)
        vcm = cmp_ref[0, B_KV + g].astype(MXU_DT)
        p_cmp = _masked_softmax2(_dot_nt(qs, kcm).reshape(B_HPG, QB, ncmp), cmp_bias)
        o_cmp = _dot(p_cmp.reshape(R, ncmp).astype(MXU_DT), vcm)
        kw_ = kv_ref[0, 2 * B_KV + g, pl.ds(wstart, WL), :]
        vw_ = kv_ref[0, 3 * B_KV + g, pl.ds(wstart, WL), :]
        p_win = _masked_softmax2(_dot_nt(qs, kw_).reshape(B_HPG, QB, WL), win_bias)
        o_win = _dot(p_win.reshape(R, WL).astype(MXU_DT), vw_)
        for hd in range(B_HPG):
            c0 = 2 * A_HEADS + (g * B_HPG + hd) * 3
            rs = slice(hd * QB, (hd + 1) * QB)
            ocmp_scr[g * R + hd * QB:g * R + (hd + 1) * QB] = (gates[:, c0:c0 + 1] * o_cmp[rs]
                                                               + gates[:, c0 + 2:c0 + 3] * o_win[rs])
        psum = jnp.sum(p_cmp, axis=0)
        imp_t = lax.dot_general(ovt_ref[...], psum, (((1,), (1,)), ((), ())),
                                preferred_element_type=f32, precision=lax.Precision.HIGHEST)
        sc = jnp.where(forced, 1e6, imp_t)
        sc = jnp.where(valid, sc, NEG)
        rank = jnp.zeros((NSB, QB), f32)
        for j in range(NSB):
            cj = jnp.broadcast_to(sc[j:j + 1, :], (NSB, QB))
            ahead = (cj > sc) | ((cj == sc) & (jb > j))
            rank = rank + jnp.where(ahead, 1.0, 0.0)
        drop = jnp.where(rank < NSEL, 0.0, NEG)
        pad = [jnp.zeros((B_DH, QB), f32), drop]
        if B_DH + NSB < LANES:
            pad.append(jnp.zeros((LANES - B_DH - NSB, QB), f32))
        drop_t = jnp.concatenate(pad, axis=0).T.astype(MXU_DT)
        for hd in range(B_HPG):
            rows = slice((g * B_HPG + hd) * QB, (g * B_HPG + hd + 1) * QB)
            qaug_scr[rows] = q_ref[0, g * B_HPG + hd] + drop_t

    acc_scr[...] = jnp.zeros(acc_scr.shape, f32)

    def sel_body(c, carry, diagonal=False):
        ms, ls = list(carry[0]), list(carry[1])
        ks = pl.multiple_of(c * KC, KC)
        for g in range(B_KV):
            kc_ = kv_ref[0, 0 * B_KV + g, pl.ds(ks, KC), :]
            vc_ = kv_ref[0, 1 * B_KV + g, pl.ds(ks, KC), :]
            s3 = _dot_nt(qaug_scr[g * R:(g + 1) * R], kc_).reshape(B_HPG, QB, KC)
            if diagonal:
                s3 = s3 + jnp.where(ks + _iota((1, KC), 1) <= tq_col, 0.0, NEG)[None]
            m_new = jnp.maximum(ms[g], jnp.max(s3, axis=-1, keepdims=True))
            p = jnp.exp2(s3 - m_new)
            alpha = jnp.exp2(ms[g] - m_new)
            ls[g] = alpha * ls[g] + jnp.sum(p, axis=-1, keepdims=True)
            ms[g] = m_new
            rows = slice(g * R, (g + 1) * R)
            acc_scr[rows] = alpha.reshape(R, 1) * acc_scr[rows] + _dot(p.reshape(R, KC).astype(MXU_DT), vc_)
        return tuple(ms), tuple(ls)

    m0 = tuple(jnp.full((B_HPG, QB, 1), SOFTMAX_M0, f32) for _ in range(B_KV))
    l0 = tuple(jnp.zeros((B_HPG, QB, 1), f32) for _ in range(B_KV))
    nfull = nch - 1
    carry = lax.fori_loop(0, nfull // 2, lambda c2, cr: sel_body(2 * c2 + 1, sel_body(2 * c2, cr)), (m0, l0))
    carry = lax.cond(nfull % 2 == 1, lambda cr: sel_body(nfull - 1, cr), lambda cr: cr, carry)
    _, l_fin = sel_body(nfull, carry, diagonal=True)

    for g in range(B_KV):
        rows = slice(g * R, (g + 1) * R)
        o_sel = acc_scr[rows] / l_fin[g].reshape(R, 1)
        o_two = ocmp_scr[rows]
        for hd in range(B_HPG):
            c0 = 2 * A_HEADS + (g * B_HPG + hd) * 3
            rs = slice(hd * QB, (hd + 1) * QB)
            o = o_two[rs] + gates[:, c0 + 1:c0 + 2] * o_sel[rs]
            col = (g * B_HPG + hd) * LANES
            out_ref[0, :, col:col + LANES] = o.astype(out_ref.dtype)


def _nsa(bq, bs, cmp, small, g_b):
    B, _, T, _ = bq.shape
    QB = min(Q_BLOCK, T)
    KC = min(KEY_CHUNK, T)
    assert KC % QB == 0
    WL = min(B_WIN + QB, T)
    NSB = T // B_SEL_BLK
    NSEL = min(B_SEL_N, NSB)
    ncmp = cmp.shape[2]
    f32 = jnp.float32
    M = (T - B_CMP_LEN) // B_CMP_STRIDE + 1
    assert NSB % SUBLANES == 0 and NSB <= LANES
    mi, jj = np.arange(ncmp)[None, :], np.arange(NSB)[:, None]
    ovt = ((mi * B_CMP_STRIDE < (jj + 1) * B_SEL_BLK) & (mi * B_CMP_STRIDE + B_CMP_LEN > jj * B_SEL_BLK)
           & (mi < M)).astype(np.float32)
    gb = jnp.zeros((1, LANES), f32).at[0, 2 * A_HEADS:2 * A_HEADS + 3 * B_HEADS].set(g_b)
    kern = functools.partial(_nsa_kernel, QB=QB, KC=KC, WL=WL, NSB=NSB, NSEL=NSEL)
    return pl.pallas_call(
        kern,
        grid=(B, T // QB),
        in_specs=[pl.BlockSpec((1, B_HEADS, QB, LANES), lambda b, i: (b, 0, i, 0)),
                  pl.BlockSpec((1, 4 * B_KV, T, LANES), lambda b, i: (b, 0, 0, 0)),
                  pl.BlockSpec((1, 2 * B_KV, ncmp, LANES), lambda b, i: (b, 0, 0, 0)),
                  pl.BlockSpec((1, QB, LANES), lambda b, i: (b, i, 0)),
                  _const_spec((1, LANES)), _const_spec(ovt.shape)],
        out_specs=pl.BlockSpec((1, QB, B_HEADS * LANES), lambda b, i: (b, i, 0)),
        out_shape=jax.ShapeDtypeStruct((B, T, B_HEADS * LANES), MXU_DT),
        scratch_shapes=[pltpu.VMEM((B_HEADS * QB, LANES), f32), pltpu.VMEM((B_HEADS * QB, LANES), f32),
                        pltpu.VMEM((B_HEADS * QB, LANES), MXU_DT)],
        compiler_params=_cparams("parallel", "arbitrary"),
    )(bq, bs, cmp, small, gb, jnp.asarray(ovt))


def _rope_rows(d):
    inv = ROPE_BASE ** (-jnp.arange(0, d, 2, dtype=jnp.float32) / d)
    z = jnp.zeros((C_DR - d,), jnp.float32)
    inv64 = jnp.concatenate([inv, inv, z])
    sgn64 = jnp.concatenate([-jnp.ones(d // 2), jnp.ones(d // 2), z]).astype(jnp.float32)
    return inv64, sgn64


def _odd_prep_kernel(h_ref, pos_ref, win_ref, wqb_ref, wiq_ref, wuk_ref, qn_ref, kvn_ref, ikg_ref, ikb_ref,
                     rope_ref, perm_ref, qa_ref, kh_ref, kv_ref, qi_ref, ki_ref, wi_ref):
    f32 = jnp.float32
    z = _dot(h_ref[0].astype(MXU_DT), win_ref[...])
    pos = pos_ref[0].astype(f32)
    rr = rope_ref[...]
    ang_q = pos * rr[0:1, :]
    cos_q, sin_q = jnp.cos(ang_q), jnp.sin(ang_q) * rr[1:2, :]
    ang_i = pos * rr[2:3, :]
    cos_i, sin_i = jnp.cos(ang_i), jnp.sin(ang_i) * rr[3:4, :]

    def rms(x, g):
        return x * lax.rsqrt(jnp.mean(jnp.square(x), axis=-1, keepdims=True) + LN_EPS) * g

    cq = rms(z[:, 0:C_QL], qn_ref[...])
    ckv = rms(z[:, C_QL:C_QL + C_KVL], kvn_ref[...])
    o = C_QL + C_KVL
    k_rope = (z[:, o:o + LANES] * cos_q + z[:, o + LANES:o + 2 * LANES] * sin_q).astype(kh_ref.dtype)
    ckvb = ckv.astype(kv_ref.dtype)
    kv_ref[0] = ckvb
    for hd in range(C_HEADS):
        kh_ref[0, hd, :, 0:C_DN] = _dot(ckvb, wuk_ref[hd]).astype(kh_ref.dtype)
        kh_ref[0, hd, :, C_DN:C_KDIM] = k_rope
    ik = z[:, o + 2 * LANES:o + 3 * LANES]
    real = _iota((1, LANES), 1) < C_IDX_DH
    mu = jnp.sum(ik, axis=-1, keepdims=True) / C_IDX_DH
    dlt = jnp.where(real, ik - mu, 0.0)
    var = jnp.sum(jnp.square(dlt), axis=-1, keepdims=True) / C_IDX_DH
    ki = dlt * lax.rsqrt(var + LN_EPS) * ikg_ref[...] + ikb_ref[...]
    ki = ki * cos_i + _dot_f32(ki, perm_ref[...]) * sin_i
    ki_ref[0] = ki.astype(ki_ref.dtype)
    wi_ref[0] = z[:, o + 3 * LANES:o + 4 * LANES] * (C_IDX_HEADS ** -0.5 * C_IDX_DH ** -0.5)
    cqb = cq.astype(MXU_DT)
    qf = _dot(cqb, wqb_ref[...])
    qi = _dot(cqb, wiq_ref[...])
    scale = (C_DN + C_DR) ** -0.5 * LOG2E
    nh = C_HEADS * LANES
    for hd in range(C_HEADS):
        cs = slice(hd * LANES, (hd + 1) * LANES)
        q_rope = qf[:, nh:2 * nh][:, cs] * cos_q + qf[:, 2 * nh:3 * nh][:, cs] * sin_q
        qa_ref[0, hd, :, 0:C_DN] = (qf[:, hd * C_DN:(hd + 1) * C_DN] * scale).astype(qa_ref.dtype)
        qa_ref[0, hd, :, C_DN:C_KDIM] = (q_rope * scale).astype(qa_ref.dtype)
        qi_h = qi[:, 0:nh][:, cs] * cos_i + qi[:, nh:2 * nh][:, cs] * sin_i
        qi_ref[0, hd] = qi_h.astype(qi_ref.dtype)


def _rot_cols(w, n_heads, dh, d):
    w = w.reshape(w.shape[0], n_heads, dh)
    h = d // 2
    return jnp.concatenate([w[..., h:d], w[..., 0:h], jnp.zeros_like(w[..., d:])], -1).reshape(w.shape[0], n_heads * dh)


def _odd_prep(h3, pos3, w_in, q_norm, kv_norm, w_qb, w_uk, w_iq, ik_g, ik_b, tm):
    B, T, D = h3.shape
    f32 = jnp.float32
    o = 0
    parts = []
    for s in (C_QL, C_KVL, C_DR, C_IDX_DH, C_IDX_HEADS):
        parts.append(w_in[:, o:o + s]); o += s
    w_cq, w_ckv, w_kr, w_ik, w_iw = parts
    pc = lambda w: _pad_cols(w, LANES)
    win = jnp.concatenate([w_cq, w_ckv, pc(w_kr), pc(_rot_cols(w_kr, 1, C_DR, C_DR)), pc(w_ik), pc(w_iw)],
                          -1).astype(MXU_DT)
    wq = w_qb.reshape(C_QL, C_HEADS, C_DN + C_DR)
    w_nope = wq[..., :C_DN].reshape(C_QL, C_HEADS * C_DN)
    w_rope = wq[..., C_DN:].reshape(C_QL, C_HEADS * C_DR)
    wqb = jnp.concatenate([w_nope, _pad_heads(w_rope, C_HEADS, C_DR),
                           _pad_heads(_rot_cols(w_rope, C_HEADS, C_DR, C_DR), C_HEADS, C_DR)], -1).astype(MXU_DT)
    wiq = jnp.concatenate([_pad_heads(w_iq, C_IDX_HEADS, C_IDX_DH),
                           _pad_heads(_rot_cols(w_iq, C_IDX_HEADS, C_IDX_DH, C_IDX_DR), C_IDX_HEADS, C_IDX_DH)],
                          -1).astype(MXU_DT)
    wuk = w_uk.transpose(1, 0, 2).astype(MXU_DT)
    inv_q, sgn_q = _rope_rows(C_DR)
    inv_i, sgn_i = _rope_rows(C_IDX_DR)
    rope = jnp.stack([jnp.tile(v, LANES // C_DR) for v in (inv_q, sgn_q, inv_i, sgn_i)])
    rope = jnp.concatenate([rope, jnp.zeros((SUBLANES - 4, LANES), f32)])
    hh = C_IDX_DR // 2
    src = np.arange(LANES)
    src[:hh] += hh
    src[hh:C_IDX_DR] -= hh
    perm = np.zeros((LANES, LANES), np.float32)
    perm[src, np.arange(LANES)] = 1.0
    ikg = _pad_cols(ik_g[None, :], LANES)
    ikb = _pad_cols(ik_b[None, :], LANES)
    out_shape = (
        jax.ShapeDtypeStruct((B, C_HEADS, T, C_KDIM), MXU_DT),
        jax.ShapeDtypeStruct((B, C_HEADS, T, C_KDIM), MXU_DT),
        jax.ShapeDtypeStruct((B, T, C_KVL), MXU_DT),
        jax.ShapeDtypeStruct((B, C_IDX_HEADS, T, LANES), MXU_DT),
        jax.ShapeDtypeStruct((B, T, LANES), MXU_DT),
        jax.ShapeDtypeStruct((B, T, LANES), f32),
    )
    row = lambda w: pl.BlockSpec((1, tm, w), lambda b, i: (b, i, 0))
    hm = lambda w: pl.BlockSpec((1, C_HEADS, tm, w), lambda b, i: (b, 0, i, 0))
    return pl.pallas_call(
        _odd_prep_kernel,
        grid=(B, T // tm),
        in_specs=[row(D), row(1), _const_spec(win.shape), _const_spec(wqb.shape), _const_spec(wiq.shape),
                  _const_spec(wuk.shape), _const_spec((1, C_QL)), _const_spec((1, C_KVL)),
                  _const_spec((1, LANES)), _const_spec((1, LANES)), _const_spec(rope.shape),
                  _const_spec(perm.shape)],
        out_specs=(hm(C_KDIM), hm(C_KDIM), row(C_KVL), hm(LANES), row(LANES), row(LANES)),
        out_shape=out_shape,
        compiler_params=_cparams("parallel", "parallel"),
    )(h3, pos3, win, wqb, wiq, wuk, q_norm[None, :], kv_norm[None, :], ikg, ikb, rope, jnp.asarray(perm))


def _dsa_kernel(qa_ref, qi_ref, wi_ref, kh_ref, kv_ref, ki_ref, wuv_ref, eye_ref, tri_ref, out_ref,
                key_scr, hi_scr, lo_scr, acc_scr, *, QB, KC, SUB, TOPK, HG):
    f32, i32, i16 = jnp.float32, jnp.int32, jnp.int16
    i = pl.program_id(1)
    s0 = i * QB
    H = C_HEADS
    NG = H // HG
    RG = HG * QB
    nch = (s0 + QB + KC - 1) // KC
    PK = PACKED_ROWS
    tq_row = s0 + _iota((1, QB), 1)
    w_t = wi_ref[0].T
    one, zero = jnp.ones((), MXU_DT), jnp.zeros((), MXU_DT)

    def idx_body(c, _):
        ks = pl.multiple_of(c * KC, KC)
        kic = ki_ref[0, pl.ds(ks, KC), :]
        isc = None
        for hd in range(C_IDX_HEADS):
            s = jnp.maximum(_dot_nt(kic, qi_ref[0, hd]), 0.0) * w_t[hd:hd + 1, :]
            isc = s if isc is None else isc + s
        isc = jnp.where(isc == 0.0, 0.0, isc)
        kpos = ks + _iota((KC, 1), 0)
        isc = jnp.where(kpos <= tq_row, isc, NEG)
        bits = lax.bitcast_convert_type(isc, i32)
        key = jnp.where(bits < 0, bits ^ jnp.int32(0x7FFFFFFF), bits)
        key_scr[c] = key
        k3 = key.reshape(KC // PK, PK, QB)
        hi_scr[c] = lax.shift_right_arithmetic(k3, 16).astype(i16)
        lo_scr[c] = ((k3 & 0xFFFF) - 32768).astype(i16)
        return 0

    lax.fori_loop(0, nch // 2, lambda c2, z: idx_body(2 * c2 + 1, idx_body(2 * c2, z)), 0)
    lax.cond(nch % 2 == 1, lambda z: idx_body(nch - 1, z), lambda z: z, 0)

    def rep16(v):
        return jnp.broadcast_to(v, (PK, QB)).astype(i16)[None]

    def select_threshold(nk):
        def count16(pred):
            accs = [jnp.zeros((PK, QB), MXU_DT) for _ in range(4)]
            for c in range(nk):
                x = jnp.where(pred(hi_scr[c], lo_scr[c]), one, zero)
                for r in range(KC // PK):
                    accs[r % len(accs)] = accs[r % len(accs)] + x[r]
            acc = (accs[0] + accs[1]) + (accs[2] + accs[3])
            return jnp.sum(acc.astype(f32), axis=0, keepdims=True)

        def bisect(pick, base):
            def body(b, t):
                cand = t + lax.shift_left(jnp.int32(1), 15 - b)
                c16 = rep16(cand)
                return jnp.where(base + count16(lambda h, l: pick(h, l) >= c16) >= TOPK, cand, t)
            return lax.fori_loop(0, 16, body, jnp.full((1, QB), -32768, i32))

        thi = bisect(lambda h, l: h, 0.0)
        thi16 = rep16(thi)
        for c in range(nk):
            lo_scr[c] = jnp.where(hi_scr[c] == thi16, lo_scr[c], jnp.full((), -32768, i16))
        n_hi = count16(lambda h, l: h > thi16)
        tlo = bisect(lambda h, l: l, n_hi)
        tlo16 = rep16(tlo)
        n_gt = n_hi + count16(lambda h, l: l > tlo16)
        n_eq = count16(lambda h, l: (h == thi16) & (l == tlo16))
        return thi, tlo, n_gt, n_eq

    n_variants = kv_ref.shape[1] // KC
    thi, tlo, n_gt, n_eq = lax.switch(nch - 1, [functools.partial(select_threshold, k + 1) for k in range(n_variants)])
    thr = thi * 65536 + (tlo + 32768)
    room = TOPK - n_gt

    @pl.when(jnp.max(jnp.where(n_eq > room, 1, 0)) > 0)
    def _():
        def body(c, before):
            key = key_scr[c]
            eq = key == thr
            seen = before + _dot(tri_ref[...], jnp.where(eq, 1.0, 0.0).astype(MXU_DT))
            key_scr[c] = jnp.where(eq & (seen > room), key - 1, key)
            return seen[KC - 1:KC, :]
        lax.fori_loop(0, nch, body, jnp.zeros((1, QB), f32))

    acc_scr[...] = jnp.zeros(acc_scr.shape, f32)

    def att_body(c, carry):
        ms, ls = list(carry[0]), list(carry[1])
        ks = pl.multiple_of(c * KC, KC)
        key = key_scr[c]
        kpos = ks + _iota((KC, 1), 0)
        sel_t = (key >= thr) & (kpos <= tq_row)
        keep = _dot_nt(eye_ref[...], jnp.where(sel_t, 1.0, 0.0).astype(MXU_DT))
        bias = jnp.where(keep > 0.5, 0.0, NEG)
        for u in range(KC // SUB):
            sub = pl.ds(pl.multiple_of(ks + u * SUB, SUB), SUB)
            kvc = kv_ref[0, sub, :]
            b_u = bias[:, u * SUB:(u + 1) * SUB][None]
            for g in range(NG):
                s = jnp.stack([_dot_nt(qa_ref[0, hd], kh_ref[0, hd, sub, :])
                               for hd in range(g * HG, (g + 1) * HG)]) + b_u
                m_new = jnp.maximum(ms[g], jnp.max(s, axis=-1, keepdims=True))
                p = jnp.exp2(s - m_new)
                alpha = jnp.exp2(ms[g] - m_new)
                ls[g] = alpha * ls[g] + jnp.sum(p, axis=-1, keepdims=True)
                ms[g] = m_new
                rows = slice(g * RG, (g + 1) * RG)
                pv = _dot(p.reshape(RG, SUB).astype(MXU_DT), kvc)
                acc_scr[rows] = alpha.reshape(RG, 1) * acc_scr[rows] + pv
        return tuple(ms), tuple(ls)

    m0 = tuple(jnp.full((HG, QB, 1), SOFTMAX_M0, f32) for _ in range(NG))
    l0 = tuple(jnp.zeros((HG, QB, 1), f32) for _ in range(NG))
    carry = lax.fori_loop(0, nch // 2, lambda c2, cr: att_body(2 * c2 + 1, att_body(2 * c2, cr)), (m0, l0))
    _, l_fin = lax.cond(nch % 2 == 1, lambda cr: att_body(nch - 1, cr), lambda cr: cr, carry)
    for g in range(NG):
        o_lat = (acc_scr[g * RG:(g + 1) * RG] / l_fin[g].reshape(RG, 1)).astype(MXU_DT)
        for k in range(HG):
            hd = g * HG + k
            out_ref[0, :, hd * C_DV:(hd + 1) * C_DV] = _dot(o_lat[k * QB:(k + 1) * QB], wuv_ref[hd]).astype(out_ref.dtype)


def _dsa(qa, kh, kv, qi, ki, wi, w_uv):
    B, H, T, _ = qa.shape
    QB = min(DSA_Q_BLOCK, T)
    KC = min(KEY_CHUNK, T)
    topk = min(C_TOPK, T // 4)
    wuv = w_uv.transpose(1, 0, 2).astype(MXU_DT)
    assert T // PACKED_ROWS <= 256
    eye = jnp.eye(QB, dtype=MXU_DT)
    tri = jnp.tril(jnp.ones((KC, KC), MXU_DT))
    kern = functools.partial(_dsa_kernel, QB=QB, KC=KC, SUB=min(KEY_SUB, KC), TOPK=topk, HG=DSA_HEAD_GROUP)
    half_words = pltpu.VMEM((T // KC, KC // PACKED_ROWS, PACKED_ROWS, QB), jnp.int16)
    return pl.pallas_call(
        kern,
        grid=(B, T // QB),
        in_specs=[pl.BlockSpec((1, H, QB, C_KDIM), lambda b, i: (b, 0, i, 0)),
                  pl.BlockSpec((1, H, QB, LANES), lambda b, i: (b, 0, i, 0)),
                  pl.BlockSpec((1, QB, LANES), lambda b, i: (b, i, 0)),
                  pl.BlockSpec((1, H, T, C_KDIM), lambda b, i: (b, 0, 0, 0), pipeline_mode=pl.Buffered(1)),
                  pl.BlockSpec((1, T, C_KVL), lambda b, i: (b, 0, 0)),
                  pl.BlockSpec((1, T, LANES), lambda b, i: (b, 0, 0)),
                  _const_spec(wuv.shape), _const_spec(eye.shape), _const_spec(tri.shape)],
        out_specs=pl.BlockSpec((1, QB, H * C_DV), lambda b, i: (b, i, 0)),
        out_shape=jax.ShapeDtypeStruct((B, T, H * C_DV), MXU_DT),
        scratch_shapes=[pltpu.VMEM((T // KC, KC, QB), jnp.int32),
                        half_words, half_words,
                        pltpu.VMEM((H * QB, C_KVL), jnp.float32)],
        compiler_params=_cparams("parallel", "arbitrary"),
    )(qa, qi, wi, kh, kv, ki, wuv, eye, tri)


def _post_kernel(*refs, n_mix, n_ff):
    h_ref = refs[0]
    mix = refs[1:1 + 2 * n_mix]
    g1, b1, w1_ref, w2_ref, g2, b2, wg_ref, p_ref, wp_ref, out_ref = refs[1 + 2 * n_mix:]
    h = h_ref[...]
    y = _dot(mix[0][...].astype(MXU_DT), mix[1][...])
    for k in range(1, n_mix):
        y = y + _dot(mix[2 * k][...].astype(MXU_DT), mix[2 * k + 1][...])
    h1 = _layer_norm(DN_ALPHA * h + y, g1[...], b1[...])
    h1b = h1.astype(MXU_DT)
    ff = D_FF // n_ff
    u = None
    for k in range(n_ff):
        a = jnp.square(jnp.maximum(_dot(h1b, w1_ref[:, k * ff:(k + 1) * ff]), 0.0))
        t = _dot(a.astype(MXU_DT), w2_ref[k * ff:(k + 1) * ff, :])
        u = t if u is None else u + t
    h2 = _layer_norm(DN_ALPHA * h1 + u, g2[...], b2[...])
    gate = jax.nn.sigmoid(_dot(h2.astype(MXU_DT), wg_ref[...]))
    out_ref[...] = h2 + gate * _dot(p_ref[0].astype(MXU_DT), wp_ref[...])


def _post(h2d, mixes, ln1_g, ln1_b, w1, w2, ln2_g, ln2_b, wg, p_all, layer, wp, tm):
    M, D = h2d.shape
    row = lambda w: pl.BlockSpec((tm, w), lambda i: (i, 0))
    vec = lambda v: v[None, :]
    in_specs = [row(D)]
    args = [h2d]
    for x, w in mixes:
        in_specs += [row(x.shape[1]), _const_spec(w.shape)]
        args += [x, w]
    in_specs += [_const_spec((1, D)), _const_spec((1, D)), _const_spec(w1.shape), _const_spec(w2.shape),
                 _const_spec((1, D)), _const_spec((1, D)), _const_spec(wg.shape),
                 pl.BlockSpec((1, tm, D_PLE), lambda i: (layer, i, 0)), _const_spec(wp.shape)]
    args += [vec(ln1_g), vec(ln1_b), w1, w2, vec(ln2_g), vec(ln2_b), wg, p_all, wp]
    kern = functools.partial(_post_kernel, n_mix=len(mixes), n_ff=4)
    return pl.pallas_call(
        kern,
        grid=(M // tm,),
        in_specs=in_specs,
        out_specs=row(D),
        out_shape=jax.ShapeDtypeStruct((M, D), jnp.float32),
        compiler_params=_cparams("parallel"),
    )(*args)


def kernel(x, p, positions, e_w_in, e_a_conv, e_a_i_b, e_a_f_b, e_a_norm, e_b_cmp_pos, e_b_cmp_w1, e_b_cmp_w2, e_b_g_b, e_w_out, o_w_in, o_q_norm, o_kv_norm, o_w_qb, o_w_uk, o_w_uv, o_w_iq, o_ik_g, o_ik_b, o_w_out, ln1_g, ln1_b, ln2_g, ln2_b, mlp_w1, mlp_w2, ple_gate_w, ple_w):
    B, T, D = x.shape
    M = B * T
    tm = min(ROW_TILE, T)
    h = x
    pos3 = positions[..., None]
    p_all = p.reshape(DEPTH, M, D_PLE)
    bf = lambda w: w.astype(MXU_DT)
    for i in range(DEPTH):
        j = i // 2
        if i % 2 == 0:
            qk, av, ao, small, bq, bc, bs = _even_proj(h, _even_w_in_aug(e_w_in[j]), tm)
            ya = _mlstm(qk, av, ao, small, e_a_conv[j], e_a_i_b[j], e_a_f_b[j], e_a_norm[j])
            cmp = _nsa_compress(bc, e_b_cmp_pos[j], e_b_cmp_w1[j], e_b_cmp_w2[j])
            yb = _nsa(bq, bs, cmp, small, e_b_g_b[j])
            w_out = e_w_out[j]
            mixes = [(ya.reshape(M, A_W), bf(w_out[:A_W])),
                     (yb.reshape(M, B_HEADS * LANES), bf(_pad_heads(w_out[A_W:], B_HEADS, B_DH, axis=0)))]
        else:
            qa, kh, kv, qi, ki, wi = _odd_prep(h, pos3, o_w_in[j], o_q_norm[j], o_kv_norm[j], o_w_qb[j], o_w_uk[j],
                                               o_w_iq[j], o_ik_g[j], o_ik_b[j], tm)
            o = _dsa(qa, kh, kv, qi, ki, wi, o_w_uv[j])
            mixes = [(o.reshape(M, C_HEADS * C_DV), bf(o_w_out[j]))]
        h = _post(h.reshape(M, D), mixes, ln1_g[i], ln1_b[i], bf(mlp_w1[i]), bf(mlp_w2[i]), ln2_g[i], ln2_b[i],
                  bf(ple_gate_w[i]), p_all, i, bf(ple_w[i]), min(MLP_ROW_TILE, T)).reshape(B, T, D)
    return h
```

```python
import functools

import numpy as np
import jax
import jax.numpy as jnp
from jax import lax
from jax.experimental import pallas as pl
from jax.experimental.pallas import tpu as pltpu

D_MODEL = 1024
DEPTH = 4
D_PLE = 256
D_FF = 4 * D_MODEL
DN_ALPHA = (2.0 * DEPTH) ** 0.25
LN_EPS = 1e-5
NEG = -1e30

A_HEADS = 4
A_DH = D_MODEL // 8
A_W = A_HEADS * A_DH
A_CONV = 4

B_HEADS = 8
B_DH = 64
B_KV = 2
B_HPG = B_HEADS // B_KV
B_CMP_LEN = 32
B_CMP_STRIDE = 16
B_CMP_HID = 128
B_SEL_BLK = 64
B_SEL_N = 16
B_WIN = 512

C_HEADS = 8
C_DN = 128
C_DR = 64
C_DV = 128
C_QL = 512
C_KVL = 256
C_IDX_HEADS = 8
C_IDX_DH = 64
C_IDX_DR = 32
C_TOPK = 256
ROPE_BASE = 10000.0

LANES = 128
SUBLANES = 8
PACKED_ROWS = 16
VMEM_LIMIT_BYTES = 56 * 2**20
MXU_DT = jnp.bfloat16

MLSTM_CHUNK = 256
MLSTM_SEQS = 1
ROW_TILE = 512
MLP_ROW_TILE = 512
Q_BLOCK = 256
DSA_Q_BLOCK = 256
DSA_HEAD_GROUP = 4
KEY_CHUNK = 512
KEY_SUB = 512
C_KDIM = C_DN + LANES

LOG2E = 1.4426950408889634
SOFTMAX_M0 = 0.5 * NEG
SOFTMAX_TINY = 1e-30


def _cparams(*sem):
    return pltpu.CompilerParams(dimension_semantics=sem, vmem_limit_bytes=VMEM_LIMIT_BYTES)


def _const_spec(shape):
    nd = len(shape)
    return pl.BlockSpec(shape, lambda *_: (0,) * nd, pipeline_mode=pl.Buffered(1))


def _dot(a, b):
    return jnp.dot(a, b, preferred_element_type=jnp.float32)


def _dot_nt(a, b):
    return lax.dot_general(a, b, (((1,), (1,)), ((), ())), preferred_element_type=jnp.float32)


def _dot_tn(a, b):
    return lax.dot_general(a, b, (((0,), (0,)), ((), ())), preferred_element_type=jnp.float32)


def _dot_f32(a, b):
    return jnp.dot(a, b, preferred_element_type=jnp.float32, precision=lax.Precision.HIGHEST)


def _layer_norm(x, g, b):
    mu = jnp.mean(x, axis=-1, keepdims=True)
    var = jnp.mean(jnp.square(x - mu), axis=-1, keepdims=True)
    return (x - mu) * lax.rsqrt(var + LN_EPS) * g + b


def _masked_softmax2(s, bias):
    s = s + bias
    m = jnp.maximum(jnp.max(s, axis=-1, keepdims=True), SOFTMAX_M0)
    e = jnp.exp2(s - m)
    return e / jnp.maximum(jnp.sum(e, axis=-1, keepdims=True), SOFTMAX_TINY)


def _iota(shape, dim):
    return lax.broadcasted_iota(jnp.int32, shape, dim)


def _pad_heads(w, n_heads, dh, axis=-1):
    axis = axis % w.ndim
    shp = w.shape[:axis] + (n_heads, dh) + w.shape[axis + 1:]
    w = w.reshape(shp)
    pad = [(0, 0)] * w.ndim
    pad[axis + 1] = (0, LANES - dh)
    w = jnp.pad(w, pad)
    return w.reshape(shp[:axis] + (n_heads * LANES,) + shp[axis + 2:])


def _pad_cols(w, width):
    return jnp.pad(w, [(0, 0)] * (w.ndim - 1) + [(0, width - w.shape[-1])])


def _even_proj_kernel(h_ref, w_ref, blk_ref, qk_ref, av_ref, ao_ref, sm_ref, bq_ref, bc_ref, bs_ref):
    z = _dot(h_ref[0].astype(MXU_DT), w_ref[...])
    o = 0
    qk_ref[0] = z[:, o:o + 2 * A_W]; o += 2 * A_W
    av_ref[0] = z[:, o:o + A_W].astype(av_ref.dtype); o += A_W
    ao_ref[0] = z[:, o:o + A_W]; o += A_W
    sm_ref[0] = z[:, o:o + LANES]; o += LANES
    for hd in range(B_HEADS):
        bq_ref[0, hd] = z[:, o:o + LANES].astype(bq_ref.dtype); o += LANES
    for j in range(2 * B_KV):
        bc_ref[0, j] = z[:, o:o + LANES]; o += LANES
    for j in range(4 * B_KV):
        zj = z[:, o:o + LANES]; o += LANES
        if j < B_KV:
            zj = zj + blk_ref[...]
        bs_ref[0, j] = zj.astype(bs_ref.dtype)


def _even_proj(h3, w_aug, tm):
    B, T, D = h3.shape
    n = w_aug.shape[1]
    f32 = jnp.float32
    out_shape = (
        jax.ShapeDtypeStruct((B, T, 2 * A_W), f32),
        jax.ShapeDtypeStruct((B, T, A_W), MXU_DT),
        jax.ShapeDtypeStruct((B, T, A_W), f32),
        jax.ShapeDtypeStruct((B, T, LANES), f32),
        jax.ShapeDtypeStruct((B, B_HEADS, T, LANES), MXU_DT),
        jax.ShapeDtypeStruct((B, 2 * B_KV, T, LANES), f32),
        jax.ShapeDtypeStruct((B, 4 * B_KV, T, LANES), MXU_DT),
    )
    row = lambda w: pl.BlockSpec((1, tm, w), lambda b, i: (b, i, 0))
    hm = lambda nh: pl.BlockSpec((1, nh, tm, LANES), lambda b, i: (b, 0, i, 0))
    assert T // B_SEL_BLK <= LANES - B_DH
    blk = (np.arange(LANES)[None, :] == B_DH + np.arange(T)[:, None] // B_SEL_BLK).astype(np.float32)
    return pl.pallas_call(
        _even_proj_kernel,
        grid=(B, T // tm),
        in_specs=[row(D), _const_spec((D, n)), pl.BlockSpec((tm, LANES), lambda b, i: (i, 0))],
        out_specs=(row(2 * A_W), row(A_W), row(A_W), row(LANES), hm(B_HEADS), hm(2 * B_KV), hm(4 * B_KV)),
        out_shape=out_shape,
        compiler_params=_cparams("parallel", "parallel"),
    )(h3, w_aug, jnp.asarray(blk))


def _even_w_in_aug(w_in):
    sizes = (A_W, A_W, A_W, A_W, A_HEADS, A_HEADS, B_HEADS * B_DH) + (B_KV * B_DH,) * 6 + (3 * B_HEADS,)
    parts, o = [], 0
    for s in sizes:
        parts.append(w_in[:, o:o + s]); o += s
    aq, ak, av, ao, ai, af, bq, bkc, bvc, bks, bvs, bkw, bvw, bg = parts
    small = _pad_cols(jnp.concatenate([ai, af, bg], -1), LANES)
    ph = lambda w: _pad_heads(w, B_KV, B_DH)
    bq = bq * (B_DH ** -0.5 * LOG2E)
    cols = [aq, ak, av, ao, small, _pad_heads(bq, B_HEADS, B_DH),
            ph(bkc), ph(bvc), ph(bks), ph(bvs), ph(bkw), ph(bvw)]
    return jnp.concatenate(cols, -1).astype(MXU_DT)


def _mlstm_kernel(qk_ref, v_ref, o_ref, gc_ref, gr_ref, cw_ref, bc_ref, br_ref, ng_ref, tri_ref, out_ref,
                  xs_scr, ct_scr, n_scr, m_scr, *, L, NB):
    c = pl.program_id(1)
    f32 = jnp.float32
    W2 = 2 * A_W

    @pl.when(c == 0)
    def _():
        xs_scr[:, 0:SUBLANES, :] = jnp.zeros((NB, SUBLANES, W2), f32)
        ct_scr[...] = jnp.zeros(ct_scr.shape, f32)
        n_scr[...] = jnp.zeros(n_scr.shape, f32)
        m_scr[...] = jnp.zeros(m_scr.shape, f32)

    tri = tri_ref[...]
    causal = _iota((L, L), 1) <= _iota((L, L), 0)
    cw = cw_ref[...]
    for bb in range(NB):
        _mlstm_chunk(bb, qk_ref, v_ref, o_ref, gc_ref, gr_ref, cw, bc_ref, br_ref, ng_ref, tri, causal, out_ref,
                     xs_scr, ct_scr, n_scr, m_scr, L)


def _mlstm_chunk(bb, qk_ref, v_ref, o_ref, gc_ref, gr_ref, cw, bc_ref, br_ref, ng_ref, tri, causal, out_ref,
                 xs_scr, ct_scr, n_scr, m_scr, L):
    f32 = jnp.float32
    xs_scr[bb, SUBLANES:SUBLANES + L, :] = qk_ref[bb]
    base = SUBLANES - (A_CONV - 1)
    acc = xs_scr[bb, base:base + L, :] * cw[0:1, :]
    for j in range(1, A_CONV):
        acc = acc + xs_scr[bb, base + j:base + j + L, :] * cw[j:j + 1, :]
    qk = acc * jax.nn.sigmoid(acc)
    xs_scr[bb, 0:SUBLANES, :] = xs_scr[bb, L:L + SUBLANES, :]

    gc = gc_ref[bb] + bc_ref[...]
    gr = gr_ref[bb, 0] + br_ref[...]
    ig_c = gc[:, 0:A_HEADS]
    b_c = _dot_f32(tri, jax.nn.log_sigmoid(gc[:, A_HEADS:2 * A_HEADS]))
    ig_r = gr[0:A_HEADS, :]
    b_r = _dot_f32(jax.nn.log_sigmoid(gr[A_HEADS:2 * A_HEADS, :]), tri.T)

    for hd in range(A_HEADS):
        sl = slice(hd * A_DH, (hd + 1) * A_DH)
        q_h = (qk[:, sl] * A_DH ** -0.5).astype(MXU_DT)
        k_f = qk[:, A_W + hd * A_DH:A_W + (hd + 1) * A_DH]
        k_h = k_f.astype(MXU_DT)
        v_h = v_ref[bb][:, sl]
        bi = b_c[:, hd:hd + 1]
        ic = ig_c[:, hd:hd + 1]
        dmat = jnp.where(causal, bi - b_r[hd:hd + 1, :] + ig_r[hd:hd + 1, :], NEG)
        m_prev = m_scr[bb, hd][:, 0:1]
        m_inter = bi + m_prev
        m_t = jnp.maximum(m_inter, jnp.max(dmat, axis=1, keepdims=True))
        e_inter = jnp.exp(m_inter - m_t)
        s = _dot_nt(q_h, k_h) * jnp.exp(dmat - m_t)
        ct = ct_scr[bb, hd]
        nrow = n_scr[bb, hd]
        num = e_inter * _dot(q_h, ct.astype(MXU_DT)) + _dot(s.astype(MXU_DT), v_h)
        den = e_inter * jnp.sum(q_h.astype(f32) * nrow, axis=1, keepdims=True) + jnp.sum(s, axis=1, keepdims=True)
        hc = num / jnp.maximum(jnp.abs(den), jnp.exp(-m_t))
        b_last = bi[L - 1:L, :]
        dec = b_last - bi + ic
        m_new = jnp.maximum(b_last + m_prev, jnp.max(dec, axis=0, keepdims=True))
        wgt = jnp.exp(dec - m_new)
        e_st = jnp.exp(b_last + m_prev - m_new)
        kw = k_f * wgt
        ct_scr[bb, hd] = e_st * ct + _dot_tn(kw.astype(MXU_DT), v_h)
        n_scr[bb, hd] = e_st * nrow + jnp.sum(kw, axis=0, keepdims=True)
        m_scr[bb, hd] = jnp.broadcast_to(m_new, (1, LANES))
        mu = jnp.mean(hc, axis=1, keepdims=True)
        var = jnp.mean(jnp.square(hc - mu), axis=1, keepdims=True)
        hn = (hc - mu) * lax.rsqrt(var + LN_EPS) * ng_ref[:, sl]
        out_ref[bb, :, sl] = (hn * jax.nn.sigmoid(o_ref[bb][:, sl])).astype(out_ref.dtype)


def _mlstm(qk, av, ao, small, conv_w, i_b, f_b, norm_g):
    B, T, _ = qk.shape
    L = min(MLSTM_CHUNK, T)
    N = T // L
    f32 = jnp.float32
    gates = small[..., 0:2 * A_HEADS]
    gates_r = gates.reshape(B, N, L, 2 * A_HEADS).transpose(0, 1, 3, 2)
    bias = jnp.concatenate([i_b, f_b]).astype(f32)
    tri = jnp.tril(jnp.ones((L, L), f32))
    NB = MLSTM_SEQS if B % MLSTM_SEQS == 0 else 1
    row = lambda w: pl.BlockSpec((NB, L, w), lambda b, c: (b, c, 0))
    kern = functools.partial(_mlstm_kernel, L=L, NB=NB)
    return pl.pallas_call(
        kern,
        grid=(B // NB, N),
        in_specs=[row(2 * A_W), row(A_W), row(A_W), row(2 * A_HEADS),
                  pl.BlockSpec((NB, 1, 2 * A_HEADS, L), lambda b, c: (b, c, 0, 0)),
                  _const_spec((A_CONV, 2 * A_W)), _const_spec((1, 2 * A_HEADS)), _const_spec((2 * A_HEADS, 1)),
                  _const_spec((1, A_W)), _const_spec((L, L))],
        out_specs=row(A_W),
        out_shape=jax.ShapeDtypeStruct((B, T, A_W), MXU_DT),
        scratch_shapes=[pltpu.VMEM((NB, L + 2 * SUBLANES, 2 * A_W), f32),
                        pltpu.VMEM((NB, A_HEADS, A_DH, A_DH), f32),
                        pltpu.VMEM((NB, A_HEADS, 1, A_DH), f32),
                        pltpu.VMEM((NB, A_HEADS, 1, LANES), f32)],
        compiler_params=_cparams("parallel", "arbitrary"),
    )(qk, av, ao, gates, gates_r, conv_w, bias[None, :], bias[:, None], norm_g[None, :], tri)


def _nsa_cmp_kernel(x_ref, w1a_ref, w1b_ref, w2_ref, pos_ref, out_ref):
    n = x_ref.shape[2] // B_CMP_STRIDE
    half = B_CMP_STRIDE * LANES
    for j in range(2):
        bias = (_dot(pos_ref[j, :, 0:half].astype(MXU_DT), w1a_ref[j])
                + _dot(pos_ref[j, :, half:2 * half].astype(MXU_DT), w1b_ref[j]))[0:1, :]
        for g in range(B_KV):
            u = jnp.concatenate([x_ref[0, j * B_KV + g, pl.ds(r, n, stride=B_CMP_STRIDE), :]
                                 for r in range(B_CMP_STRIDE)], axis=1).astype(MXU_DT)
            a = _dot(u, w1a_ref[j])
            bm = _dot(u, w1b_ref[j])
            pre = a + pltpu.roll(bm, n - 1, 0) + bias
            hid = jax.nn.gelu(pre)
            out_ref[0, j * B_KV + g] = _dot(hid.astype(MXU_DT), w2_ref[j])


def _nsa_compress(bc, cmp_pos, cmp_w1, cmp_w2):
    B, _, T, _ = bc.shape
    nblk = T // B_CMP_STRIDE
    half = B_CMP_STRIDE * LANES
    w1 = jnp.pad(cmp_w1.reshape(2, B_CMP_LEN, B_DH, B_CMP_HID), ((0, 0), (0, 0), (0, LANES - B_DH), (0, 0)))
    w1 = w1.reshape(2, B_CMP_LEN * LANES, B_CMP_HID).astype(MXU_DT)
    w1a, w1b = w1[:, :half], w1[:, half:]
    w2 = _pad_cols(cmp_w2, LANES).astype(MXU_DT)
    pos = jnp.pad(cmp_pos, ((0, 0), (0, 0), (0, LANES - B_DH))).reshape(2, 1, B_CMP_LEN * LANES)
    pos = jnp.broadcast_to(pos, (2, SUBLANES, B_CMP_LEN * LANES))
    return pl.pallas_call(
        _nsa_cmp_kernel,
        grid=(B,),
        in_specs=[pl.BlockSpec((1, 2 * B_KV, T, LANES), lambda b: (b, 0, 0, 0)),
                  _const_spec(w1a.shape), _const_spec(w1b.shape), _const_spec(w2.shape), _const_spec(pos.shape)],
        out_specs=pl.BlockSpec((1, 2 * B_KV, nblk, LANES), lambda b: (b, 0, 0, 0)),
        out_shape=jax.ShapeDtypeStruct((B, 2 * B_KV, nblk, LANES), jnp.float32),
        compiler_params=_cparams("parallel"),
    )(bc, w1a, w1b, w2, pos)


def _nsa_kernel(q_ref, kv_ref, cmp_ref, sm_ref, gb_ref, ovt_ref, out_ref, acc_scr, ocmp_scr, qaug_scr,
                *, QB, KC, WL, NSB, NSEL):
    f32 = jnp.float32
    i = pl.program_id(1)
    s0 = i * QB
    R = B_HPG * QB
    tq_col = s0 + _iota((QB, 1), 0)
    tq_row = s0 + _iota((1, QB), 1)
    nch = (s0 + QB + KC - 1) // KC
    gates = jax.nn.sigmoid(sm_ref[0] + gb_ref[...])
    ncmp = cmp_ref.shape[2]
    cmp_end = _iota((1, ncmp), 1) * B_CMP_STRIDE + (B_CMP_LEN - 1)
    cmp_bias = jnp.where(cmp_end <= tq_col, 0.0, NEG)[None]
    jb = _iota((NSB, 1), 0)
    cur = lax.shift_right_logical(tq_row, int(np.log2(B_SEL_BLK)))
    forced = (jb == 0) | (jb == cur) | (jb == cur - 1)
    valid = jb * B_SEL_BLK <= tq_row
    wstart = pl.multiple_of(jnp.maximum(s0 + QB - WL, 0), QB)
    wpos = wstart + _iota((1, WL), 1)
    win_bias = jnp.where((wpos <= tq_col) & (wpos > tq_col - B_WIN), 0.0, NEG)[None]

    for g in range(B_KV):
        qs = q_ref[0, g * B_HPG:(g + 1) * B_HPG].reshape(R, LANES)
        kcm = cmp_ref[0, g].astype(MXU_DT)
        vcm = cmp_ref[0, B_KV + g].astype(MXU_DT)
        p_cmp = _masked_softmax2(_dot_nt(qs, kcm).reshape(B_HPG, QB, ncmp), cmp_bias)
        o_cmp = _dot(p_cmp.reshape(R, ncmp).astype(MXU_DT), vcm)
        kw_ = kv_ref[0, 2 * B_KV + g, pl.ds(wstart, WL), :]
        vw_ = kv_ref[0, 3 * B_KV + g, pl.ds(wstart, WL), :]
        p_win = _masked_softmax2(_dot_nt(qs, kw_).reshape(B_HPG, QB, WL), win_bias)
        o_win = _dot(p_win.reshape(R, WL).astype(MXU_DT), vw_)
        for hd in range(B_HPG):
            c0 = 2 * A_HEADS + (g * B_HPG + hd) * 3
            rs = slice(hd * QB, (hd + 1) * QB)
            ocmp_scr[g * R + hd * QB:g * R + (hd + 1) * QB] = (gates[:, c0:c0 + 1] * o_cmp[rs]
                                                               + gates[:, c0 + 2:c0 + 3] * o_win[rs])
        psum = jnp.sum(p_cmp, axis=0)
        imp_t = lax.dot_general(ovt_ref[...], psum, (((1,), (1,)), ((), ())),
                                preferred_element_type=f32, precision=lax.Precision.HIGHEST)
        sc = jnp.where(forced, 1e6, imp_t)
        sc = jnp.where(valid, sc, NEG)
        rank = jnp.zeros((NSB, QB), f32)
        for j in range(NSB):
            cj = jnp.broadcast_to(sc[j:j + 1, :], (NSB, QB))
            ahead = (cj > sc) | ((cj == sc) & (jb > j))
            rank = rank + jnp.where(ahead, 1.0, 0.0)
        drop = jnp.where(rank < NSEL, 0.0, NEG)
        pad = [jnp.zeros((B_DH, QB), f32), drop]
        if B_DH + NSB < LANES:
            pad.append(jnp.zeros((LANES - B_DH - NSB, QB), f32))
        drop_t = jnp.concatenate(pad, axis=0).T.astype(MXU_DT)
        for hd in range(B_HPG):
            rows = slice((g * B_HPG + hd) * QB, (g * B_HPG + hd + 1) * QB)
            qaug_scr[rows] = q_ref[0, g * B_HPG + hd] + drop_t

    acc_scr[...] = jnp.zeros(acc_scr.shape, f32)

    def sel_body(c, carry, diagonal=False):
        ms, ls = list(carry[0]), list(carry[1])
        ks = pl.multiple_of(c * KC, KC)
        for g in range(B_KV):
            kc_ = kv_ref[0, 0 * B_KV + g, pl.ds(ks, KC), :]
            vc_ = kv_ref[0, 1 * B_KV + g, pl.ds(ks, KC), :]
            s3 = _dot_nt(qaug_scr[g * R:(g + 1) * R], kc_).reshape(B_HPG, QB, KC)
            if diagonal:
                s3 = s3 + jnp.where(ks + _iota((1, KC), 1) <= tq_col, 0.0, NEG)[None]
            m_new = jnp.maximum(ms[g], jnp.max(s3, axis=-1, keepdims=True))
            p = jnp.exp2(s3 - m_new)
            alpha = jnp.exp2(ms[g] - m_new)
            ls[g] = alpha * ls[g] + jnp.sum(p, axis=-1, keepdims=True)
            ms[g] = m_new
            rows = slice(g * R, (g + 1) * R)
            acc_scr[rows] = alpha.reshape(R, 1) * acc_scr[rows] + _dot(p.reshape(R, KC).astype(MXU_DT), vc_)
        return tuple(ms), tuple(ls)

    m0 = tuple(jnp.full((B_HPG, QB, 1), SOFTMAX_M0, f32) for _ in range(B_KV))
    l0 = tuple(jnp.zeros((B_HPG, QB, 1), f32) for _ in range(B_KV))
    nfull = nch - 1
    carry = lax.fori_loop(0, nfull // 2, lambda c2, cr: sel_body(2 * c2 + 1, sel_body(2 * c2, cr)), (m0, l0))
    carry = lax.cond(nfull % 2 == 1, lambda cr: sel_body(nfull - 1, cr), lambda cr: cr, carry)
    _, l_fin = sel_body(nfull, carry, diagonal=True)

    for g in range(B_KV):
        rows = slice(g * R, (g + 1) * R)
        o_sel = acc_scr[rows] / l_fin[g].reshape(R, 1)
        o_two = ocmp_scr[rows]
        for hd in range(B_HPG):
            c0 = 2 * A_HEADS + (g * B_HPG + hd) * 3
            rs = slice(hd * QB, (hd + 1) * QB)
            o = o_two[rs] + gates[:, c0 + 1:c0 + 2] * o_sel[rs]
            col = (g * B_HPG + hd) * LANES
            out_ref[0, :, col:col + LANES] = o.astype(out_ref.dtype)


def _nsa(bq, bs, cmp, small, g_b):
    B, _, T, _ = bq.shape
    QB = min(Q_BLOCK, T)
    KC = min(KEY_CHUNK, T)
    assert KC % QB == 0
    WL = min(B_WIN + QB, T)
    NSB = T // B_SEL_BLK
    NSEL = min(B_SEL_N, NSB)
    ncmp = cmp.shape[2]
    f32 = jnp.float32
    M = (T - B_CMP_LEN) // B_CMP_STRIDE + 1
    assert NSB % SUBLANES == 0 and NSB <= LANES
    mi, jj = np.arange(ncmp)[None, :], np.arange(NSB)[:, None]
    ovt = ((mi * B_CMP_STRIDE < (jj + 1) * B_SEL_BLK) & (mi * B_CMP_STRIDE + B_CMP_LEN > jj * B_SEL_BLK)
           & (mi < M)).astype(np.float32)
    gb = jnp.zeros((1, LANES), f32).at[0, 2 * A_HEADS:2 * A_HEADS + 3 * B_HEADS].set(g_b)
    kern = functools.partial(_nsa_kernel, QB=QB, KC=KC, WL=WL, NSB=NSB, NSEL=NSEL)
    return pl.pallas_call(
        kern,
        grid=(B, T // QB),
        in_specs=[pl.BlockSpec((1, B_HEADS, QB, LANES), lambda b, i: (b, 0, i, 0)),
                  pl.BlockSpec((1, 4 * B_KV, T, LANES), lambda b, i: (b, 0, 0, 0)),
                  pl.BlockSpec((1, 2 * B_KV, ncmp, LANES), lambda b, i: (b, 0, 0, 0)),
                  pl.BlockSpec((1, QB, LANES), lambda b, i: (b, i, 0)),
                  _const_spec((1, LANES)), _const_spec(ovt.shape)],
        out_specs=pl.BlockSpec((1, QB, B_HEADS * LANES), lambda b, i: (b, i, 0)),
        out_shape=jax.ShapeDtypeStruct((B, T, B_HEADS * LANES), MXU_DT),
        scratch_shapes=[pltpu.VMEM((B_HEADS * QB, LANES), f32), pltpu.VMEM((B_HEADS * QB, LANES), f32),
                        pltpu.VMEM((B_HEADS * QB, LANES), MXU_DT)],
        compiler_params=_cparams("parallel", "arbitrary"),
    )(bq, bs, cmp, small, gb, jnp.asarray(ovt))


def _rope_rows(d):
    inv = ROPE_BASE ** (-jnp.arange(0, d, 2, dtype=jnp.float32) / d)
    z = jnp.zeros((C_DR - d,), jnp.float32)
    inv64 = jnp.concatenate([inv, inv, z])
    sgn64 = jnp.concatenate([-jnp.ones(d // 2), jnp.ones(d // 2), z]).astype(jnp.float32)
    return inv64, sgn64


def _odd_prep_kernel(h_ref, pos_ref, win_ref, wqb_ref, wiq_ref, wuk_ref, qn_ref, kvn_ref, ikg_ref, ikb_ref,
                     rope_ref, perm_ref, qa_ref, kh_ref, kv_ref, qi_ref, ki_ref, wi_ref):
    f32 = jnp.float32
    z = _dot(h_ref[0].astype(MXU_DT), win_ref[...])
    pos = pos_ref[0].astype(f32)
    rr = rope_ref[...]
    ang_q = pos * rr[0:1, :]
    cos_q, sin_q = jnp.cos(ang_q), jnp.sin(ang_q) * rr[1:2, :]
    ang_i = pos * rr[2:3, :]
    cos_i, sin_i = jnp.cos(ang_i), jnp.sin(ang_i) * rr[3:4, :]

    def rms(x, g):
        return x * lax.rsqrt(jnp.mean(jnp.square(x), axis=-1, keepdims=True) + LN_EPS) * g

    cq = rms(z[:, 0:C_QL], qn_ref[...])
    ckv = rms(z[:, C_QL:C_QL + C_KVL], kvn_ref[...])
    o = C_QL + C_KVL
    k_rope = (z[:, o:o + LANES] * cos_q + z[:, o + LANES:o + 2 * LANES] * sin_q).astype(kh_ref.dtype)
    ckvb = ckv.astype(kv_ref.dtype)
    kv_ref[0] = ckvb
    for hd in range(C_HEADS):
        kh_ref[0, hd, :, 0:C_DN] = _dot(ckvb, wuk_ref[hd]).astype(kh_ref.dtype)
        kh_ref[0, hd, :, C_DN:C_KDIM] = k_rope
    ik = z[:, o + 2 * LANES:o + 3 * LANES]
    real = _iota((1, LANES), 1) < C_IDX_DH
    mu = jnp.sum(ik, axis=-1, keepdims=True) / C_IDX_DH
    dlt = jnp.where(real, ik - mu, 0.0)
    var = jnp.sum(jnp.square(dlt), axis=-1, keepdims=True) / C_IDX_DH
    ki = dlt * lax.rsqrt(var + LN_EPS) * ikg_ref[...] + ikb_ref[...]
    ki = ki * cos_i + _dot_f32(ki, perm_ref[...]) * sin_i
    ki_ref[0] = ki.astype(ki_ref.dtype)
    wi_ref[0] = z[:, o + 3 * LANES:o + 4 * LANES] * (C_IDX_HEADS ** -0.5 * C_IDX_DH ** -0.5)
    cqb = cq.astype(MXU_DT)
    qf = _dot(cqb, wqb_ref[...])
    qi = _dot(cqb, wiq_ref[...])
    scale = (C_DN + C_DR) ** -0.5 * LOG2E
    nh = C_HEADS * LANES
    for hd in range(C_HEADS):
        cs = slice(hd * LANES, (hd + 1) * LANES)
        q_rope = qf[:, nh:2 * nh][:, cs] * cos_q + qf[:, 2 * nh:3 * nh][:, cs] * sin_q
        qa_ref[0, hd, :, 0:C_DN] = (qf[:, hd * C_DN:(hd + 1) * C_DN] * scale).astype(qa_ref.dtype)
        qa_ref[0, hd, :, C_DN:C_KDIM] = (q_rope * scale).astype(qa_ref.dtype)
        qi_h = qi[:, 0:nh][:, cs] * cos_i + qi[:, nh:2 * nh][:, cs] * sin_i
        qi_ref[0, hd] = qi_h.astype(qi_ref.dtype)


def _rot_cols(w, n_heads, dh, d):
    w = w.reshape(w.shape[0], n_heads, dh)
    h = d // 2
    return jnp.concatenate([w[..., h:d], w[..., 0:h], jnp.zeros_like(w[..., d:])], -1).reshape(w.shape[0], n_heads * dh)


def _odd_prep(h3, pos3, w_in, q_norm, kv_norm, w_qb, w_uk, w_iq, ik_g, ik_b, tm):
    B, T, D = h3.shape
    f32 = jnp.float32
    o = 0
    parts = []
    for s in (C_QL, C_KVL, C_DR, C_IDX_DH, C_IDX_HEADS):
        parts.append(w_in[:, o:o + s]); o += s
    w_cq, w_ckv, w_kr, w_ik, w_iw = parts
    pc = lambda w: _pad_cols(w, LANES)
    win = jnp.concatenate([w_cq, w_ckv, pc(w_kr), pc(_rot_cols(w_kr, 1, C_DR, C_DR)), pc(w_ik), pc(w_iw)],
                          -1).astype(MXU_DT)
    wq = w_qb.reshape(C_QL, C_HEADS, C_DN + C_DR)
    w_nope = wq[..., :C_DN].reshape(C_QL, C_HEADS * C_DN)
    w_rope = wq[..., C_DN:].reshape(C_QL, C_HEADS * C_DR)
    wqb = jnp.concatenate([w_nope, _pad_heads(w_rope, C_HEADS, C_DR),
                           _pad_heads(_rot_cols(w_rope, C_HEADS, C_DR, C_DR), C_HEADS, C_DR)], -1).astype(MXU_DT)
    wiq = jnp.concatenate([_pad_heads(w_iq, C_IDX_HEADS, C_IDX_DH),
                           _pad_heads(_rot_cols(w_iq, C_IDX_HEADS, C_IDX_DH, C_IDX_DR), C_IDX_HEADS, C_IDX_DH)],
                          -1).astype(MXU_DT)
    wuk = w_uk.transpose(1, 0, 2).astype(MXU_DT)
    inv_q, sgn_q = _rope_rows(C_DR)
    inv_i, sgn_i = _rope_rows(C_IDX_DR)
    rope = jnp.stack([jnp.tile(v, LANES // C_DR) for v in (inv_q, sgn_q, inv_i, sgn_i)])
    rope = jnp.concatenate([rope, jnp.zeros((SUBLANES - 4, LANES), f32)])
    hh = C_IDX_DR // 2
    src = np.arange(LANES)
    src[:hh] += hh
    src[hh:C_IDX_DR] -= hh
    perm = np.zeros((LANES, LANES), np.float32)
    perm[src, np.arange(LANES)] = 1.0
    ikg = _pad_cols(ik_g[None, :], LANES)
    ikb = _pad_cols(ik_b[None, :], LANES)
    out_shape = (
        jax.ShapeDtypeStruct((B, C_HEADS, T, C_KDIM), MXU_DT),
        jax.ShapeDtypeStruct((B, C_HEADS, T, C_KDIM), MXU_DT),
        jax.ShapeDtypeStruct((B, T, C_KVL), MXU_DT),
        jax.ShapeDtypeStruct((B, C_IDX_HEADS, T, LANES), MXU_DT),
        jax.ShapeDtypeStruct((B, T, LANES), MXU_DT),
        jax.ShapeDtypeStruct((B, T, LANES), f32),
    )
    row = lambda w: pl.BlockSpec((1, tm, w), lambda b, i: (b, i, 0))
    hm = lambda w: pl.BlockSpec((1, C_HEADS, tm, w), lambda b, i: (b, 0, i, 0))
    return pl.pallas_call(
        _odd_prep_kernel,
        grid=(B, T // tm),
        in_specs=[row(D), row(1), _const_spec(win.shape), _const_spec(wqb.shape), _const_spec(wiq.shape),
                  _const_spec(wuk.shape), _const_spec((1, C_QL)), _const_spec((1, C_KVL)),
                  _const_spec((1, LANES)), _const_spec((1, LANES)), _const_spec(rope.shape),
                  _const_spec(perm.shape)],
        out_specs=(hm(C_KDIM), hm(C_KDIM), row(C_KVL), hm(LANES), row(LANES), row(LANES)),
        out_shape=out_shape,
        compiler_params=_cparams("parallel", "parallel"),
    )(h3, pos3, win, wqb, wiq, wuk, q_norm[None, :], kv_norm[None, :], ikg, ikb, rope, jnp.asarray(perm))


def _dsa_kernel(qa_ref, qi_ref, wi_ref, kh_ref, kv_ref, ki_ref, wuv_ref, eye_ref, tri_ref, out_ref,
                key_scr, hi_scr, lo_scr, acc_scr, *, QB, KC, SUB, TOPK, HG):
    f32, i32, i16 = jnp.float32, jnp.int32, jnp.int16
    i = pl.program_id(1)
    s0 = i * QB
    H = C_HEADS
    NG = H // HG
    RG = HG * QB
    nch = (s0 + QB + KC - 1) // KC
    PK = PACKED_ROWS
    tq_row = s0 + _iota((1, QB), 1)
    w_t = wi_ref[0].T
    one, zero = jnp.ones((), MXU_DT), jnp.zeros((), MXU_DT)

    def idx_body(c, _):
        ks = pl.multiple_of(c * KC, KC)
        kic = ki_ref[0, pl.ds(ks, KC), :]
        isc = None
        for hd in range(C_IDX_HEADS):
            s = jnp.maximum(_dot_nt(kic, qi_ref[0, hd]), 0.0) * w_t[hd:hd + 1, :]
            isc = s if isc is None else isc + s
        isc = jnp.where(isc == 0.0, 0.0, isc)
        kpos = ks + _iota((KC, 1), 0)
        isc = jnp.where(kpos <= tq_row, isc, NEG)
        bits = lax.bitcast_convert_type(isc, i32)
        key = jnp.where(bits < 0, bits ^ jnp.int32(0x7FFFFFFF), bits)
        key_scr[c] = key
        k3 = key.reshape(KC // PK, PK, QB)
        hi_scr[c] = lax.shift_right_arithmetic(k3, 16).astype(i16)
        lo_scr[c] = ((k3 & 0xFFFF) - 32768).astype(i16)
        return 0

    lax.fori_loop(0, nch // 2, lambda c2, z: idx_body(2 * c2 + 1, idx_body(2 * c2, z)), 0)
    lax.cond(nch % 2 == 1, lambda z: idx_body(nch - 1, z), lambda z: z, 0)

    def rep16(v):
        return jnp.broadcast_to(v, (PK, QB)).astype(i16)[None]

    def select_threshold(nk):
        def count16(pred):
            accs = [jnp.zeros((PK, QB), MXU_DT) for _ in range(4)]
            for c in range(nk):
                x = jnp.where(pred(hi_scr[c], lo_scr[c]), one, zero)
                for r in range(KC // PK):
                    accs[r % len(accs)] = accs[r % len(accs)] + x[r]
            acc = (accs[0] + accs[1]) + (accs[2] + accs[3])
            return jnp.sum(acc.astype(f32), axis=0, keepdims=True)

        def bisect(pick, base):
            def body(b, t):
                cand = t + lax.shift_left(jnp.int32(1), 15 - b)
                c16 = rep16(cand)
                return jnp.where(base + count16(lambda h, l: pick(h, l) >= c16) >= TOPK, cand, t)
            return lax.fori_loop(0, 16, body, jnp.full((1, QB), -32768, i32))

        thi = bisect(lambda h, l: h, 0.0)
        thi16 = rep16(thi)
        for c in range(nk):
            lo_scr[c] = jnp.where(hi_scr[c] == thi16, lo_scr[c], jnp.full((), -32768, i16))
        n_hi = count16(lambda h, l: h > thi16)
        tlo = bisect(lambda h, l: l, n_hi)
        tlo16 = rep16(tlo)
        n_gt = n_hi + count16(lambda h, l: l > tlo16)
        n_eq = count16(lambda h, l: (h == thi16) & (l == tlo16))
        return thi, tlo, n_gt, n_eq

    n_variants = kv_ref.shape[1] // KC
    thi, tlo, n_gt, n_eq = lax.switch(nch - 1, [functools.partial(select_threshold, k + 1) for k in range(n_variants)])
    thr = thi * 65536 + (tlo + 32768)
    room = TOPK - n_gt

    @pl.when(jnp.max(jnp.where(n_eq > room, 1, 0)) > 0)
    def _():
        def body(c, before):
            key = key_scr[c]
            eq = key == thr
            seen = before + _dot(tri_ref[...], jnp.where(eq, 1.0, 0.0).astype(MXU_DT))
            key_scr[c] = jnp.where(eq & (seen > room), key - 1, key)
            return seen[KC - 1:KC, :]
        lax.fori_loop(0, nch, body, jnp.zeros((1, QB), f32))

    acc_scr[...] = jnp.zeros(acc_scr.shape, f32)

    def att_body(c, carry):
        ms, ls = list(carry[0]), list(carry[1])
        ks = pl.multiple_of(c * KC, KC)
        key = key_scr[c]
        kpos = ks + _iota((KC, 1), 0)
        sel_t = (key >= thr) & (kpos <= tq_row)
        keep = _dot_nt(eye_ref[...], jnp.where(sel_t, 1.0, 0.0).astype(MXU_DT))
        bias = jnp.where(keep > 0.5, 0.0, NEG)
        for u in range(KC // SUB):
            sub = pl.ds(pl.multiple_of(ks + u * SUB, SUB), SUB)
            kvc = kv_ref[0, sub, :]
            b_u = bias[:, u * SUB:(u + 1) * SUB][None]
            for g in range(NG):
                s = jnp.stack([_dot_nt(qa_ref[0, hd], kh_ref[0, hd, sub, :])
                               for hd in range(g * HG, (g + 1) * HG)]) + b_u
                m_new = jnp.maximum(ms[g], jnp.max(s, axis=-1, keepdims=True))
                p = jnp.exp2(s - m_new)
                alpha = jnp.exp2(ms[g] - m_new)
                ls[g] = alpha * ls[g] + jnp.sum(p, axis=-1, keepdims=True)
                ms[g] = m_new
                rows = slice(g * RG, (g + 1) * RG)
                pv = _dot(p.reshape(RG, SUB).astype(MXU_DT), kvc)
                acc_scr[rows] = alpha.reshape(RG, 1) * acc_scr[rows] + pv
        return tuple(ms), tuple(ls)

    m0 = tuple(jnp.full((HG, QB, 1), SOFTMAX_M0, f32) for _ in range(NG))
    l0 = tuple(jnp.zeros((HG, QB, 1), f32) for _ in range(NG))
    carry = lax.fori_loop(0, nch // 2, lambda c2, cr: att_body(2 * c2 + 1, att_body(2 * c2, cr)), (m0, l0))
    _, l_fin = lax.cond(nch % 2 == 1, lambda cr: att_body(nch - 1, cr), lambda cr: cr, carry)
    for g in range(NG):
        o_lat = (acc_scr[g * RG:(g + 1) * RG] / l_fin[g].reshape(RG, 1)).astype(MXU_DT)
        for k in range(HG):
            hd = g * HG + k
            out_ref[0, :, hd * C_DV:(hd + 1) * C_DV] = _dot(o_lat[k * QB:(k + 1) * QB], wuv_ref[hd]).astype(out_ref.dtype)


def _dsa(qa, kh, kv, qi, ki, wi, w_uv):
    B, H, T, _ = qa.shape
    QB = min(DSA_Q_BLOCK, T)
    KC = min(KEY_CHUNK, T)
    topk = min(C_TOPK, T // 4)
    wuv = w_uv.transpose(1, 0, 2).astype(MXU_DT)
    assert T // PACKED_ROWS <= 256
    eye = jnp.eye(QB, dtype=MXU_DT)
    tri = jnp.tril(jnp.ones((KC, KC), MXU_DT))
    kern = functools.partial(_dsa_kernel, QB=QB, KC=KC, SUB=min(KEY_SUB, KC), TOPK=topk, HG=DSA_HEAD_GROUP)
    half_words = pltpu.VMEM((T // KC, KC // PACKED_ROWS, PACKED_ROWS, QB), jnp.int16)
    return pl.pallas_call(
        kern,
        grid=(B, T // QB),
        in_specs=[pl.BlockSpec((1, H, QB, C_KDIM), lambda b, i: (b, 0, i, 0)),
                  pl.BlockSpec((1, H, QB, LANES), lambda b, i: (b, 0, i, 0)),
                  pl.BlockSpec((1, QB, LANES), lambda b, i: (b, i, 0)),
                  pl.BlockSpec((1, H, T, C_KDIM), lambda b, i: (b, 0, 0, 0), pipeline_mode=pl.Buffered(1)),
                  pl.BlockSpec((1, T, C_KVL), lambda b, i: (b, 0, 0)),
                  pl.BlockSpec((1, T, LANES), lambda b, i: (b, 0, 0)),
                  _const_spec(wuv.shape), _const_spec(eye.shape), _const_spec(tri.shape)],
        out_specs=pl.BlockSpec((1, QB, H * C_DV), lambda b, i: (b, i, 0)),
        out_shape=jax.ShapeDtypeStruct((B, T, H * C_DV), MXU_DT),
        scratch_shapes=[pltpu.VMEM((T // KC, KC, QB), jnp.int32),
                        half_words, half_words,
                        pltpu.VMEM((H * QB, C_KVL), jnp.float32)],
        compiler_params=_cparams("parallel", "arbitrary"),
    )(qa, qi, wi, kh, kv, ki, wuv, eye, tri)


def _post_kernel(*refs, n_mix, n_ff):
    h_ref = refs[0]
    mix = refs[1:1 + 2 * n_mix]
    g1, b1, w1_ref, w2_ref, g2, b2, wg_ref, p_ref, wp_ref, out_ref = refs[1 + 2 * n_mix:]
    h = h_ref[...]
    y = _dot(mix[0][...].astype(MXU_DT), mix[1][...])
    for k in range(1, n_mix):
        y = y + _dot(mix[2 * k][...].astype(MXU_DT), mix[2 * k + 1][...])
    h1 = _layer_norm(DN_ALPHA * h + y, g1[...], b1[...])
    h1b = h1.astype(MXU_DT)
    ff = D_FF // n_ff
    u = None
    for k in range(n_ff):
        a = jnp.square(jnp.maximum(_dot(h1b, w1_ref[:, k * ff:(k + 1) * ff]), 0.0))
        t = _dot(a.astype(MXU_DT), w2_ref[k * ff:(k + 1) * ff, :])
        u = t if u is None else u + t
    h2 = _layer_norm(DN_ALPHA * h1 + u, g2[...], b2[...])
    gate = jax.nn.sigmoid(_dot(h2.astype(MXU_DT), wg_ref[...]))
    out_ref[...] = h2 + gate * _dot(p_ref[0].astype(MXU_DT), wp_ref[...])


def _post(h2d, mixes, ln1_g, ln1_b, w1, w2, ln2_g, ln2_b, wg, p_all, layer, wp, tm):
    M, D = h2d.shape
    row = lambda w: pl.BlockSpec((tm, w), lambda i: (i, 0))
    vec = lambda v: v[None, :]
    in_specs = [row(D)]
    args = [h2d]
    for x, w in mixes:
        in_specs += [row(x.shape[1]), _const_spec(w.shape)]
        args += [x, w]
    in_specs += [_const_spec((1, D)), _const_spec((1, D)), _const_spec(w1.shape), _const_spec(w2.shape),
                 _const_spec((1, D)), _const_spec((1, D)), _const_spec(wg.shape),
                 pl.BlockSpec((1, tm, D_PLE), lambda i: (layer, i, 0)), _const_spec(wp.shape)]
    args += [vec(ln1_g), vec(ln1_b), w1, w2, vec(ln2_g), vec(ln2_b), wg, p_all, wp]
    kern = functools.partial(_post_kernel, n_mix=len(mixes), n_ff=4)
    return pl.pallas_call(
        kern,
        grid=(M // tm,),
        in_specs=in_specs,
        out_specs=row(D),
        out_shape=jax.ShapeDtypeStruct((M, D), jnp.float32),
        compiler_params=_cparams("parallel"),
    )(*args)


def kernel(x, p, positions, e_w_in, e_a_conv, e_a_i_b, e_a_f_b, e_a_norm, e_b_cmp_pos, e_b_cmp_w1, e_b_cmp_w2, e_b_g_b, e_w_out, o_w_in, o_q_norm, o_kv_norm, o_w_qb, o_w_uk, o_w_uv, o_w_iq, o_ik_g, o_ik_b, o_w_out, ln1_g, ln1_b, ln2_g, ln2_b, mlp_w1, mlp_w2, ple_gate_w, ple_w):
    B, T, D = x.shape
    M = B * T
    tm = min(ROW_TILE, T)
    h = x
    pos3 = positions[..., None]
    p_all = p.reshape(DEPTH, M, D_PLE)
    bf = lambda w: w.astype(MXU_DT)
    for i in range(DEPTH):
        j = i // 2
        if i % 2 == 0:
            qk, av, ao, small, bq, bc, bs = _even_proj(h, _even_w_in_aug(e_w_in[j]), tm)
            ya = _mlstm(qk, av, ao, small, e_a_conv[j], e_a_i_b[j], e_a_f_b[j], e_a_norm[j])
            cmp = _nsa_compress(bc, e_b_cmp_pos[j], e_b_cmp_w1[j], e_b_cmp_w2[j])
            yb = _nsa(bq, bs, cmp, small, e_b_g_b[j])
            w_out = e_w_out[j]
            mixes = [(ya.reshape(M, A_W), bf(w_out[:A_W])),
                     (yb.reshape(M, B_HEADS * LANES), bf(_pad_heads(w_out[A_W:], B_HEADS, B_DH, axis=0)))]
        else:
            qa, kh, kv, qi, ki, wi = _odd_prep(h, pos3, o_w_in[j], o_q_norm[j], o_kv_norm[j], o_w_qb[j], o_w_uk[j],
                                               o_w_iq[j], o_ik_g[j], o_ik_b[j], tm)
            o = _dsa(qa, kh, kv, qi, ki, wi, o_w_uv[j])
            mixes = [(o.reshape(M, C_HEADS * C_DV), bf(o_w_out[j]))]
        h = _post(h.reshape(M, D), mixes, ln1_g[i], ln1_b[i], bf(mlp_w1[i]), bf(mlp_w2[i]), ln2_g[i], ln2_b[i],
                  bf(ple_gate_w[i]), p_all, i, bf(ple_w[i]), min(MLP_ROW_TILE, T)).reshape(B, T, D)
    return h
```

```python
import functools

import numpy as np
import jax
import jax.numpy as jnp
from jax import lax
from jax.experimental import pallas as pl
from jax.experimental.pallas import tpu as pltpu

D_MODEL = 1024
DEPTH = 4
D_PLE = 256
D_FF = 4 * D_MODEL
DN_ALPHA = (2.0 * DEPTH) ** 0.25
LN_EPS = 1e-5
NEG = -1e30

A_HEADS = 4
A_DH = D_MODEL // 8
A_W = A_HEADS * A_DH
A_CONV = 4

B_HEADS = 8
B_DH = 64
B_KV = 2
B_HPG = B_HEADS // B_KV
B_CMP_LEN = 32
B_CMP_STRIDE = 16
B_CMP_HID = 128
B_SEL_BLK = 64
B_SEL_N = 16
B_WIN = 512

C_HEADS = 8
C_DN = 128
C_DR = 64
C_DV = 128
C_QL = 512
C_KVL = 256
C_IDX_HEADS = 8
C_IDX_DH = 64
C_IDX_DR = 32
C_TOPK = 256
ROPE_BASE = 10000.0

LANES = 128
SUBLANES = 8
PACKED_ROWS = 16
VMEM_LIMIT_BYTES = 56 * 2**20
MXU_DT = jnp.bfloat16

MLSTM_CHUNK = 256
MLSTM_SEQS = 1
ROW_TILE = 512
MLP_ROW_TILE = 512
Q_BLOCK = 256
DSA_Q_BLOCK = 256
DSA_HEAD_GROUP = 4
KEY_CHUNK = 512
KEY_SUB = 512
C_KDIM = C_DN + LANES

LOG2E = 1.4426950408889634
SOFTMAX_M0 = 0.5 * NEG
SOFTMAX_TINY = 1e-30


def _cparams(*sem):
    return pltpu.CompilerParams(dimension_semantics=sem, vmem_limit_bytes=VMEM_LIMIT_BYTES)


def _const_spec(shape):
    nd = len(shape)
    return pl.BlockSpec(shape, lambda *_: (0,) * nd, pipeline_mode=pl.Buffered(1))


def _dot(a, b):
    return jnp.dot(a, b, preferred_element_type=jnp.float32)


def _dot_nt(a, b):
    return lax.dot_general(a, b, (((1,), (1,)), ((), ())), preferred_element_type=jnp.float32)


def _dot_tn(a, b):
    return lax.dot_general(a, b, (((0,), (0,)), ((), ())), preferred_element_type=jnp.float32)


def _dot_f32(a, b):
    return jnp.dot(a, b, preferred_element_type=jnp.float32, precision=lax.Precision.HIGHEST)


def _layer_norm(x, g, b):
    mu = jnp.mean(x, axis=-1, keepdims=True)
    var = jnp.mean(jnp.square(x - mu), axis=-1, keepdims=True)
    return (x - mu) * lax.rsqrt(var + LN_EPS) * g + b


def _masked_softmax2(s, bias):
    s = s + bias
    m = jnp.maximum(jnp.max(s, axis=-1, keepdims=True), SOFTMAX_M0)
    e = jnp.exp2(s - m)
    return e / jnp.maximum(jnp.sum(e, axis=-1, keepdims=True), SOFTMAX_TINY)


def _iota(shape, dim):
    return lax.broadcasted_iota(jnp.int32, shape, dim)


def _pad_heads(w, n_heads, dh, axis=-1):
    axis = axis % w.ndim
    shp = w.shape[:axis] + (n_heads, dh) + w.shape[axis + 1:]
    w = w.reshape(shp)
    pad = [(0, 0)] * w.ndim
    pad[axis + 1] = (0, LANES - dh)
    w = jnp.pad(w, pad)
    return w.reshape(shp[:axis] + (n_heads * LANES,) + shp[axis + 2:])


def _pad_cols(w, width):
    return jnp.pad(w, [(0, 0)] * (w.ndim - 1) + [(0, width - w.shape[-1])])


def _even_proj_kernel(h_ref, w_ref, blk_ref, qk_ref, av_ref, ao_ref, sm_ref, bq_ref, bc_ref, bs_ref):
    z = _dot(h_ref[0].astype(MXU_DT), w_ref[...])
    o = 0
    qk_ref[0] = z[:, o:o + 2 * A_W]; o += 2 * A_W
    av_ref[0] = z[:, o:o + A_W].astype(av_ref.dtype); o += A_W
    ao_ref[0] = z[:, o:o + A_W]; o += A_W
    sm_ref[0] = z[:, o:o + LANES]; o += LANES
    for hd in range(B_HEADS):
        bq_ref[0, hd] = z[:, o:o + LANES].astype(bq_ref.dtype); o += LANES
    for j in range(2 * B_KV):
        bc_ref[0, j] = z[:, o:o + LANES]; o += LANES
    for j in range(4 * B_KV):
        zj = z[:, o:o + LANES]; o += LANES
        if j < B_KV:
            zj = zj + blk_ref[...]
        bs_ref[0, j] = zj.astype(bs_ref.dtype)


def _even_proj(h3, w_aug, tm):
    B, T, D = h3.shape
    n = w_aug.shape[1]
    f32 = jnp.float32
    out_shape = (
        jax.ShapeDtypeStruct((B, T, 2 * A_W), f32),
        jax.ShapeDtypeStruct((B, T, A_W), MXU_DT),
        jax.ShapeDtypeStruct((B, T, A_W), f32),
        jax.ShapeDtypeStruct((B, T, LANES), f32),
        jax.ShapeDtypeStruct((B, B_HEADS, T, LANES), MXU_DT),
        jax.ShapeDtypeStruct((B, 2 * B_KV, T, LANES), f32),
        jax.ShapeDtypeStruct((B, 4 * B_KV, T, LANES), MXU_DT),
    )
    row = lambda w: pl.BlockSpec((1, tm, w), lambda b, i: (b, i, 0))
    hm = lambda nh: pl.BlockSpec((1, nh, tm, LANES), lambda b, i: (b, 0, i, 0))
    assert T // B_SEL_BLK <= LANES - B_DH
    blk = (np.arange(LANES)[None, :] == B_DH + np.arange(T)[:, None] // B_SEL_BLK).astype(np.float32)
    return pl.pallas_call(
        _even_proj_kernel,
        grid=(B, T // tm),
        in_specs=[row(D), _const_spec((D, n)), pl.BlockSpec((tm, LANES), lambda b, i: (i, 0))],
        out_specs=(row(2 * A_W), row(A_W), row(A_W), row(LANES), hm(B_HEADS), hm(2 * B_KV), hm(4 * B_KV)),
        out_shape=out_shape,
        compiler_params=_cparams("parallel", "parallel"),
    )(h3, w_aug, jnp.asarray(blk))


def _even_w_in_aug(w_in):
    sizes = (A_W, A_W, A_W, A_W, A_HEADS, A_HEADS, B_HEADS * B_DH) + (B_KV * B_DH,) * 6 + (3 * B_HEADS,)
    parts, o = [], 0
    for s in sizes:
        parts.append(w_in[:, o:o + s]); o += s
    aq, ak, av, ao, ai, af, bq, bkc, bvc, bks, bvs, bkw, bvw, bg = parts
    small = _pad_cols(jnp.concatenate([ai, af, bg], -1), LANES)
    ph = lambda w: _pad_heads(w, B_KV, B_DH)
    bq = bq * (B_DH ** -0.5 * LOG2E)
    cols = [aq, ak, av, ao, small, _pad_heads(bq, B_HEADS, B_DH),
            ph(bkc), ph(bvc), ph(bks), ph(bvs), ph(bkw), ph(bvw)]
    return jnp.concatenate(cols, -1).astype(MXU_DT)


def _mlstm_kernel(qk_ref, v_ref, o_ref, gc_ref, gr_ref, cw_ref, bc_ref, br_ref, ng_ref, tri_ref, out_ref,
                  xs_scr, ct_scr, n_scr, m_scr, *, L, NB):
    c = pl.program_id(1)
    f32 = jnp.float32
    W2 = 2 * A_W

    @pl.when(c == 0)
    def _():
        xs_scr[:, 0:SUBLANES, :] = jnp.zeros((NB, SUBLANES, W2), f32)
        ct_scr[...] = jnp.zeros(ct_scr.shape, f32)
        n_scr[...] = jnp.zeros(n_scr.shape, f32)
        m_scr[...] = jnp.zeros(m_scr.shape, f32)

    tri = tri_ref[...]
    causal = _iota((L, L), 1) <= _iota((L, L), 0)
    cw = cw_ref[...]
    for bb in range(NB):
        _mlstm_chunk(bb, qk_ref, v_ref, o_ref, gc_ref, gr_ref, cw, bc_ref, br_ref, ng_ref, tri, causal, out_ref,
                     xs_scr, ct_scr, n_scr, m_scr, L)


def _mlstm_chunk(bb, qk_ref, v_ref, o_ref, gc_ref, gr_ref, cw, bc_ref, br_ref, ng_ref, tri, causal, out_ref,
                 xs_scr, ct_scr, n_scr, m_scr, L):
    f32 = jnp.float32
    xs_scr[bb, SUBLANES:SUBLANES + L, :] = qk_ref[bb]
    base = SUBLANES - (A_CONV - 1)
    acc = xs_scr[bb, base:base + L, :] * cw[0:1, :]
    for j in range(1, A_CONV):
        acc = acc + xs_scr[bb, base + j:base + j + L, :] * cw[j:j + 1, :]
    qk = acc * jax.nn.sigmoid(acc)
    xs_scr[bb, 0:SUBLANES, :] = xs_scr[bb, L:L + SUBLANES, :]

    gc = gc_ref[bb] + bc_ref[...]
    gr = gr_ref[bb, 0] + br_ref[...]
    ig_c = gc[:, 0:A_HEADS]
    b_c = _dot_f32(tri, jax.nn.log_sigmoid(gc[:, A_HEADS:2 * A_HEADS]))
    ig_r = gr[0:A_HEADS, :]
    b_r = _dot_f32(jax.nn.log_sigmoid(gr[A_HEADS:2 * A_HEADS, :]), tri.T)

    for hd in range(A_HEADS):
        sl = slice(hd * A_DH, (hd + 1) * A_DH)
        q_h = (qk[:, sl] * A_DH ** -0.5).astype(MXU_DT)
        k_f = qk[:, A_W + hd * A_DH:A_W + (hd + 1) * A_DH]
        k_h = k_f.astype(MXU_DT)
        v_h = v_ref[bb][:, sl]
        bi = b_c[:, hd:hd + 1]
        ic = ig_c[:, hd:hd + 1]
        dmat = jnp.where(causal, bi - b_r[hd:hd + 1, :] + ig_r[hd:hd + 1, :], NEG)
        m_prev = m_scr[bb, hd][:, 0:1]
        m_inter = bi + m_prev
        m_t = jnp.maximum(m_inter, jnp.max(dmat, axis=1, keepdims=True))
        e_inter = jnp.exp(m_inter - m_t)
        s = _dot_nt(q_h, k_h) * jnp.exp(dmat - m_t)
        ct = ct_scr[bb, hd]
        nrow = n_scr[bb, hd]
        num = e_inter * _dot(q_h, ct.astype(MXU_DT)) + _dot(s.astype(MXU_DT), v_h)
        den = e_inter * jnp.sum(q_h.astype(f32) * nrow, axis=1, keepdims=True) + jnp.sum(s, axis=1, keepdims=True)
        hc = num / jnp.maximum(jnp.abs(den), jnp.exp(-m_t))
        b_last = bi[L - 1:L, :]
        dec = b_last - bi + ic
        m_new = jnp.maximum(b_last + m_prev, jnp.max(dec, axis=0, keepdims=True))
        wgt = jnp.exp(dec - m_new)
        e_st = jnp.exp(b_last + m_prev - m_new)
        kw = k_f * wgt
        ct_scr[bb, hd] = e_st * ct + _dot_tn(kw.astype(MXU_DT), v_h)
        n_scr[bb, hd] = e_st * nrow + jnp.sum(kw, axis=0, keepdims=True)
        m_scr[bb, hd] = jnp.broadcast_to(m_new, (1, LANES))
        mu = jnp.mean(hc, axis=1, keepdims=True)
        var = jnp.mean(jnp.square(hc - mu), axis=1, keepdims=True)
        hn = (hc - mu) * lax.rsqrt(var + LN_EPS) * ng_ref[:, sl]
        out_ref[bb, :, sl] = (hn * jax.nn.sigmoid(o_ref[bb][:, sl])).astype(out_ref.dtype)


def _mlstm(qk, av, ao, small, conv_w, i_b, f_b, norm_g):
    B, T, _ = qk.shape
    L = min(MLSTM_CHUNK, T)
    N = T // L
    f32 = jnp.float32
    gates = small[..., 0:2 * A_HEADS]
    gates_r = gates.reshape(B, N, L, 2 * A_HEADS).transpose(0, 1, 3, 2)
    bias = jnp.concatenate([i_b, f_b]).astype(f32)
    tri = jnp.tril(jnp.ones((L, L), f32))
    NB = MLSTM_SEQS if B % MLSTM_SEQS == 0 else 1
    row = lambda w: pl.BlockSpec((NB, L, w), lambda b, c: (b, c, 0))
    kern = functools.partial(_mlstm_kernel, L=L, NB=NB)
    return pl.pallas_call(
        kern,
        grid=(B // NB, N),
        in_specs=[row(2 * A_W), row(A_W), row(A_W), row(2 * A_HEADS),
                  pl.BlockSpec((NB, 1, 2 * A_HEADS, L), lambda b, c: (b, c, 0, 0)),
                  _const_spec((A_CONV, 2 * A_W)), _const_spec((1, 2 * A_HEADS)), _const_spec((2 * A_HEADS, 1)),
                  _const_spec((1, A_W)), _const_spec((L, L))],
        out_specs=row(A_W),
        out_shape=jax.ShapeDtypeStruct((B, T, A_W), MXU_DT),
        scratch_shapes=[pltpu.VMEM((NB, L + 2 * SUBLANES, 2 * A_W), f32),
                        pltpu.VMEM((NB, A_HEADS, A_DH, A_DH), f32),
                        pltpu.VMEM((NB, A_HEADS, 1, A_DH), f32),
                        pltpu.VMEM((NB, A_HEADS, 1, LANES), f32)],
        compiler_params=_cparams("parallel", "arbitrary"),
    )(qk, av, ao, gates, gates_r, conv_w, bias[None, :], bias[:, None], norm_g[None, :], tri)


def _nsa_cmp_kernel(x_ref, w1a_ref, w1b_ref, w2_ref, pos_ref, out_ref):
    n = x_ref.shape[2] // B_CMP_STRIDE
    half = B_CMP_STRIDE * LANES
    for j in range(2):
        bias = (_dot(pos_ref[j, :, 0:half].astype(MXU_DT), w1a_ref[j])
                + _dot(pos_ref[j, :, half:2 * half].astype(MXU_DT), w1b_ref[j]))[0:1, :]
        for g in range(B_KV):
            u = jnp.concatenate([x_ref[0, j * B_KV + g, pl.ds(r, n, stride=B_CMP_STRIDE), :]
                                 for r in range(B_CMP_STRIDE)], axis=1).astype(MXU_DT)
            a = _dot(u, w1a_ref[j])
            bm = _dot(u, w1b_ref[j])
            pre = a + pltpu.roll(bm, n - 1, 0) + bias
            hid = jax.nn.gelu(pre)
            out_ref[0, j * B_KV + g] = _dot(hid.astype(MXU_DT), w2_ref[j])


def _nsa_compress(bc, cmp_pos, cmp_w1, cmp_w2):
    B, _, T, _ = bc.shape
    nblk = T // B_CMP_STRIDE
    half = B_CMP_STRIDE * LANES
    w1 = jnp.pad(cmp_w1.reshape(2, B_CMP_LEN, B_DH, B_CMP_HID), ((0, 0), (0, 0), (0, LANES - B_DH), (0, 0)))
    w1 = w1.reshape(2, B_CMP_LEN * LANES, B_CMP_HID).astype(MXU_DT)
    w1a, w1b = w1[:, :half], w1[:, half:]
    w2 = _pad_cols(cmp_w2, LANES).astype(MXU_DT)
    pos = jnp.pad(cmp_pos, ((0, 0), (0, 0), (0, LANES - B_DH))).reshape(2, 1, B_CMP_LEN * LANES)
    pos = jnp.broadcast_to(pos, (2, SUBLANES, B_CMP_LEN * LANES))
    return pl.pallas_call(
        _nsa_cmp_kernel,
        grid=(B,),
        in_specs=[pl.BlockSpec((1, 2 * B_KV, T, LANES), lambda b: (b, 0, 0, 0)),
                  _const_spec(w1a.shape), _const_spec(w1b.shape), _const_spec(w2.shape), _const_spec(pos.shape)],
        out_specs=pl.BlockSpec((1, 2 * B_KV, nblk, LANES), lambda b: (b, 0, 0, 0)),
        out_shape=jax.ShapeDtypeStruct((B, 2 * B_KV, nblk, LANES), jnp.float32),
        compiler_params=_cparams("parallel"),
    )(bc, w1a, w1b, w2, pos)


def _nsa_kernel(q_ref, kv_ref, cmp_ref, sm_ref, gb_ref, ovt_ref, out_ref, acc_scr, ocmp_scr, qaug_scr,
                *, QB, KC, WL, NSB, NSEL):
    f32 = jnp.float32
    i = pl.program_id(1)
    s0 = i * QB
    R = B_HPG * QB
    tq_col = s0 + _iota((QB, 1), 0)
    tq_row = s0 + _iota((1, QB), 1)
    nch = (s0 + QB + KC - 1) // KC
    gates = jax.nn.sigmoid(sm_ref[0] + gb_ref[...])
    ncmp = cmp_ref.shape[2]
    cmp_end = _iota((1, ncmp), 1) * B_CMP_STRIDE + (B_CMP_LEN - 1)
    cmp_bias = jnp.where(cmp_end <= tq_col, 0.0, NEG)[None]
    jb = _iota((NSB, 1), 0)
    cur = lax.shift_right_logical(tq_row, int(np.log2(B_SEL_BLK)))
    forced = (jb == 0) | (jb == cur) | (jb == cur - 1)
    valid = jb * B_SEL_BLK <= tq_row
    wstart = pl.multiple_of(jnp.maximum(s0 + QB - WL, 0), QB)
    wpos = wstart + _iota((1, WL), 1)
    win_bias = jnp.where((wpos <= tq_col) & (wpos > tq_col - B_WIN), 0.0, NEG)[None]

    for g in range(B_KV):
        qs = q_ref[0, g * B_HPG:(g + 1) * B_HPG].reshape(R, LANES)
        kcm = cmp_ref[0, g].astype(MXU_DT)
        vcm = cmp_ref[0, B_KV + g].astype(MXU_DT)
        p_cmp = _masked_softmax2(_dot_nt(qs, kcm).reshape(B_HPG, QB, ncmp), cmp_bias)
        o_cmp = _dot(p_cmp.reshape(R, ncmp).astype(MXU_DT), vcm)
        kw_ = kv_ref[0, 2 * B_KV + g, pl.ds(wstart, WL), :]
        vw_ = kv_ref[0, 3 * B_KV + g, pl.ds(wstart, WL), :]
        p_win = _masked_softmax2(_dot_nt(qs, kw_).reshape(B_HPG, QB, WL), win_bias)
        o_win = _dot(p_win.reshape(R, WL).astype(MXU_DT), vw_)
        for hd in range(B_HPG):
            c0 = 2 * A_HEADS + (g * B_HPG + hd) * 3
            rs = slice(hd * QB, (hd + 1) * QB)
            ocmp_scr[g * R + hd * QB:g * R + (hd + 1) * QB] = (gates[:, c0:c0 + 1] * o_cmp[rs]
                                                               + gates[:, c0 + 2:c0 + 3] * o_win[rs])
        psum = jnp.sum(p_cmp, axis=0)
        imp_t = lax.dot_general(ovt_ref[...], psum, (((1,), (1,)), ((), ())),
                                preferred_element_type=f32, precision=lax.Precision.HIGHEST)
        sc = jnp.where(forced, 1e6, imp_t)
        sc = jnp.where(valid, sc, NEG)
        rank = jnp.zeros((NSB, QB), f32)
        for j in range(NSB):
            cj = jnp.broadcast_to(sc[j:j + 1, :], (NSB, QB))
            ahead = (cj > sc) | ((cj == sc) & (jb > j))
            rank = rank + jnp.where(ahead, 1.0, 0.0)
        drop = jnp.where(rank < NSEL, 0.0, NEG)
        pad = [jnp.zeros((B_DH, QB), f32), drop]
        if B_DH + NSB < LANES:
            pad.append(jnp.zeros((LANES - B_DH - NSB, QB), f32))
        drop_t = jnp.concatenate(pad, axis=0).T.astype(MXU_DT)
        for hd in range(B_HPG):
            rows = slice((g * B_HPG + hd) * QB, (g * B_HPG + hd + 1) * QB)
            qaug_scr[rows] = q_ref[0, g * B_HPG + hd] + drop_t

    acc_scr[...] = jnp.zeros(acc_scr.shape, f32)

    def sel_body(c, carry, diagonal=False):
        ms, ls = list(carry[0]), list(carry[1])
        ks = pl.multiple_of(c * KC, KC)
        for g in range(B_KV):
            kc_ = kv_ref[0, 0 * B_KV + g, pl.ds(ks, KC), :]
            vc_ = kv_ref[0, 1 * B_KV + g, pl.ds(ks, KC), :]
            s3 = _dot_nt(qaug_scr[g * R:(g + 1) * R], kc_).reshape(B_HPG, QB, KC)
            if diagonal:
                s3 = s3 + jnp.where(ks + _iota((1, KC), 1) <= tq_col, 0.0, NEG)[None]
            m_new = jnp.maximum(ms[g], jnp.max(s3, axis=-1, keepdims=True))
            p = jnp.exp2(s3 - m_new)
            alpha = jnp.exp2(ms[g] - m_new)
            ls[g] = alpha * ls[g] + jnp.sum(p, axis=-1, keepdims=True)
            ms[g] = m_new
            rows = slice(g * R, (g + 1) * R)
            acc_scr[rows] = alpha.reshape(R, 1) * acc_scr[rows] + _dot(p.reshape(R, KC).astype(MXU_DT), vc_)
        return tuple(ms), tuple(ls)

    m0 = tuple(jnp.full((B_HPG, QB, 1), SOFTMAX_M0, f32) for _ in range(B_KV))
    l0 = tuple(jnp.zeros((B_HPG, QB, 1), f32) for _ in range(B_KV))
    nfull = nch - 1
    carry = lax.fori_loop(0, nfull // 2, lambda c2, cr: sel_body(2 * c2 + 1, sel_body(2 * c2, cr)), (m0, l0))
    carry = lax.cond(nfull % 2 == 1, lambda cr: sel_body(nfull - 1, cr), lambda cr: cr, carry)
    _, l_fin = sel_body(nfull, carry, diagonal=True)

    for g in range(B_KV):
        rows = slice(g * R, (g + 1) * R)
        o_sel = acc_scr[rows] / l_fin[g].reshape(R, 1)
        o_two = ocmp_scr[rows]
        for hd in range(B_HPG):
            c0 = 2 * A_HEADS + (g * B_HPG + hd) * 3
            rs = slice(hd * QB, (hd + 1) * QB)
            o = o_two[rs] + gates[:, c0 + 1:c0 + 2] * o_sel[rs]
            col = (g * B_HPG + hd) * LANES
            out_ref[0, :, col:col + LANES] = o.astype(out_ref.dtype)


def _nsa(bq, bs, cmp, small, g_b):
    B, _, T, _ = bq.shape
    QB = min(Q_BLOCK, T)
    KC = min(KEY_CHUNK, T)
    assert KC % QB == 0
    WL = min(B_WIN + QB, T)
    NSB = T // B_SEL_BLK
    NSEL = min(B_SEL_N, NSB)
    ncmp = cmp.shape[2]
    f32 = jnp.float32
    M = (T - B_CMP_LEN) // B_CMP_STRIDE + 1
    assert NSB % SUBLANES == 0 and NSB <= LANES
    mi, jj = np.arange(ncmp)[None, :], np.arange(NSB)[:, None]
    ovt = ((mi * B_CMP_STRIDE < (jj + 1) * B_SEL_BLK) & (mi * B_CMP_STRIDE + B_CMP_LEN > jj * B_SEL_BLK)
           & (mi < M)).astype(np.float32)
    gb = jnp.zeros((1, LANES), f32).at[0, 2 * A_HEADS:2 * A_HEADS + 3 * B_HEADS].set(g_b)
    kern = functools.partial(_nsa_kernel, QB=QB, KC=KC, WL=WL, NSB=NSB, NSEL=NSEL)
    return pl.pallas_call(
        kern,
        grid=(B, T // QB),
        in_specs=[pl.BlockSpec((1, B_HEADS, QB, LANES), lambda b, i: (b, 0, i, 0)),
                  pl.BlockSpec((1, 4 * B_KV, T, LANES), lambda b, i: (b, 0, 0, 0)),
                  pl.BlockSpec((1, 2 * B_KV, ncmp, LANES), lambda b, i: (b, 0, 0, 0)),
                  pl.BlockSpec((1, QB, LANES), lambda b, i: (b, i, 0)),
                  _const_spec((1, LANES)), _const_spec(ovt.shape)],
        out_specs=pl.BlockSpec((1, QB, B_HEADS * LANES), lambda b, i: (b, i, 0)),
        out_shape=jax.ShapeDtypeStruct((B, T, B_HEADS * LANES), MXU_DT),
        scratch_shapes=[pltpu.VMEM((B_HEADS * QB, LANES), f32), pltpu.VMEM((B_HEADS * QB, LANES), f32),
                        pltpu.VMEM((B_HEADS * QB, LANES), MXU_DT)],
        compiler_params=_cparams("parallel", "arbitrary"),
    )(bq, bs, cmp, small, gb, jnp.asarray(ovt))


def _rope_rows(d):
    inv = ROPE_BASE ** (-jnp.arange(0, d, 2, dtype=jnp.float32) / d)
    z = jnp.zeros((C_DR - d,), jnp.float32)
    inv64 = jnp.concatenate([inv, inv, z])
    sgn64 = jnp.concatenate([-jnp.ones(d // 2), jnp.ones(d // 2), z]).astype(jnp.float32)
    return inv64, sgn64


def _odd_prep_kernel(h_ref, pos_ref, win_ref, wqb_ref, wiq_ref, wuk_ref, qn_ref, kvn_ref, ikg_ref, ikb_ref,
                     rope_ref, perm_ref, qa_ref, kh_ref, kv_ref, qi_ref, ki_ref, wi_ref):
    f32 = jnp.float32
    z = _dot(h_ref[0].astype(MXU_DT), win_ref[...])
    pos = pos_ref[0].astype(f32)
    rr = rope_ref[...]
    ang_q = pos * rr[0:1, :]
    cos_q, sin_q = jnp.cos(ang_q), jnp.sin(ang_q) * rr[1:2, :]
    ang_i = pos * rr[2:3, :]
    cos_i, sin_i = jnp.cos(ang_i), jnp.sin(ang_i) * rr[3:4, :]

    def rms(x, g):
        return x * lax.rsqrt(jnp.mean(jnp.square(x), axis=-1, keepdims=True) + LN_EPS) * g

    cq = rms(z[:, 0:C_QL], qn_ref[...])
    ckv = rms(z[:, C_QL:C_QL + C_KVL], kvn_ref[...])
    o = C_QL + C_KVL
    k_rope = (z[:, o:o + LANES] * cos_q + z[:, o + LANES:o + 2 * LANES] * sin_q).astype(kh_ref.dtype)
    ckvb = ckv.astype(kv_ref.dtype)
    kv_ref[0] = ckvb
    for hd in range(C_HEADS):
        kh_ref[0, hd, :, 0:C_DN] = _dot(ckvb, wuk_ref[hd]).astype(kh_ref.dtype)
        kh_ref[0, hd, :, C_DN:C_KDIM] = k_rope
    ik = z[:, o + 2 * LANES:o + 3 * LANES]
    real = _iota((1, LANES), 1) < C_IDX_DH
    mu = jnp.sum(ik, axis=-1, keepdims=True) / C_IDX_DH
    dlt = jnp.where(real, ik - mu, 0.0)
    var = jnp.sum(jnp.square(dlt), axis=-1, keepdims=True) / C_IDX_DH
    ki = dlt * lax.rsqrt(var + LN_EPS) * ikg_ref[...] + ikb_ref[...]
    ki = ki * cos_i + _dot_f32(ki, perm_ref[...]) * sin_i
    ki_ref[0] = ki.astype(ki_ref.dtype)
    wi_ref[0] = z[:, o + 3 * LANES:o + 4 * LANES] * (C_IDX_HEADS ** -0.5 * C_IDX_DH ** -0.5)
    cqb = cq.astype(MXU_DT)
    qf = _dot(cqb, wqb_ref[...])
    qi = _dot(cqb, wiq_ref[...])
    scale = (C_DN + C_DR) ** -0.5 * LOG2E
    nh = C_HEADS * LANES
    for hd in range(C_HEADS):
        cs = slice(hd * LANES, (hd + 1) * LANES)
        q_rope = qf[:, nh:2 * nh][:, cs] * cos_q + qf[:, 2 * nh:3 * nh][:, cs] * sin_q
        qa_ref[0, hd, :, 0:C_DN] = (qf[:, hd * C_DN:(hd + 1) * C_DN] * scale).astype(qa_ref.dtype)
        qa_ref[0, hd, :, C_DN:C_KDIM] = (q_rope * scale).astype(qa_ref.dtype)
        qi_h = qi[:, 0:nh][:, cs] * cos_i + qi[:, nh:2 * nh][:, cs] * sin_i
        qi_ref[0, hd] = qi_h.astype(qi_ref.dtype)


def _rot_cols(w, n_heads, dh, d):
    w = w.reshape(w.shape[0], n_heads, dh)
    h = d // 2
    return jnp.concatenate([w[..., h:d], w[..., 0:h], jnp.zeros_like(w[..., d:])], -1).reshape(w.shape[0], n_heads * dh)


def _odd_prep(h3, pos3, w_in, q_norm, kv_norm, w_qb, w_uk, w_iq, ik_g, ik_b, tm):
    B, T, D = h3.shape
    f32 = jnp.float32
    o = 0
    parts = []
    for s in (C_QL, C_KVL, C_DR, C_IDX_DH, C_IDX_HEADS):
        parts.append(w_in[:, o:o + s]); o += s
    w_cq, w_ckv, w_kr, w_ik, w_iw = parts
    pc = lambda w: _pad_cols(w, LANES)
    win = jnp.concatenate([w_cq, w_ckv, pc(w_kr), pc(_rot_cols(w_kr, 1, C_DR, C_DR)), pc(w_ik), pc(w_iw)],
                          -1).astype(MXU_DT)
    wq = w_qb.reshape(C_QL, C_HEADS, C_DN + C_DR)
    w_nope = wq[..., :C_DN].reshape(C_QL, C_HEADS * C_DN)
    w_rope = wq[..., C_DN:].reshape(C_QL, C_HEADS * C_DR)
    wqb = jnp.concatenate([w_nope, _pad_heads(w_rope, C_HEADS, C_DR),
                           _pad_heads(_rot_cols(w_rope, C_HEADS, C_DR, C_DR), C_HEADS, C_DR)], -1).astype(MXU_DT)
    wiq = jnp.concatenate([_pad_heads(w_iq, C_IDX_HEADS, C_IDX_DH),
                           _pad_heads(_rot_cols(w_iq, C_IDX_HEADS, C_IDX_DH, C_IDX_DR), C_IDX_HEADS, C_IDX_DH)],
                          -1).astype(MXU_DT)
    wuk = w_uk.transpose(1, 0, 2).astype(MXU_DT)
    inv_q, sgn_q = _rope_rows(C_DR)
    inv_i, sgn_i = _rope_rows(C_IDX_DR)
    rope = jnp.stack([jnp.tile(v, LANES // C_DR) for v in (inv_q, sgn_q, inv_i, sgn_i)])
    rope = jnp.concatenate([rope, jnp.zeros((SUBLANES - 4, LANES), f32)])
    hh = C_IDX_DR // 2
    src = np.arange(LANES)
    src[:hh] += hh
    src[hh:C_IDX_DR] -= hh
    perm = np.zeros((LANES, LANES), np.float32)
    perm[src, np.arange(LANES)] = 1.0
    ikg = _pad_cols(ik_g[None, :], LANES)
    ikb = _pad_cols(ik_b[None, :], LANES)
    out_shape = (
        jax.ShapeDtypeStruct((B, C_HEADS, T, C_KDIM), MXU_DT),
        jax.ShapeDtypeStruct((B, C_HEADS, T, C_KDIM), MXU_DT),
        jax.ShapeDtypeStruct((B, T, C_KVL), MXU_DT),
        jax.ShapeDtypeStruct((B, C_IDX_HEADS, T, LANES), MXU_DT),
        jax.ShapeDtypeStruct((B, T, LANES), MXU_DT),
        jax.ShapeDtypeStruct((B, T, LANES), f32),
    )
    row = lambda w: pl.BlockSpec((1, tm, w), lambda b, i: (b, i, 0))
    hm = lambda w: pl.BlockSpec((1, C_HEADS, tm, w), lambda b, i: (b, 0, i, 0))
    return pl.pallas_call(
        _odd_prep_kernel,
        grid=(B, T // tm),
        in_specs=[row(D), row(1), _const_spec(win.shape), _const_spec(wqb.shape), _const_spec(wiq.shape),
                  _const_spec(wuk.shape), _const_spec((1, C_QL)), _const_spec((1, C_KVL)),
                  _const_spec((1, LANES)), _const_spec((1, LANES)), _const_spec(rope.shape),
                  _const_spec(perm.shape)],
        out_specs=(hm(C_KDIM), hm(C_KDIM), row(C_KVL), hm(LANES), row(LANES), row(LANES)),
        out_shape=out_shape,
        compiler_params=_cparams("parallel", "parallel"),
    )(h3, pos3, win, wqb, wiq, wuk, q_norm[None, :], kv_norm[None, :], ikg, ikb, rope, jnp.asarray(perm))


def _dsa_kernel(qa_ref, qi_ref, wi_ref, kh_ref, kv_ref, ki_ref, wuv_ref, eye_ref, tri_ref, out_ref,
                key_scr, hi_scr, lo_scr, acc_scr, *, QB, KC, SUB, TOPK, HG):
    f32, i32, i16 = jnp.float32, jnp.int32, jnp.int16
    i = pl.program_id(1)
    s0 = i * QB
    H = C_HEADS
    NG = H // HG
    RG = HG * QB
    nch = (s0 + QB + KC - 1) // KC
    PK = PACKED_ROWS
    tq_row = s0 + _iota((1, QB), 1)
    w_t = wi_ref[0].T
    one, zero = jnp.ones((), MXU_DT), jnp.zeros((), MXU_DT)

    def idx_body(c, _):
        ks = pl.multiple_of(c * KC, KC)
        kic = ki_ref[0, pl.ds(ks, KC), :]
        isc = None
        for hd in range(C_IDX_HEADS):
            s = jnp.maximum(_dot_nt(kic, qi_ref[0, hd]), 0.0) * w_t[hd:hd + 1, :]
            isc = s if isc is None else isc + s
        isc = jnp.where(isc == 0.0, 0.0, isc)
        kpos = ks + _iota((KC, 1), 0)
        isc = jnp.where(kpos <= tq_row, isc, NEG)
        bits = lax.bitcast_convert_type(isc, i32)
        key = jnp.where(bits < 0, bits ^ jnp.int32(0x7FFFFFFF), bits)
        key_scr[c] = key
        k3 = key.reshape(KC // PK, PK, QB)
        hi_scr[c] = lax.shift_right_arithmetic(k3, 16).astype(i16)
        lo_scr[c] = ((k3 & 0xFFFF) - 32768).astype(i16)
        return 0

    lax.fori_loop(0, nch // 2, lambda c2, z: idx_body(2 * c2 + 1, idx_body(2 * c2, z)), 0)
    lax.cond(nch % 2 == 1, lambda z: idx_body(nch - 1, z), lambda z: z, 0)

    def rep16(v):
        return jnp.broadcast_to(v, (PK, QB)).astype(i16)[None]

    def select_threshold(nk):
        def count16(pred):
            accs = [jnp.zeros((PK, QB), MXU_DT) for _ in range(4)]
            for c in range(nk):
                x = jnp.where(pred(hi_scr[c], lo_scr[c]), one, zero)
                for r in range(KC // PK):
                    accs[r % len(accs)] = accs[r % len(accs)] + x[r]
            acc = (accs[0] + accs[1]) + (accs[2] + accs[3])
            return jnp.sum(acc.astype(f32), axis=0, keepdims=True)

        def bisect(pick, base):
            def body(b, t):
                cand = t + lax.shift_left(jnp.int32(1), 15 - b)
                c16 = rep16(cand)
                return jnp.where(base + count16(lambda h, l: pick(h, l) >= c16) >= TOPK, cand, t)
            return lax.fori_loop(0, 16, body, jnp.full((1, QB), -32768, i32))

        thi = bisect(lambda h, l: h, 0.0)
        thi16 = rep16(thi)
        for c in range(nk):
            lo_scr[c] = jnp.where(hi_scr[c] == thi16, lo_scr[c], jnp.full((), -32768, i16))
        n_hi = count16(lambda h, l: h > thi16)
        tlo = bisect(lambda h, l: l, n_hi)
        tlo16 = rep16(tlo)
        n_gt = n_hi + count16(lambda h, l: l > tlo16)
        n_eq = count16(lambda h, l: (h == thi16) & (l == tlo16))
        return thi, tlo, n_gt, n_eq

    n_variants = kv_ref.shape[1] // KC
    thi, tlo, n_gt, n_eq = lax.switch(nch - 1, [functools.partial(select_threshold, k + 1) for k in range(n_variants)])
    thr = thi * 65536 + (tlo + 32768)
    room = TOPK - n_gt

    @pl.when(jnp.max(jnp.where(n_eq > room, 1, 0)) > 0)
    def _():
        def body(c, before):
            key = key_scr[c]
            eq = key == thr
            seen = before + _dot(tri_ref[...], jnp.where(eq, 1.0, 0.0).astype(MXU_DT))
            key_scr[c] = jnp.where(eq & (seen > room), key - 1, key)
            return seen[KC - 1:KC, :]
        lax.fori_loop(0, nch, body, jnp.zeros((1, QB), f32))

    acc_scr[...] = jnp.zeros(acc_scr.shape, f32)

    def att_body(c, carry):
        ms, ls = list(carry[0]), list(carry[1])
        ks = pl.multiple_of(c * KC, KC)
        key = key_scr[c]
        kpos = ks + _iota((KC, 1), 0)
        sel_t = (key >= thr) & (kpos <= tq_row)
        keep = _dot_nt(eye_ref[...], jnp.where(sel_t, 1.0, 0.0).astype(MXU_DT))
        bias = jnp.where(keep > 0.5, 0.0, NEG)
        for u in range(KC // SUB):
            sub = pl.ds(pl.multiple_of(ks + u * SUB, SUB), SUB)
            kvc = kv_ref[0, sub, :]
            b_u = bias[:, u * SUB:(u + 1) * SUB][None]
            for g in range(NG):
                s = jnp.stack([_dot_nt(qa_ref[0, hd], kh_ref[0, hd, sub, :])
                               for hd in range(g * HG, (g + 1) * HG)]) + b_u
                m_new = jnp.maximum(ms[g], jnp.max(s, axis=-1, keepdims=True))
                p = jnp.exp2(s - m_new)
                alpha = jnp.exp2(ms[g] - m_new)
                ls[g] = alpha * ls[g] + jnp.sum(p, axis=-1, keepdims=True)
                ms[g] = m_new
                rows = slice(g * RG, (g + 1) * RG)
                pv = _dot(p.reshape(RG, SUB).astype(MXU_DT), kvc)
                acc_scr[rows] = alpha.reshape(RG, 1) * acc_scr[rows] + pv
        return tuple(ms), tuple(ls)

    m0 = tuple(jnp.full((HG, QB, 1), SOFTMAX_M0, f32) for _ in range(NG))
    l0 = tuple(jnp.zeros((HG, QB, 1), f32) for _ in range(NG))
    carry = lax.fori_loop(0, nch // 2, lambda c2, cr: att_body(2 * c2 + 1, att_body(2 * c2, cr)), (m0, l0))
    _, l_fin = lax.cond(nch % 2 == 1, lambda cr: att_body(nch - 1, cr), lambda cr: cr, carry)
    for g in range(NG):
        o_lat = (acc_scr[g * RG:(g + 1) * RG] / l_fin[g].reshape(RG, 1)).astype(MXU_DT)
        for k in range(HG):
            hd = g * HG + k
            out_ref[0, :, hd * C_DV:(hd + 1) * C_DV] = _dot(o_lat[k * QB:(k + 1) * QB], wuv_ref[hd]).astype(out_ref.dtype)


def _dsa(qa, kh, kv, qi, ki, wi, w_uv):
    B, H, T, _ = qa.shape
    QB = min(DSA_Q_BLOCK, T)
    KC = min(KEY_CHUNK, T)
    topk = min(C_TOPK, T // 4)
    wuv = w_uv.transpose(1, 0, 2).astype(MXU_DT)
    assert T // PACKED_ROWS <= 256
    eye = jnp.eye(QB, dtype=MXU_DT)
    tri = jnp.tril(jnp.ones((KC, KC), MXU_DT))
    kern = functools.partial(_dsa_kernel, QB=QB, KC=KC, SUB=min(KEY_SUB, KC), TOPK=topk, HG=DSA_HEAD_GROUP)
    half_words = pltpu.VMEM((T // KC, KC // PACKED_ROWS, PACKED_ROWS, QB), jnp.int16)
    return pl.pallas_call(
        kern,
        grid=(B, T // QB),
        in_specs=[pl.BlockSpec((1, H, QB, C_KDIM), lambda b, i: (b, 0, i, 0)),
                  pl.BlockSpec((1, H, QB, LANES), lambda b, i: (b, 0, i, 0)),
                  pl.BlockSpec((1, QB, LANES), lambda b, i: (b, i, 0)),
                  pl.BlockSpec((1, H, T, C_KDIM), lambda b, i: (b, 0, 0, 0), pipeline_mode=pl.Buffered(1)),
                  pl.BlockSpec((1, T, C_KVL), lambda b, i: (b, 0, 0)),
                  pl.BlockSpec((1, T, LANES), lambda b, i: (b, 0, 0)),
                  _const_spec(wuv.shape), _const_spec(eye.shape), _const_spec(tri.shape)],
        out_specs=pl.BlockSpec((1, QB, H * C_DV), lambda b, i: (b, i, 0)),
        out_shape=jax.ShapeDtypeStruct((B, T, H * C_DV), MXU_DT),
        scratch_shapes=[pltpu.VMEM((T // KC, KC, QB), jnp.int32),
                        half_words, half_words,
                        pltpu.VMEM((H * QB, C_KVL), jnp.float32)],
        compiler_params=_cparams("parallel", "arbitrary"),
    )(qa, qi, wi, kh, kv, ki, wuv, eye, tri)


def _post_kernel(*refs, n_mix, n_ff):
    h_ref = refs[0]
    mix = refs[1:1 + 2 * n_mix]
    g1, b1, w1_ref, w2_ref, g2, b2, wg_ref, p_ref, wp_ref, out_ref = refs[1 + 2 * n_mix:]
    tm = h_ref.shape[0]
    nh = 2 if tm % (2 * PACKED_ROWS) == 0 else 1
    for r in range(nh):
        rows = slice(r * (tm // nh), (r + 1) * (tm // nh))
        h = h_ref[rows, :]
        y = _dot(mix[0][rows, :].astype(MXU_DT), mix[1][...])
        for k in range(1, n_mix):
            y = y + _dot(mix[2 * k][rows, :].astype(MXU_DT), mix[2 * k + 1][...])
        h1 = _layer_norm(DN_ALPHA * h + y, g1[...], b1[...])
        h1b = h1.astype(MXU_DT)
        ff = D_FF // n_ff
        u = None
        for k in range(n_ff):
            a = jnp.square(jnp.maximum(_dot(h1b, w1_ref[:, k * ff:(k + 1) * ff]), 0.0))
            t = _dot(a.astype(MXU_DT), w2_ref[k * ff:(k + 1) * ff, :])
            u = t if u is None else u + t
        h2 = _layer_norm(DN_ALPHA * h1 + u, g2[...], b2[...])
        gate = jax.nn.sigmoid(_dot(h2.astype(MXU_DT), wg_ref[...]))
        out_ref[rows, :] = h2 + gate * _dot(p_ref[0, rows, :].astype(MXU_DT), wp_ref[...])


def _post(h2d, mixes, ln1_g, ln1_b, w1, w2, ln2_g, ln2_b, wg, p_all, layer, wp, tm):
    M, D = h2d.shape
    row = lambda w: pl.BlockSpec((tm, w), lambda i: (i, 0))
    vec = lambda v: v[None, :]
    in_specs = [row(D)]
    args = [h2d]
    for x, w in mixes:
        in_specs += [row(x.shape[1]), _const_spec(w.shape)]
        args += [x, w]
    in_specs += [_const_spec((1, D)), _const_spec((1, D)), _const_spec(w1.shape), _const_spec(w2.shape),
                 _const_spec((1, D)), _const_spec((1, D)), _const_spec(wg.shape),
                 pl.BlockSpec((1, tm, D_PLE), lambda i: (layer, i, 0)), _const_spec(wp.shape)]
    args += [vec(ln1_g), vec(ln1_b), w1, w2, vec(ln2_g), vec(ln2_b), wg, p_all, wp]
    kern = functools.partial(_post_kernel, n_mix=len(mixes), n_ff=4)
    return pl.pallas_call(
        kern,
        grid=(M // tm,),
        in_specs=in_specs,
        out_specs=row(D),
        out_shape=jax.ShapeDtypeStruct((M, D), jnp.float32),
        compiler_params=_cparams("parallel"),
    )(*args)


def kernel(x, p, positions, e_w_in, e_a_conv, e_a_i_b, e_a_f_b, e_a_norm, e_b_cmp_pos, e_b_cmp_w1, e_b_cmp_w2, e_b_g_b, e_w_out, o_w_in, o_q_norm, o_kv_norm, o_w_qb, o_w_uk, o_w_uv, o_w_iq, o_ik_g, o_ik_b, o_w_out, ln1_g, ln1_b, ln2_g, ln2_b, mlp_w1, mlp_w2, ple_gate_w, ple_w):
    B, T, D = x.shape
    M = B * T
    tm = min(ROW_TILE, T)
    h = x
    pos3 = positions[..., None]
    p_all = p.reshape(DEPTH, M, D_PLE)
    bf = lambda w: w.astype(MXU_DT)
    for i in range(DEPTH):
        j = i // 2
        if i % 2 == 0:
            qk, av, ao, small, bq, bc, bs = _even_proj(h, _even_w_in_aug(e_w_in[j]), tm)
            ya = _mlstm(qk, av, ao, small, e_a_conv[j], e_a_i_b[j], e_a_f_b[j], e_a_norm[j])
            cmp = _nsa_compress(bc, e_b_cmp_pos[j], e_b_cmp_w1[j], e_b_cmp_w2[j])
            yb = _nsa(bq, bs, cmp, small, e_b_g_b[j])
            w_out = e_w_out[j]
            mixes = [(ya.reshape(M, A_W), bf(w_out[:A_W])),
                     (yb.reshape(M, B_HEADS * LANES), bf(_pad_heads(w_out[A_W:], B_HEADS, B_DH, axis=0)))]
        else:
            qa, kh, kv, qi, ki, wi = _odd_prep(h, pos3, o_w_in[j], o_q_norm[j], o_kv_norm[j], o_w_qb[j], o_w_uk[j],
                                               o_w_iq[j], o_ik_g[j], o_ik_b[j], tm)
            o = _dsa(qa, kh, kv, qi, ki, wi, o_w_uv[j])
            mixes = [(o.reshape(M, C_HEADS * C_DV), bf(o_w_out[j]))]
        h = _post(h.reshape(M, D), mixes, ln1_g[i], ln1_b[i], bf(mlp_w1[i]), bf(mlp_w2[i]), ln2_g[i], ln2_b[i],
                  bf(ple_gate_w[i]), p_all, i, bf(ple_w[i]), min(MLP_ROW_TILE, T)).reshape(B, T, D)
    return h
```
